```python
import jax, jax.numpy as jnp
from jax import lax
import numpy as np


D_MODEL = 2048
BATCH = 2
SEQ = 4096
DEPTH = 2

GRID_W = 64
CTX_LEN = 256
HEAD_DIM = 128
N_HEADS = D_MODEL // HEAD_DIM
A_KV_HEADS = 4
A_Q_BLOCK = 128
ROPE_THETA = 10000.0
NA_WIN_ROWS = 8
NA_WIN_COLS = 16
N_EXPERTS = 32
TOP_K = 4
D_EXPERT = D_MODEL
SWIGLU_ALPHA = 1.702
SWIGLU_LIMIT = 7.0
NORM_EPS = 1e-6
N_MIXERS = 2
N_A_LAYERS = (DEPTH + N_MIXERS - 1) // N_MIXERS
N_B_LAYERS = DEPTH // N_MIXERS
N_MOD = 6

kernel_name = 'hybrid_flow_gqa_natten_moe'


def rms_norm(x, g):
    x32 = x.astype(jnp.float32)
    y = x32 * lax.rsqrt(jnp.mean(x32 * x32, axis=-1, keepdims=True) + NORM_EPS)
    return (y * g.astype(jnp.float32)).astype(x.dtype)


def split_heads(t, n_heads):
    b, n_tok, _ = t.shape
    return t.reshape(b, n_tok, n_heads, HEAD_DIM).transpose(0, 2, 1, 3)


def merge_heads(t):
    b, h, n_tok, d = t.shape
    return t.transpose(0, 2, 1, 3).reshape(b, n_tok, h * d)


def rope_1d(x, pos):
    m = x.shape[-1] // 2
    inv_freq = ROPE_THETA ** (-jnp.arange(m, dtype=jnp.float32) / m)
    ang = pos.astype(jnp.float32)[:, None] * inv_freq[None, :]
    cos = jnp.cos(ang).astype(x.dtype)
    sin = jnp.sin(ang).astype(x.dtype)
    x1, x2 = x[..., :m], x[..., m:]
    return jnp.concatenate([x1 * cos - x2 * sin, x2 * cos + x1 * sin], axis=-1)


def axial_rope(x):
    n_tok = x.shape[-2]
    t = jnp.arange(n_tok, dtype=jnp.int32)
    half = x.shape[-1] // 2
    return jnp.concatenate([rope_1d(x[..., :half], t // GRID_W),
                            rope_1d(x[..., half:], t % GRID_W)], axis=-1)


def grouped_softmax_attention(q, k, v):
    s = jnp.einsum('bkgqd,bksd->bkgqs', q, k).astype(jnp.float32) * (HEAD_DIM ** -0.5)
    p = jax.nn.softmax(s, axis=-1).astype(v.dtype)
    return jnp.einsum('bkgqs,bksd->bkgqd', p, v)


def gqa_axial_attention(hx, hc, w_qkv, w_o, q_gain, k_gain, with_ctx):
    B, S, _ = hx.shape
    L = hc.shape[1]
    G = N_HEADS // A_KV_HEADS
    nq = N_HEADS * HEAD_DIM
    nkv = A_KV_HEADS * HEAD_DIM
    qkv_x = hx @ w_qkv
    qx = axial_rope(rms_norm(split_heads(qkv_x[..., :nq], N_HEADS), q_gain))
    kx = axial_rope(rms_norm(split_heads(qkv_x[..., nq:nq + nkv], A_KV_HEADS), k_gain))
    vx = split_heads(qkv_x[..., nq + nkv:], A_KV_HEADS)
    if with_ctx:
        qkv_c = hc @ w_qkv
        qc = rms_norm(split_heads(qkv_c[..., :nq], N_HEADS), q_gain)
        kv_c = qkv_c[..., nq:]
    else:
        kv_c = hc @ w_qkv[:, nq:]
    kc = rms_norm(split_heads(kv_c[..., :nkv], A_KV_HEADS), k_gain)
    vc = split_heads(kv_c[..., nkv:], A_KV_HEADS)
    k_all = jnp.concatenate([kc, kx], axis=2)
    v_all = jnp.concatenate([vc, vx], axis=2)
    nb = S // A_Q_BLOCK
    qb = jnp.moveaxis(qx.reshape(B, A_KV_HEADS, G, nb, A_Q_BLOCK, HEAD_DIM), 3, 0)
    ob = lax.map(lambda q: grouped_softmax_attention(q, k_all, v_all), qb)
    ox = jnp.moveaxis(ob, 0, 3).reshape(B, N_HEADS, S, HEAD_DIM)
    yx = merge_heads(ox) @ w_o
    if with_ctx:
        oc = grouped_softmax_attention(qc.reshape(B, A_KV_HEADS, G, L, HEAD_DIM), kc, vc)
        yc = merge_heads(oc.reshape(B, N_HEADS, L, HEAD_DIM)) @ w_o
        return yx, yc
    return yx, None


def neighborhood_attention(hx, hc, w_qkv, w_o, q_gain, k_gain, rel_bias, with_ctx):
    B, S, _ = hx.shape
    L = hc.shape[1]
    rows = S // GRID_W
    wr = min(NA_WIN_ROWS, rows)
    inner = N_HEADS * HEAD_DIM
    qkv_x = hx @ w_qkv
    qx = rms_norm(split_heads(qkv_x[..., :inner], N_HEADS), q_gain)
    kx = rms_norm(split_heads(qkv_x[..., inner:2 * inner], N_HEADS), k_gain)
    vx = split_heads(qkv_x[..., 2 * inner:], N_HEADS)
    if with_ctx:
        qkv_c = hc @ w_qkv
        qc = rms_norm(split_heads(qkv_c[..., :inner], N_HEADS), q_gain)
        kv_c = qkv_c[..., inner:]
    else:
        kv_c = hc @ w_qkv[:, inner:]
    kc = rms_norm(split_heads(kv_c[..., :inner], N_HEADS), k_gain)
    vc = split_heads(kv_c[..., inner:], N_HEADS)

    kg = kx.reshape(B, N_HEADS, rows, GRID_W, HEAD_DIM)
    vg = vx.reshape(B, N_HEADS, rows, GRID_W, HEAD_DIM)
    q_rows = jnp.moveaxis(qx.reshape(B, N_HEADS, rows, GRID_W, HEAD_DIM), 2, 0)
    col = jnp.arange(GRID_W, dtype=jnp.int32)
    c_start = jnp.clip(col - NA_WIN_COLS // 2, 0, GRID_W - NA_WIN_COLS)
    col_mask = (col[None, :] >= c_start[:, None]) & (col[None, :] < c_start[:, None] + NA_WIN_COLS)
    dc_idx = jnp.clip(col[None, :] - col[:, None], -(NA_WIN_COLS - 1), NA_WIN_COLS - 1) + NA_WIN_COLS - 1
    bias_tab = rel_bias.astype(jnp.float32)
    scale = HEAD_DIM ** -0.5

    def row_block(args):
        q_r, r = args
        r0 = jnp.clip(r - wr // 2, 0, rows - wr)
        kb = lax.dynamic_slice_in_dim(kg, r0, wr, axis=2)
        vb = lax.dynamic_slice_in_dim(vg, r0, wr, axis=2)
        dr_idx = r0 + jnp.arange(wr, dtype=jnp.int32) - r + NA_WIN_ROWS - 1
        bias = jnp.take(jnp.take(bias_tab, dr_idx, axis=1), dc_idx, axis=2)
        bias = bias.transpose(0, 2, 1, 3)
        s_loc = jnp.einsum('bhqd,bhrkd->bhqrk', q_r, kb).astype(jnp.float32) * scale + bias[None]
        s_loc = jnp.where(col_mask[:, None, :], s_loc, -jnp.inf)
        s_ctx = jnp.einsum('bhqd,bhld->bhql', q_r, kc).astype(jnp.float32) * scale
        s = jnp.concatenate([s_loc.reshape(B, N_HEADS, GRID_W, wr * GRID_W), s_ctx], axis=-1)
        p = jax.nn.softmax(s, axis=-1).astype(vb.dtype)
        p_loc = p[..., :wr * GRID_W].reshape(B, N_HEADS, GRID_W, wr, GRID_W)
        p_ctx = p[..., wr * GRID_W:]
        return (jnp.einsum('bhqrk,bhrkd->bhqd', p_loc, vb)
                + jnp.einsum('bhql,bhld->bhqd', p_ctx, vc))

    o_rows = lax.map(row_block, (q_rows, jnp.arange(rows, dtype=jnp.int32)))
    ox = jnp.moveaxis(o_rows, 0, 2).reshape(B, N_HEADS, S, HEAD_DIM)
    yx = merge_heads(ox) @ w_o
    if with_ctx:
        oc = grouped_softmax_attention(qc[:, :, None], kc, vc)[:, :, 0]
        yc = merge_heads(oc) @ w_o
        return yx, yc
    return yx, None


def moe_clamped_swiglu(h, router_w, router_b, w1, b1, w2, b2):
    n_tok = h.shape[0]
    logits = (h @ router_w + router_b).astype(jnp.float32)
    top_v, top_i = lax.top_k(logits, TOP_K)
    top_w = jax.nn.softmax(top_v, axis=-1)
    gates = jnp.zeros((n_tok, N_EXPERTS), jnp.float32).at[
        jnp.arange(n_tok)[:, None], top_i].set(top_w)
    out = jnp.zeros(h.shape, jnp.float32)
    for e in range(N_EXPERTS):
        a = (h @ w1[e] + b1[e]).reshape(n_tok, D_EXPERT, 2)
        glu = jnp.minimum(a[..., 0], SWIGLU_LIMIT)
        lin = jnp.clip(a[..., 1], -SWIGLU_LIMIT, SWIGLU_LIMIT)
        act = glu * jax.nn.sigmoid(SWIGLU_ALPHA * glu) * (lin + 1.0)
        out = out + gates[:, e:e + 1] * (act @ w2[e] + b2[e]).astype(jnp.float32)
    return out.astype(h.dtype)


def modulate(h, shift, scale):
    return h * (1.0 + scale) + shift


def setup_inputs(seed: int = 0) -> dict:
    key = jax.random.key(seed)
    ks = jax.random.split(key, 24)
    D = D_MODEL
    inner = N_HEADS * HEAD_DIM
    a_qkv = (N_HEADS + 2 * A_KV_HEADS) * HEAD_DIM

    def nrm(k, shape, s):
        return jax.random.normal(k, shape, jnp.float32) * s

    return {
        'x': nrm(ks[0], (BATCH, SEQ, D), 1.0),
        'c': nrm(ks[1], (BATCH, D), 1.0),
        'ctx': nrm(ks[2], (BATCH, CTX_LEN, D), 1.0),
        'c_ctx': nrm(ks[3], (D,), 1.0),
        'ada_w': nrm(ks[4], (DEPTH, D, N_MOD * D), 0.5 * D ** -0.5),
        'ada_b': nrm(ks[5], (DEPTH, N_MOD * D), 0.01),
        'norm_mix_g': 1.0 + nrm(ks[6], (DEPTH, D), 0.05),
        'norm_ffn_g': 1.0 + nrm(ks[7], (DEPTH, D), 0.05),
        'a_w_qkv': nrm(ks[8], (N_A_LAYERS, D, a_qkv), D ** -0.5),
        'a_w_o': nrm(ks[9], (N_A_LAYERS, inner, D), inner ** -0.5),
        'a_q_gain': 1.0 + nrm(ks[10], (N_A_LAYERS, HEAD_DIM), 0.05),
        'a_k_gain': 1.0 + nrm(ks[11], (N_A_LAYERS, HEAD_DIM), 0.05),
        'b_w_qkv': nrm(ks[12], (N_B_LAYERS, D, 3 * inner), D ** -0.5),
        'b_w_o': nrm(ks[13], (N_B_LAYERS, inner, D), inner ** -0.5),
        'b_q_gain': 1.0 + nrm(ks[14], (N_B_LAYERS, HEAD_DIM), 0.05),
        'b_k_gain': 1.0 + nrm(ks[15], (N_B_LAYERS, HEAD_DIM), 0.05),
        'b_rel_bias': nrm(ks[16], (N_B_LAYERS, N_HEADS, 2 * NA_WIN_ROWS - 1, 2 * NA_WIN_COLS - 1), 0.1),
        'router_w': nrm(ks[17], (DEPTH, D, N_EXPERTS), D ** -0.5),
        'router_b': nrm(ks[18], (DEPTH, N_EXPERTS), 0.01),
        'exp_w1': nrm(ks[19], (DEPTH, N_EXPERTS, D, 2 * D_EXPERT), D ** -0.5),
        'exp_b1': nrm(ks[20], (DEPTH, N_EXPERTS, 2 * D_EXPERT), 0.01),
        'exp_w2': nrm(ks[21], (DEPTH, N_EXPERTS, D_EXPERT, D), D_EXPERT ** -0.5),
        'exp_b2': nrm(ks[22], (DEPTH, N_EXPERTS, D), 0.01),
    }


def reference(x, c, ctx, c_ctx, ada_w, ada_b, norm_mix_g, norm_ffn_g,
              a_w_qkv, a_w_o, a_q_gain, a_k_gain,
              b_w_qkv, b_w_o, b_q_gain, b_k_gain, b_rel_bias,
              router_w, router_b, exp_w1, exp_b1, exp_w2, exp_b2):
    B, S, D = x.shape
    L = ctx.shape[1]
    for i in range(DEPTH):
        last = i == DEPTH - 1
        mod_x = (jax.nn.silu(c) @ ada_w[i] + ada_b[i])[:, None, :]
        mod_c = (jax.nn.silu(c_ctx) @ ada_w[i] + ada_b[i])[None, None, :]
        sh1x, sc1x, g1x, sh2x, sc2x, g2x = jnp.split(mod_x, N_MOD, axis=-1)
        sh1c, sc1c, g1c, sh2c, sc2c, g2c = jnp.split(mod_c, N_MOD, axis=-1)

        hx = modulate(rms_norm(x, norm_mix_g[i]), sh1x, sc1x)
        hc = modulate(rms_norm(ctx, norm_mix_g[i]), sh1c, sc1c)
        j = i // N_MIXERS
        if i % N_MIXERS == 0:
            yx, yc = gqa_axial_attention(hx, hc, a_w_qkv[j], a_w_o[j], a_q_gain[j], a_k_gain[j],
                                         not last)
        else:
            yx, yc = neighborhood_attention(hx, hc, b_w_qkv[j], b_w_o[j], b_q_gain[j], b_k_gain[j],
                                            b_rel_bias[j], not last)
        x = x + g1x * yx

        hx = modulate(rms_norm(x, norm_ffn_g[i]), sh2x, sc2x)
        if last:
            yx = moe_clamped_swiglu(hx.reshape(B * S, D), router_w[i], router_b[i],
                                    exp_w1[i], exp_b1[i], exp_w2[i], exp_b2[i]).reshape(B, S, D)
        else:
            ctx = ctx + g1c * yc
            hc = modulate(rms_norm(ctx, norm_ffn_g[i]), sh2c, sc2c)
            y_all = moe_clamped_swiglu(
                jnp.concatenate([hx.reshape(B * S, D), hc.reshape(B * L, D)], axis=0),
                router_w[i], router_b[i], exp_w1[i], exp_b1[i], exp_w2[i], exp_b2[i])
            yx = y_all[:B * S].reshape(B, S, D)
            ctx = ctx + g2c * y_all[B * S:].reshape(B, L, D)
        x = x + g2x * yx
    return x
```

```python
import functools
import math

import jax
import jax.numpy as jnp
from jax import lax
from jax.experimental import pallas as pl
from jax.experimental.pallas import tpu as pltpu

F32 = jnp.float32
BF16 = jnp.bfloat16
I32 = jnp.int32

LANES = 128
SUBLANES = 8
VMEM_BYTES_V7X = 64 * 1024 * 1024

HEAD_DIM = 128
GRID_W = 64
A_KV_HEADS = 4
NA_WIN_ROWS = 8
NA_WIN_COLS = 16
N_EXPERTS = 32
TOP_K = 4
ROPE_THETA = 10000.0
SWIGLU_ALPHA = 1.702
SWIGLU_LIMIT = 7.0
NORM_EPS = 1e-6
N_MOD = 6
LOG2E = math.log2(math.e)
NEG_INF = float("-inf")

MOE_TILE = 256


def _vmem_limit(nbytes):
    return int(min(nbytes + 8 * 1024 * 1024, VMEM_BYTES_V7X - 8 * 1024 * 1024))


def _params(semantics, vmem_bytes):
    return pltpu.CompilerParams(dimension_semantics=semantics,
                                vmem_limit_bytes=_vmem_limit(vmem_bytes))


def _ada_kernel(c_ref, w_ref, b_ref, o_ref):
    c = c_ref[...]
    a = (c * jax.nn.sigmoid(c)).astype(BF16)
    o_ref[0] = jnp.dot(a, w_ref[0].astype(BF16), preferred_element_type=F32) + b_ref[0]


def _ada_modulation(c_rows, ada_w, ada_b):
    depth, d, n = ada_w.shape
    rows = c_rows.shape[0]
    tn = 1024
    return pl.pallas_call(
        _ada_kernel,
        grid=(depth, n // tn),
        in_specs=[pl.BlockSpec((rows, d), lambda l, j: (0, 0)),
                  pl.BlockSpec((1, d, tn), lambda l, j: (l, 0, j)),
                  pl.BlockSpec((1, 1, tn), lambda l, j: (l, 0, j))],
        out_specs=pl.BlockSpec((1, rows, tn), lambda l, j: (l, 0, j)),
        out_shape=jax.ShapeDtypeStruct((depth, rows, n), F32),
        compiler_params=_params(("arbitrary", "arbitrary"), 2 * d * tn * 4 + d * tn * 2),
        name="ada_modulation",
    )(c_rows, ada_w, ada_b.reshape(depth, 1, n))


def _norm_modulate(x, g, shift, scale):
    ms = jnp.mean(x * x, axis=-1, keepdims=True)
    y = x * lax.rsqrt(ms + NORM_EPS) * g
    return y * (1.0 + scale) + shift


def _qkv_kernel(flag_ref, x_ref, sh_ref, sc_ref, g_ref, w_ref, gain_ref, cos_ref, sin_ref,
                o_ref, h_ref, *, rope):
    n = pl.program_id(2)

    @pl.when(n == 0)
    def _():
        h_ref[...] = _norm_modulate(x_ref[...], g_ref[...], sh_ref[...], sc_ref[...]).astype(BF16)

    a = jnp.dot(h_ref[...], w_ref[...].astype(BF16), preferred_element_type=F32)
    tn = a.shape[1]

    @pl.when(flag_ref[n] == 0)
    def _():
        o_ref[...] = a.astype(BF16)

    @pl.when(flag_ref[n] == 1)
    def _():
        lane = lax.broadcasted_iota(I32, (1, HEAD_DIM), 1)
        first_half = (lane % (HEAD_DIM // 2)) < (HEAD_DIM // 4)
        for hh in range(tn // HEAD_DIM):
            sl = slice(hh * HEAD_DIM, (hh + 1) * HEAD_DIM)
            ah = a[:, sl]
            ms = jnp.mean(ah * ah, axis=-1, keepdims=True)
            yh = ah * lax.rsqrt(ms + NORM_EPS) * gain_ref[:, sl]
            if rope:
                partner = jnp.where(first_half,
                                    pltpu.roll(yh, HEAD_DIM - HEAD_DIM // 4, 1),
                                    pltpu.roll(yh, HEAD_DIM // 4, 1))
                yh = yh * cos_ref[...] + partner * sin_ref[...]
            o_ref[:, sl] = yh.astype(BF16)


def _qkv_project(x, shift, scale, g, w, gain_cols, norm_flags, cos, sin, *, rope, tm):
    b, t, d = x.shape
    n = w.shape[1]
    tn = 512
    kernel = functools.partial(_qkv_kernel, rope=rope)
    grid_spec = pltpu.PrefetchScalarGridSpec(
        num_scalar_prefetch=1,
        grid=(b, t // tm, n // tn),
        in_specs=[pl.BlockSpec((None, tm, d), lambda bi, m, j, f: (bi, m, 0)),
                  pl.BlockSpec((None, 1, d), lambda bi, m, j, f: (bi, 0, 0)),
                  pl.BlockSpec((None, 1, d), lambda bi, m, j, f: (bi, 0, 0)),
                  pl.BlockSpec((1, d), lambda bi, m, j, f: (0, 0)),
                  pl.BlockSpec((d, tn), lambda bi, m, j, f: (0, j)),
                  pl.BlockSpec((1, tn), lambda bi, m, j, f: (0, j)),
                  pl.BlockSpec((tm, HEAD_DIM), lambda bi, m, j, f: (m, 0)),
                  pl.BlockSpec((tm, HEAD_DIM), lambda bi, m, j, f: (m, 0))],
        out_specs=pl.BlockSpec((None, tm, tn), lambda bi, m, j, f: (bi, m, j)),
        scratch_shapes=[pltpu.VMEM((tm, d), BF16)])
    vmem = 2 * tm * d * 4 + tm * d * 2 + 2 * d * tn * 4 + d * tn * 2 + 4 * tm * tn * 4
    return pl.pallas_call(
        kernel, grid_spec=grid_spec,
        out_shape=jax.ShapeDtypeStruct((b, t, n), BF16),
        compiler_params=_params(("arbitrary", "arbitrary", "arbitrary"), vmem),
        name="qkv_project",
    )(norm_flags, x, shift, scale, g, w, gain_cols, cos, sin)


def _nt_dot(a, b):
    return lax.dot_general(a, b, (((1,), (1,)), ((), ())), preferred_element_type=F32)


def _attn_kernel(q_ref, kc_ref, vc_ref, *rest, group, has_x):
    if has_x:
        kx_ref, vx_ref, o_ref = rest
    else:
        (o_ref,) = rest
    kc = kc_ref[...]
    vc = vc_ref[...]
    for g in range(group):
        sl = slice(g * HEAD_DIM, (g + 1) * HEAD_DIM)
        q = q_ref[:, sl]
        sc = _nt_dot(q, kc)
        m = jnp.max(sc, axis=-1, keepdims=True)
        if has_x:
            sx = _nt_dot(q, kx_ref[...])
            m = jnp.maximum(m, jnp.max(sx, axis=-1, keepdims=True))
        pc = jnp.exp2(sc - m)
        l = jnp.sum(pc, axis=-1, keepdims=True)
        acc = jnp.dot(pc.astype(BF16), vc, preferred_element_type=F32)
        if has_x:
            px = jnp.exp2(sx - m)
            l = l + jnp.sum(px, axis=-1, keepdims=True)
            acc = acc + jnp.dot(px.astype(BF16), vx_ref[...], preferred_element_type=F32)
        o_ref[:, sl] = (acc / l).astype(BF16)


def _attention(q_src, ctx_src, x_src, *, n_q_heads, n_kv_heads, k_col0, v_col0, tq):
    b, t_q, _ = q_src.shape
    l = ctx_src.shape[1]
    group = n_q_heads // n_kv_heads
    has_x = x_src is not None
    gw = group * HEAD_DIM
    in_specs = [pl.BlockSpec((None, tq, gw), lambda bi, kv, i: (bi, i, kv)),
                pl.BlockSpec((None, l, HEAD_DIM), lambda bi, kv, i: (bi, 0, k_col0 + kv)),
                pl.BlockSpec((None, l, HEAD_DIM), lambda bi, kv, i: (bi, 0, v_col0 + kv))]
    args = [q_src, ctx_src, ctx_src]
    s = 0
    if has_x:
        s = x_src.shape[1]
        in_specs += [pl.BlockSpec((None, s, HEAD_DIM), lambda bi, kv, i: (bi, 0, k_col0 + kv)),
                     pl.BlockSpec((None, s, HEAD_DIM), lambda bi, kv, i: (bi, 0, v_col0 + kv))]
        args += [x_src, x_src]
    vmem = 4 * (l + s) * HEAD_DIM * 2 + 4 * tq * gw * 2 + 4 * tq * (l + s) * 4
    return pl.pallas_call(
        functools.partial(_attn_kernel, group=group, has_x=has_x),
        grid=(b, n_kv_heads, t_q // tq),
        in_specs=in_specs,
        out_specs=pl.BlockSpec((None, tq, gw), lambda bi, kv, i: (bi, i, kv)),
        out_shape=jax.ShapeDtypeStruct((b, t_q, n_q_heads * HEAD_DIM), BF16),
        compiler_params=_params(("arbitrary", "arbitrary", "arbitrary"), vmem),
        name="attention_x" if has_x else "attention_ctx",
    )(*args)


NA_Q_ROWS = 8
NA_BAND_ROWS = 2 * NA_WIN_ROWS
NA_DR = 2 * NA_WIN_ROWS - 1
NA_DC = 2 * NA_WIN_COLS - 1
NA_PAIR_TILES = 30


def _na_kernel(tab_ref, q_ref, k_ref, v_ref, kc_ref, vc_ref, o_ref, tile_ref, pair_ref, s_ref, *, rows):
    h = pl.program_id(1)
    a = pl.program_id(2)
    nq = NA_Q_ROWS * GRID_W
    nb = NA_BAND_ROWS * GRID_W
    l_ctx = kc_ref.shape[0]

    @pl.when(a == 0)
    def _():
        qc = lax.broadcasted_iota(I32, (GRID_W, LANES), 0)
        kc = lax.broadcasted_iota(I32, (GRID_W, LANES), 1) % GRID_W
        dc = jnp.clip(kc - qc, -(NA_WIN_COLS - 1), NA_WIN_COLS - 1) + NA_WIN_COLS - 1
        c0 = jnp.clip(qc - NA_WIN_COLS // 2, 0, GRID_W - NA_WIN_COLS)
        col_ok = (kc >= c0) & (kc < c0 + NA_WIN_COLS)
        for dr in range(NA_DR):
            acc = jnp.zeros((GRID_W, LANES), F32)
            for d in range(NA_DC):
                acc = jnp.where(dc == d, tab_ref[h * (NA_DR * NA_DC) + dr * NA_DC + d] * LOG2E, acc)
            tile_ref[dr] = jnp.where(col_ok, acc, NEG_INF)
        left = lax.broadcasted_iota(I32, (GRID_W, LANES), 1) < GRID_W
        zero = jnp.zeros((GRID_W, LANES), F32)
        for p in range(NA_PAIR_TILES):
            lo = tile_ref[p - NA_WIN_ROWS] if 0 <= p - NA_WIN_ROWS < NA_DR else zero
            hi = tile_ref[p - NA_WIN_ROWS + 1] if 0 <= p - NA_WIN_ROWS + 1 < NA_DR else zero
            pair_ref[p] = jnp.where(left, lo, hi)

    wr = NA_WIN_ROWS
    kb0 = jnp.clip(a * NA_Q_ROWS - wr // 2, 0, rows - NA_BAND_ROWS)
    k_band = k_ref[pl.ds(pl.multiple_of(kb0 * GRID_W, 4 * GRID_W), nb), :]
    v_band = v_ref[pl.ds(pl.multiple_of(kb0 * GRID_W, 4 * GRID_W), nb), :]
    q = q_ref[...]
    s_loc = _nt_dot(q, k_band)
    s_ref[:, nb:] = _nt_dot(q, kc_ref[...])

    band_row = lax.broadcasted_iota(I32, (1, nb), 1) // GRID_W
    for rq in range(NA_Q_ROWS):
        r = a * NA_Q_ROWS + rq
        r0 = jnp.clip(r - wr // 2, 0, rows - wr)
        lo = r0 - kb0
        row_mask = jnp.where((band_row >= lo) & (band_row < lo + wr), 0.0, NEG_INF)
        j0 = kb0 - r + 2 * wr - 1
        bias = jnp.concatenate([pair_ref[j0 + 2 * i] for i in range(NA_BAND_ROWS // 2)], axis=1)
        rs = slice(rq * GRID_W, (rq + 1) * GRID_W)
        s_ref[rs, :nb] = s_loc[rs, :] + bias + row_mask

    s = s_ref[...]
    m = jnp.max(s, axis=-1, keepdims=True)
    p = jnp.exp2(s - m)
    l = jnp.sum(p, axis=-1, keepdims=True)
    pb = p.astype(BF16)
    acc = (jnp.dot(pb[:, :nb], v_band, preferred_element_type=F32)
           + jnp.dot(pb[:, nb:], vc_ref[...], preferred_element_type=F32))
    o_ref[...] = (acc / l).astype(BF16)
    del nq, l_ctx


def _neighborhood_attention(qkv_x, qkv_c, rel_bias, *, n_heads):
    b, s, _ = qkv_x.shape
    l = qkv_c.shape[1]
    rows = s // GRID_W
    nq = NA_Q_ROWS * GRID_W
    nb = NA_BAND_ROWS * GRID_W
    grid_spec = pltpu.PrefetchScalarGridSpec(
        num_scalar_prefetch=0,
        grid=(b, n_heads, rows // NA_Q_ROWS),
        in_specs=[pl.BlockSpec(memory_space=pltpu.SMEM),
                  pl.BlockSpec((None, nq, HEAD_DIM), lambda bi, h, a: (bi, a, h)),
                  pl.BlockSpec((None, s, HEAD_DIM), lambda bi, h, a: (bi, 0, n_heads + h)),
                  pl.BlockSpec((None, s, HEAD_DIM), lambda bi, h, a: (bi, 0, 2 * n_heads + h)),
                  pl.BlockSpec((None, l, HEAD_DIM), lambda bi, h, a: (bi, 0, n_heads + h)),
                  pl.BlockSpec((None, l, HEAD_DIM), lambda bi, h, a: (bi, 0, 2 * n_heads + h))],
        out_specs=pl.BlockSpec((None, nq, HEAD_DIM), lambda bi, h, a: (bi, a, h)),
        scratch_shapes=[pltpu.VMEM((NA_DR, GRID_W, LANES), F32),
                        pltpu.VMEM((NA_PAIR_TILES, GRID_W, LANES), F32),
                        pltpu.VMEM((nq, nb + l), F32)])
    vmem = 4 * (s + l) * HEAD_DIM * 2 + 4 * nq * (nb + l) * 4 + 2 * 1024 * 1024
    return pl.pallas_call(
        functools.partial(_na_kernel, rows=rows), grid_spec=grid_spec,
        out_shape=jax.ShapeDtypeStruct((b, s, n_heads * HEAD_DIM), BF16),
        compiler_params=_params(("arbitrary", "arbitrary", "arbitrary"), vmem),
        name="neighborhood_attention",
    )(rel_bias.reshape(-1), qkv_x, qkv_x, qkv_x, qkv_c, qkv_c)


def _oproj_kernel(o_ref, w_ref, x_ref, g_ref, out_ref):
    y = jnp.dot(o_ref[...], w_ref[...].astype(BF16), preferred_element_type=F32)
    out_ref[...] = x_ref[...] + g_ref[...] * y


def _out_project(o, w, x, gate, *, tm):
    b, t, k = o.shape
    d = w.shape[1]
    tn = 512
    vmem = 2 * tm * k * 2 + 2 * k * tn * 4 + k * tn * 2 + 6 * tm * tn * 4
    return pl.pallas_call(
        _oproj_kernel,
        grid=(b, t // tm, d // tn),
        in_specs=[pl.BlockSpec((None, tm, k), lambda bi, m, j: (bi, m, 0)),
                  pl.BlockSpec((k, tn), lambda bi, m, j: (0, j)),
                  pl.BlockSpec((None, tm, tn), lambda bi, m, j: (bi, m, j)),
                  pl.BlockSpec((None, 1, tn), lambda bi, m, j: (bi, 0, j))],
        out_specs=pl.BlockSpec((None, tm, tn), lambda bi, m, j: (bi, m, j)),
        out_shape=jax.ShapeDtypeStruct((b, t, d), F32),
        compiler_params=_params(("arbitrary", "arbitrary", "arbitrary"), vmem),
        name="out_project",
    )(o, w, x, gate)


ROUTER_BLOCK = 256


def _router_kernel(x_ref, sh_ref, sc_ref, g_ref, rw_ref, rb_ref, cnt_in_ref,
                   hp_ref, idx_ref, wgt_ref, rank_ref, cnt_ref, run_ref):
    i = pl.program_id(0)

    @pl.when(i == 0)
    def _():
        run_ref[...] = cnt_in_ref[...]

    h = _norm_modulate(x_ref[...], g_ref[...], sh_ref[...], sc_ref[...])
    tb, d = h.shape
    hp_ref[...] = pltpu.pack_elementwise([h[:, :d // 2], h[:, d // 2:]], packed_dtype=BF16)

    logits = jnp.dot(h, rw_ref[...], preferred_element_type=F32,
                     precision=lax.Precision.HIGHEST) + rb_ref[...]
    lane = lax.broadcasted_iota(I32, (tb, LANES), 1)
    lane_f = lane.astype(F32)
    work = logits
    idx_acc = jnp.zeros((tb, LANES), F32)
    val_acc = jnp.full((tb, LANES), NEG_INF, F32)
    hots = []
    for k in range(TOP_K):
        m = jnp.max(work, axis=-1, keepdims=True)
        idx = jnp.min(jnp.where(work == m, lane_f, float(LANES)), axis=-1, keepdims=True)
        hot = lane_f == idx
        hots.append(hot)
        idx_acc = jnp.where(lane == k, idx, idx_acc)
        val_acc = jnp.where(lane == k, m, val_acc)
        work = jnp.where(hot, NEG_INF, work)

    e = jnp.exp(val_acc - jnp.max(val_acc, axis=-1, keepdims=True))
    wgt_ref[...] = e / jnp.sum(e, axis=-1, keepdims=True)
    idx_ref[...] = idx_acc.astype(I32)

    chosen = jnp.zeros((tb, LANES), F32)
    for hot in hots:
        chosen = chosen + jnp.where(hot, 1.0, 0.0)
    row = lax.broadcasted_iota(I32, (tb, tb), 0)
    col = lax.broadcasted_iota(I32, (tb, tb), 1)
    earlier = jnp.where(col < row, 1.0, 0.0).astype(BF16)
    before = jnp.dot(earlier, chosen.astype(BF16), preferred_element_type=F32) + run_ref[...]
    rank_acc = jnp.zeros((tb, LANES), F32)
    for k, hot in enumerate(hots):
        rk = jnp.sum(jnp.where(hot, before, 0.0), axis=-1, keepdims=True)
        rank_acc = jnp.where(lane == k, rk, rank_acc)
    rank_ref[...] = rank_acc.astype(I32)
    run_ref[...] = run_ref[...] + jnp.sum(chosen, axis=0, keepdims=True)
    cnt_ref[...] = run_ref[...]


def _router(x2d, shift, scale, g, rw_pad, rb_pad, counts_in, *, rows_per_mod):
    n, d = x2d.shape
    tb = ROUTER_BLOCK
    per = rows_per_mod // tb
    vmem = 2 * tb * d * 4 + 6 * tb * d * 4 + 2 * d * LANES * 4
    outs = pl.pallas_call(
        _router_kernel,
        grid=(n // tb,),
        in_specs=[pl.BlockSpec((tb, d), lambda i: (i, 0)),
                  pl.BlockSpec((None, 1, d), lambda i: (i // per, 0, 0)),
                  pl.BlockSpec((None, 1, d), lambda i: (i // per, 0, 0)),
                  pl.BlockSpec((1, d), lambda i: (0, 0)),
                  pl.BlockSpec((d, LANES), lambda i: (0, 0)),
                  pl.BlockSpec((1, LANES), lambda i: (0, 0)),
                  pl.BlockSpec((1, LANES), lambda i: (0, 0))],
        out_specs=[pl.BlockSpec((tb, d // 2), lambda i: (i, 0)),
                   pl.BlockSpec((tb, LANES), lambda i: (i, 0)),
                   pl.BlockSpec((tb, LANES), lambda i: (i, 0)),
                   pl.BlockSpec((tb, LANES), lambda i: (i, 0)),
                   pl.BlockSpec((1, LANES), lambda i: (0, 0))],
        out_shape=[jax.ShapeDtypeStruct((n, d // 2), jnp.uint32),
                   jax.ShapeDtypeStruct((n, LANES), I32),
                   jax.ShapeDtypeStruct((n, LANES), F32),
                   jax.ShapeDtypeStruct((n, LANES), I32),
                   jax.ShapeDtypeStruct((1, LANES), F32)],
        scratch_shapes=[pltpu.VMEM((1, LANES), F32)],
        compiler_params=_params(("arbitrary",), vmem),
        name="router",
    )(x2d, shift, scale, g, rw_pad, rb_pad, counts_in)
    return outs


def _dispatch_kernel(fill_ref, slot_ref, hp_ref, xs_ref, zero_ref, sem, fill_sem):
    tb = hp_ref.shape[0]
    t_rows = zero_ref.shape[0]

    @pl.when(pl.program_id(0) == 0)
    def _():
        zero_ref[...] = jnp.zeros_like(zero_ref)

        def tile_copy(t):
            return pltpu.make_async_copy(zero_ref, xs_ref.at[pl.ds(t * t_rows, t_rows), :], fill_sem)

        def fill(t, carry):
            @pl.when(fill_ref[t] == 1)
            def _():
                tile_copy(t).start()
            return carry

        def fill_wait(t, carry):
            @pl.when(fill_ref[t] == 1)
            def _():
                tile_copy(t).wait()
            return carry

        lax.fori_loop(0, fill_ref.shape[0], fill, 0)
        lax.fori_loop(0, fill_ref.shape[0], fill_wait, 0)

    def row_copy(j, s):
        return pltpu.make_async_copy(hp_ref.at[pl.ds(j, 1), :], xs_ref.at[pl.ds(s, 1), :], sem)

    def issue(j, carry):
        for k in range(TOP_K):
            row_copy(j, slot_ref[k, j]).start()
        return carry

    lax.fori_loop(0, tb, issue, 0)

    def drain(j, carry):
        for k in range(TOP_K):
            row_copy(0, 0).wait()
        return carry

    lax.fori_loop(0, tb, drain, 0)


def _dispatch(fill_tiles, slots_t, hp, n_slots):
    n, w = hp.shape
    tb = ROUTER_BLOCK
    grid_spec = pltpu.PrefetchScalarGridSpec(
        num_scalar_prefetch=1,
        grid=(n // tb,),
        in_specs=[pl.BlockSpec((TOP_K, tb), lambda i, f: (0, i), memory_space=pltpu.SMEM),
                  pl.BlockSpec((tb, w), lambda i, f: (i, 0))],
        out_specs=pl.BlockSpec(memory_space=pl.ANY),
        scratch_shapes=[pltpu.VMEM((MOE_TILE, w), jnp.uint32),
                        pltpu.SemaphoreType.DMA(()), pltpu.SemaphoreType.DMA(())])
    return pl.pallas_call(
        _dispatch_kernel, grid_spec=grid_spec,
        out_shape=jax.ShapeDtypeStruct((n_slots, w), jnp.uint32),
        compiler_params=_params(("arbitrary",), 2 * tb * w * 4 + MOE_TILE * w * 4),
        name="dispatch",
    )(fill_tiles, slots_t, hp)


def _expert_up_kernel(te_ref, tq_ref, first_ref, nvalid_ref, xs_ref, w_ref, b_ref, o_ref, wb_ref):
    t = pl.program_id(1)
    del te_ref, tq_ref

    @pl.when(first_ref[t] == 1)
    def _():
        wb_ref[...] = w_ref[...].astype(BF16)

    @pl.when(nvalid_ref[t] == 0)
    def _():
        o_ref[...] = jnp.zeros_like(o_ref)

    @pl.when(nvalid_ref[t] > 0)
    def _():
        xw = xs_ref[...]
        lo = pltpu.unpack_elementwise(xw, index=0, packed_dtype=BF16, unpacked_dtype=F32)
        hi = pltpu.unpack_elementwise(xw, index=1, packed_dtype=BF16, unpacked_dtype=F32)
        xb = jnp.concatenate([lo.astype(BF16), hi.astype(BF16)], axis=1)
        a = jnp.dot(xb, wb_ref[...], preferred_element_type=F32) + b_ref[...]
        tn = a.shape[1]
        even = (lax.broadcasted_iota(I32, (1, LANES), 1) % 2) == 0
        for c in range(tn // (2 * LANES)):
            ga = a[:, (2 * c) * LANES:(2 * c + 1) * LANES]
            gb = a[:, (2 * c + 1) * LANES:(2 * c + 2) * LANES]
            glu = jnp.where(even, ga, pltpu.roll(gb, 1, 1))
            lin = jnp.where(even, pltpu.roll(ga, LANES - 1, 1), gb)
            glu = jnp.minimum(glu, SWIGLU_LIMIT)
            lin = jnp.clip(lin, -SWIGLU_LIMIT, SWIGLU_LIMIT)
            act = glu * jax.nn.sigmoid(SWIGLU_ALPHA * glu) * (lin + 1.0)
            o_ref[:, c * LANES:(c + 1) * LANES] = act.astype(BF16)


def _expert_up(tile_tabs, xs, w1, b1, max_tiles):
    p, half = xs.shape
    e, d, n = w1.shape
    tn = 1024
    t_rows = MOE_TILE
    grid_spec = pltpu.PrefetchScalarGridSpec(
        num_scalar_prefetch=4,
        grid=(n // tn, max_tiles),
        in_specs=[pl.BlockSpec((t_rows, half), lambda j, t, te, tq, fi, nv: (tq[t], 0)),
                  pl.BlockSpec((None, d, tn), lambda j, t, te, tq, fi, nv: (te[t], 0, j)),
                  pl.BlockSpec((None, 1, tn), lambda j, t, te, tq, fi, nv: (te[t], 0, j))],
        out_specs=pl.BlockSpec((t_rows, tn // 2), lambda j, t, te, tq, fi, nv: (t, j)),
        scratch_shapes=[pltpu.VMEM((d, tn), BF16)])
    vmem = 2 * d * tn * 4 + d * tn * 2 + 2 * t_rows * half * 4 + 6 * t_rows * tn * 4
    return pl.pallas_call(
        _expert_up_kernel, grid_spec=grid_spec,
        out_shape=jax.ShapeDtypeStruct((p, n // 2), BF16),
        compiler_params=_params(("arbitrary", "arbitrary"), vmem),
        name="expert_up",
    )(*tile_tabs, xs, w1, b1.reshape(e, 1, n))


def _expert_down_kernel(te_ref, tq_ref, first_ref, nvalid_ref, a_ref, w_ref, b_ref, o_ref, wb_ref):
    t = pl.program_id(1)
    del te_ref, tq_ref

    @pl.when(first_ref[t] == 1)
    def _():
        half = LANES // 2
        for qd in range(w_ref.shape[0] // LANES):
            wa = w_ref[qd * LANES:qd * LANES + half, :]
            wc = w_ref[qd * LANES + half:(qd + 1) * LANES, :]
            packed = pltpu.pack_elementwise([wa, wc], packed_dtype=BF16)
            wb_ref[qd * LANES:(qd + 1) * LANES, :] = pltpu.bitcast(packed, BF16)

    @pl.when(nvalid_ref[t] == 0)
    def _():
        o_ref[...] = jnp.zeros_like(o_ref)

    @pl.when(nvalid_ref[t] > 0)
    def _():
        o_ref[...] = jnp.dot(a_ref[...], wb_ref[...], preferred_element_type=F32) + b_ref[...]


def _expert_down(tile_tabs, act, w2, b2, max_tiles):
    p, k = act.shape
    e, _, d = w2.shape
    tn = 1024
    t_rows = MOE_TILE
    grid_spec = pltpu.PrefetchScalarGridSpec(
        num_scalar_prefetch=4,
        grid=(d // tn, max_tiles),
        in_specs=[pl.BlockSpec((t_rows, k), lambda j, t, te, tq, fi, nv: (tq[t], 0)),
                  pl.BlockSpec((None, k, tn), lambda j, t, te, tq, fi, nv: (te[t], 0, j)),
                  pl.BlockSpec((None, 1, tn), lambda j, t, te, tq, fi, nv: (te[t], 0, j))],
        out_specs=pl.BlockSpec((t_rows, tn), lambda j, t, te, tq, fi, nv: (t, j)),
        scratch_shapes=[pltpu.VMEM((k, tn), BF16)])
    vmem = 2 * k * tn * 4 + k * tn * 2 + 2 * t_rows * k * 2 + 4 * t_rows * tn * 4
    return pl.pallas_call(
        _expert_down_kernel, grid_spec=grid_spec,
        out_shape=jax.ShapeDtypeStruct((p, d), F32),
        compiler_params=_params(("arbitrary", "arbitrary"), vmem),
        name="expert_down",
    )(*tile_tabs, act, w2, b2.reshape(e, 1, d))


COMBINE_BLOCK = 128


def _combine_kernel(slot_ref, slot_next_ref, y_ref, w_ref, x_ref, g_ref, o_ref, buf, sem, *, nblk):
    i = pl.program_id(0)
    tb = x_ref.shape[0]

    def row_copy(s, b, k, j):
        return pltpu.make_async_copy(y_ref.at[pl.ds(s, 1), :], buf.at[b, k, pl.ds(j, 1), :], sem.at[b])

    def issue(slots, b):
        def body(j, carry):
            for k in range(TOP_K):
                row_copy(slots[k, j], b, k, j).start()
            return carry
        lax.fori_loop(0, tb, body, 0)

    @pl.when(i == 0)
    def _():
        issue(slot_ref, 0)

    @pl.when(i + 1 < nblk)
    def _():
        issue(slot_next_ref, (i + 1) % 2)

    cur = i % 2

    def drain(j, carry):
        for k in range(TOP_K):
            row_copy(0, cur, 0, 0).wait()
        return carry

    lax.fori_loop(0, tb, drain, 0)

    w = w_ref[...]
    acc = w[:, 0:1] * buf[cur, 0]
    for k in range(1, TOP_K):
        acc = acc + w[:, k:k + 1] * buf[cur, k]
    o_ref[...] = x_ref[...] + g_ref[...] * acc


def _combine(slots_t, y, gates, x2d, gate_mod, *, rows_per_mod):
    n, d = x2d.shape
    tb = COMBINE_BLOCK
    nblk = n // tb
    per = rows_per_mod // tb
    vmem = 2 * TOP_K * tb * d * 4 + 6 * tb * d * 4
    return pl.pallas_call(
        functools.partial(_combine_kernel, nblk=nblk),
        grid=(nblk,),
        in_specs=[pl.BlockSpec((TOP_K, tb), lambda i: (0, i), memory_space=pltpu.SMEM),
                  pl.BlockSpec((TOP_K, tb), lambda i: (0, jnp.minimum(i + 1, nblk - 1)),
                               memory_space=pltpu.SMEM),
                  pl.BlockSpec(memory_space=pl.ANY),
                  pl.BlockSpec((tb, LANES), lambda i: (i, 0)),
                  pl.BlockSpec((tb, d), lambda i: (i, 0)),
                  pl.BlockSpec((None, 1, d), lambda i: (i // per, 0, 0))],
        out_specs=pl.BlockSpec((tb, d), lambda i: (i, 0)),
        out_shape=jax.ShapeDtypeStruct((n, d), F32),
        scratch_shapes=[pltpu.VMEM((2, TOP_K, tb, d), F32), pltpu.SemaphoreType.DMA((2,))],
        compiler_params=_params(("arbitrary",), vmem),
        name="combine",
    )(slots_t, slots_t, y, gates, x2d, gate_mod)


def _tile_tables(counts, max_tiles):
    t_rows = MOE_TILE
    tiles_e = (counts + t_rows - 1) // t_rows
    tile_end = jnp.cumsum(tiles_e)
    tile_start = tile_end - tiles_e
    num_tiles = tile_end[-1]
    t = jnp.arange(max_tiles, dtype=I32)
    t_eff = jnp.minimum(t, num_tiles - 1)
    te = jnp.minimum(jnp.searchsorted(tile_end, t_eff, side="right"), N_EXPERTS - 1).astype(I32)
    live = t < num_tiles
    nvalid = jnp.where(live, jnp.clip(counts[te] - (t_eff - tile_start[te]) * t_rows, 0, t_rows), 0)
    first = jnp.where(live & (t_eff == tile_start[te]), 1, 0)
    fill = jnp.where(live & (t_eff != tile_end[te] - 1), 0, 1)
    return ((tile_start * t_rows).astype(I32), fill.astype(I32),
            (te, t_eff.astype(I32), first.astype(I32), nvalid.astype(I32)))


def _moe(parts, shift2, scale2, gate2, norm_g, router_w, router_b, w1, b1, w2, b2):
    d = router_w.shape[0]
    rw_pad = jnp.zeros((d, LANES), F32).at[:, :N_EXPERTS].set(router_w)
    rb_pad = jnp.full((1, LANES), NEG_INF, F32).at[0, :N_EXPERTS].set(router_b)
    counts = jnp.zeros((1, LANES), F32)
    routed = []
    for x2d, mod_rows, per in parts:
        hp, idx, wgt, rank, counts = _router(
            x2d, shift2[mod_rows][:, None, :], scale2[mod_rows][:, None, :], norm_g[None, :],
            rw_pad, rb_pad, counts, rows_per_mod=per)
        routed.append((hp, idx[:, :TOP_K], wgt, rank[:, :TOP_K]))
    n_total = sum(x2d.shape[0] for x2d, _, _ in parts)
    max_tiles = (n_total * TOP_K) // MOE_TILE + N_EXPERTS
    n_slots = max_tiles * MOE_TILE
    group_start, fill_tiles, tabs = _tile_tables(counts[0, :N_EXPERTS].astype(I32), max_tiles)

    slots = [(group_start[idx] + rank).T for _, idx, _, rank in routed]
    hp_all = jnp.concatenate([r[0] for r in routed], axis=0) if len(routed) > 1 else routed[0][0]
    slots_all = jnp.concatenate(slots, axis=1) if len(slots) > 1 else slots[0]
    xs = _dispatch(fill_tiles, slots_all, hp_all, n_slots)
    act = _expert_up(tabs, xs, w1, b1, max_tiles)
    y = _expert_down(tabs, act, w2, b2, max_tiles)
    outs = []
    for (x2d, mod_rows, per), sl, (_, _, wgt, _) in zip(parts, slots, routed):
        outs.append(_combine(sl, y, wgt, x2d, gate2[mod_rows][:, None, :], rows_per_mod=per))
    return outs


def _rope_tables(seq):
    t = jnp.arange(seq, dtype=I32)
    m = HEAD_DIM // 4
    inv_freq = ROPE_THETA ** (-jnp.arange(m, dtype=F32) / m)
    ang_r = (t // GRID_W).astype(F32)[:, None] * inv_freq[None, :]
    ang_c = (t % GRID_W).astype(F32)[:, None] * inv_freq[None, :]
    cos = jnp.concatenate([jnp.cos(ang_r)] * 2 + [jnp.cos(ang_c)] * 2, axis=-1)
    sin = jnp.concatenate([-jnp.sin(ang_r), jnp.sin(ang_r), -jnp.sin(ang_c), jnp.sin(ang_c)], axis=-1)
    return cos, sin


def kernel(x, c, ctx, c_ctx, ada_w, ada_b, norm_mix_g, norm_ffn_g, a_w_qkv, a_w_o, a_q_gain, a_k_gain,
           b_w_qkv, b_w_o, b_q_gain, b_k_gain, b_rel_bias, router_w, router_b, exp_w1, exp_b1, exp_w2, exp_b2):
    bsz, seq, d = x.shape
    l_ctx = ctx.shape[1]
    depth = ada_w.shape[0]
    n_heads = d // HEAD_DIM
    q_scale = HEAD_DIM ** -0.5 * LOG2E

    c_rows = jnp.zeros((SUBLANES, d), F32).at[:bsz].set(c).at[bsz].set(c_ctx)
    mod = _ada_modulation(c_rows, ada_w, ada_b)
    cos, sin = _rope_tables(seq)
    ones_tab = jnp.ones((l_ctx, HEAD_DIM), F32)
    x_rows = jnp.arange(bsz)
    c_rows_idx = jnp.full((bsz,), bsz)

    for i in range(depth):
        last = i == depth - 1
        sh1, sc1, g1, sh2, sc2, g2 = [mod[i, :, k * d:(k + 1) * d] for k in range(N_MOD)]
        j = i // 2
        if i % 2 == 0:
            w_qkv, w_o, qg, kg = a_w_qkv[j], a_w_o[j], a_q_gain[j], a_k_gain[j]
            n_q, n_k, n_v = n_heads, A_KV_HEADS, A_KV_HEADS
        else:
            w_qkv, w_o, qg, kg = b_w_qkv[j], b_w_o[j], b_q_gain[j], b_k_gain[j]
            n_q, n_k, n_v = n_heads, n_heads, n_heads
        heads_per_tile = 512 // HEAD_DIM
        gain_cols = jnp.concatenate([jnp.tile(qg * q_scale, n_q), jnp.tile(kg, n_k),
                                     jnp.ones((n_v * HEAD_DIM,), F32)])[None, :]
        flags = jnp.concatenate([jnp.ones(((n_q + n_k) // heads_per_tile,), I32),
                                 jnp.zeros((n_v // heads_per_tile,), I32)])
        rope = i % 2 == 0
        qkv_x = _qkv_project(x, sh1[x_rows][:, None, :], sc1[x_rows][:, None, :], norm_mix_g[i][None, :],
                             w_qkv, gain_cols, flags, cos, sin, rope=rope, tm=1024)
        qkv_c = _qkv_project(ctx, sh1[c_rows_idx][:, None, :], sc1[c_rows_idx][:, None, :],
                             norm_mix_g[i][None, :], w_qkv, gain_cols, flags, ones_tab, ones_tab,
                             rope=False, tm=l_ctx)
        if i % 2 == 0:
            ox = _attention(qkv_x, qkv_c, qkv_x, n_q_heads=n_q, n_kv_heads=n_k,
                            k_col0=n_q, v_col0=n_q + n_k, tq=256)
        else:
            ox = _neighborhood_attention(qkv_x, qkv_c, b_rel_bias[j], n_heads=n_heads)
        x = _out_project(ox, w_o, x, g1[x_rows][:, None, :], tm=1024)
        parts = [(x.reshape(bsz * seq, d), x_rows, seq)]
        if not last:
            oc = _attention(qkv_c, qkv_c, None, n_q_heads=n_q, n_kv_heads=n_k,
                            k_col0=n_q, v_col0=n_q + n_k, tq=l_ctx)
            ctx = _out_project(oc, w_o, ctx, g1[c_rows_idx][:, None, :], tm=l_ctx)
            parts.append((ctx.reshape(bsz * l_ctx, d), jnp.full((1,), bsz), bsz * l_ctx))
        outs = _moe(parts, sh2, sc2, g2, norm_ffn_g[i], router_w[i], router_b[i],
                    exp_w1[i], exp_b1[i], exp_w2[i], exp_b2[i])
        x = outs[0].reshape(bsz, seq, d)
        if not last:
            ctx = outs[1].reshape(bsz, l_ctx, d)
    return x
```

```python
import functools
import math

import jax
import jax.numpy as jnp
from jax import lax
from jax.experimental import pallas as pl
from jax.experimental.pallas import tpu as pltpu

F32 = jnp.float32
BF16 = jnp.bfloat16
I32 = jnp.int32

LANES = 128
SUBLANES = 8
VMEM_BYTES_V7X = 64 * 1024 * 1024

HEAD_DIM = 128
GRID_W = 64
A_KV_HEADS = 4
NA_WIN_ROWS = 8
NA_WIN_COLS = 16
N_EXPERTS = 32
TOP_K = 4
ROPE_THETA = 10000.0
SWIGLU_ALPHA = 1.702
SWIGLU_LIMIT = 7.0
NORM_EPS = 1e-6
N_MOD = 6
LOG2E = math.log2(math.e)
NEG_INF = float("-inf")

MOE_TILE = 256


def _vmem_limit(nbytes):
    return int(min(nbytes + 8 * 1024 * 1024, VMEM_BYTES_V7X - 8 * 1024 * 1024))


def _params(semantics, vmem_bytes):
    return pltpu.CompilerParams(dimension_semantics=semantics,
                                vmem_limit_bytes=_vmem_limit(vmem_bytes))


def _ada_kernel(c_ref, w_ref, b_ref, o_ref):
    c = c_ref[...]
    a = (c * jax.nn.sigmoid(c)).astype(BF16)
    o_ref[0] = jnp.dot(a, w_ref[0].astype(BF16), preferred_element_type=F32) + b_ref[0]


def _ada_modulation(c_rows, ada_w, ada_b):
    depth, d, n = ada_w.shape
    rows = c_rows.shape[0]
    tn = 1024
    return pl.pallas_call(
        _ada_kernel,
        grid=(depth, n // tn),
        in_specs=[pl.BlockSpec((rows, d), lambda l, j: (0, 0)),
                  pl.BlockSpec((1, d, tn), lambda l, j: (l, 0, j)),
                  pl.BlockSpec((1, 1, tn), lambda l, j: (l, 0, j))],
        out_specs=pl.BlockSpec((1, rows, tn), lambda l, j: (l, 0, j)),
        out_shape=jax.ShapeDtypeStruct((depth, rows, n), F32),
        compiler_params=_params(("arbitrary", "arbitrary"), 2 * d * tn * 4 + d * tn * 2),
        name="ada_modulation",
    )(c_rows, ada_w, ada_b.reshape(depth, 1, n))


def _norm_modulate(x, g, shift, scale):
    ms = jnp.mean(x * x, axis=-1, keepdims=True)
    y = x * lax.rsqrt(ms + NORM_EPS) * g
    return y * (1.0 + scale) + shift


def _qkv_kernel(flag_ref, x_ref, sh_ref, sc_ref, g_ref, w_ref, gain_ref, cos_ref, sin_ref,
                o_ref, h_ref, *, rope):
    n = pl.program_id(2)

    @pl.when(n == 0)
    def _():
        h_ref[...] = _norm_modulate(x_ref[...], g_ref[...], sh_ref[...], sc_ref[...]).astype(BF16)

    a = jnp.dot(h_ref[...], w_ref[...].astype(BF16), preferred_element_type=F32)
    tn = a.shape[1]

    @pl.when(flag_ref[n] == 0)
    def _():
        o_ref[...] = a.astype(BF16)

    @pl.when(flag_ref[n] == 1)
    def _():
        lane = lax.broadcasted_iota(I32, (1, HEAD_DIM), 1)
        first_half = (lane % (HEAD_DIM // 2)) < (HEAD_DIM // 4)
        for hh in range(tn // HEAD_DIM):
            sl = slice(hh * HEAD_DIM, (hh + 1) * HEAD_DIM)
            ah = a[:, sl]
            ms = jnp.mean(ah * ah, axis=-1, keepdims=True)
            yh = ah * lax.rsqrt(ms + NORM_EPS) * gain_ref[:, sl]
            if rope:
                partner = jnp.where(first_half,
                                    pltpu.roll(yh, HEAD_DIM - HEAD_DIM // 4, 1),
                                    pltpu.roll(yh, HEAD_DIM // 4, 1))
                yh = yh * cos_ref[...] + partner * sin_ref[...]
            o_ref[:, sl] = yh.astype(BF16)


def _qkv_project(x, shift, scale, g, w, gain_cols, norm_flags, cos, sin, *, rope, tm):
    b, t, d = x.shape
    n = w.shape[1]
    tn = 512
    kernel = functools.partial(_qkv_kernel, rope=rope)
    grid_spec = pltpu.PrefetchScalarGridSpec(
        num_scalar_prefetch=1,
        grid=(b, t // tm, n // tn),
        in_specs=[pl.BlockSpec((None, tm, d), lambda bi, m, j, f: (bi, m, 0)),
                  pl.BlockSpec((None, 1, d), lambda bi, m, j, f: (bi, 0, 0)),
                  pl.BlockSpec((None, 1, d), lambda bi, m, j, f: (bi, 0, 0)),
                  pl.BlockSpec((1, d), lambda bi, m, j, f: (0, 0)),
                  pl.BlockSpec((d, tn), lambda bi, m, j, f: (0, j)),
                  pl.BlockSpec((1, tn), lambda bi, m, j, f: (0, j)),
                  pl.BlockSpec((tm, HEAD_DIM), lambda bi, m, j, f: (m, 0)),
                  pl.BlockSpec((tm, HEAD_DIM), lambda bi, m, j, f: (m, 0))],
        out_specs=pl.BlockSpec((None, tm, tn), lambda bi, m, j, f: (bi, m, j)),
        scratch_shapes=[pltpu.VMEM((tm, d), BF16)])
    vmem = 2 * tm * d * 4 + tm * d * 2 + 2 * d * tn * 4 + d * tn * 2 + 4 * tm * tn * 4
    return pl.pallas_call(
        kernel, grid_spec=grid_spec,
        out_shape=jax.ShapeDtypeStruct((b, t, n), BF16),
        compiler_params=_params(("arbitrary", "arbitrary", "arbitrary"), vmem),
        name="qkv_project",
    )(norm_flags, x, shift, scale, g, w, gain_cols, cos, sin)


def _nt_dot(a, b):
    return lax.dot_general(a, b, (((1,), (1,)), ((), ())), preferred_element_type=F32)


def _attn_kernel(q_ref, kc_ref, vc_ref, *rest, group, has_x):
    if has_x:
        kx_ref, vx_ref, o_ref = rest
    else:
        (o_ref,) = rest
    kc = kc_ref[...]
    vc = vc_ref[...]
    for g in range(group):
        sl = slice(g * HEAD_DIM, (g + 1) * HEAD_DIM)
        q = q_ref[:, sl]
        sc = _nt_dot(q, kc)
        m = jnp.max(sc, axis=-1, keepdims=True)
        if has_x:
            sx = _nt_dot(q, kx_ref[...])
            m = jnp.maximum(m, jnp.max(sx, axis=-1, keepdims=True))
        pc = jnp.exp2(sc - m)
        l = jnp.sum(pc, axis=-1, keepdims=True)
        acc = jnp.dot(pc.astype(BF16), vc, preferred_element_type=F32)
        if has_x:
            px = jnp.exp2(sx - m)
            l = l + jnp.sum(px, axis=-1, keepdims=True)
            acc = acc + jnp.dot(px.astype(BF16), vx_ref[...], preferred_element_type=F32)
        o_ref[:, sl] = (acc / l).astype(BF16)


def _attention(q_src, ctx_src, x_src, *, n_q_heads, n_kv_heads, k_col0, v_col0, tq):
    b, t_q, _ = q_src.shape
    l = ctx_src.shape[1]
    group = n_q_heads // n_kv_heads
    has_x = x_src is not None
    gw = group * HEAD_DIM
    in_specs = [pl.BlockSpec((None, tq, gw), lambda bi, kv, i: (bi, i, kv)),
                pl.BlockSpec((None, l, HEAD_DIM), lambda bi, kv, i: (bi, 0, k_col0 + kv)),
                pl.BlockSpec((None, l, HEAD_DIM), lambda bi, kv, i: (bi, 0, v_col0 + kv))]
    args = [q_src, ctx_src, ctx_src]
    s = 0
    if has_x:
        s = x_src.shape[1]
        in_specs += [pl.BlockSpec((None, s, HEAD_DIM), lambda bi, kv, i: (bi, 0, k_col0 + kv)),
                     pl.BlockSpec((None, s, HEAD_DIM), lambda bi, kv, i: (bi, 0, v_col0 + kv))]
        args += [x_src, x_src]
    vmem = 4 * (l + s) * HEAD_DIM * 2 + 4 * tq * gw * 2 + 4 * tq * (l + s) * 4
    return pl.pallas_call(
        functools.partial(_attn_kernel, group=group, has_x=has_x),
        grid=(b, n_kv_heads, t_q // tq),
        in_specs=in_specs,
        out_specs=pl.BlockSpec((None, tq, gw), lambda bi, kv, i: (bi, i, kv)),
        out_shape=jax.ShapeDtypeStruct((b, t_q, n_q_heads * HEAD_DIM), BF16),
        compiler_params=_params(("arbitrary", "arbitrary", "arbitrary"), vmem),
        name="attention_x" if has_x else "attention_ctx",
    )(*args)


NA_Q_ROWS = 8
NA_BAND_ROWS = 2 * NA_WIN_ROWS
NA_DR = 2 * NA_WIN_ROWS - 1
NA_DC = 2 * NA_WIN_COLS - 1
NA_PAIR_TILES = 30


def _na_kernel(tab_ref, q_ref, k_ref, v_ref, kc_ref, vc_ref, o_ref, tile_ref, pair_ref, s_ref, *, rows):
    h = pl.program_id(1)
    a = pl.program_id(2)
    nq = NA_Q_ROWS * GRID_W
    nb = NA_BAND_ROWS * GRID_W
    l_ctx = kc_ref.shape[0]

    @pl.when(a == 0)
    def _():
        qc = lax.broadcasted_iota(I32, (GRID_W, LANES), 0)
        kc = lax.broadcasted_iota(I32, (GRID_W, LANES), 1) % GRID_W
        dc = jnp.clip(kc - qc, -(NA_WIN_COLS - 1), NA_WIN_COLS - 1) + NA_WIN_COLS - 1
        c0 = jnp.clip(qc - NA_WIN_COLS // 2, 0, GRID_W - NA_WIN_COLS)
        col_ok = (kc >= c0) & (kc < c0 + NA_WIN_COLS)
        for dr in range(NA_DR):
            acc = jnp.zeros((GRID_W, LANES), F32)
            for d in range(NA_DC):
                acc = jnp.where(dc == d, tab_ref[h * (NA_DR * NA_DC) + dr * NA_DC + d] * LOG2E, acc)
            tile_ref[dr] = jnp.where(col_ok, acc, NEG_INF)
        left = lax.broadcasted_iota(I32, (GRID_W, LANES), 1) < GRID_W
        zero = jnp.zeros((GRID_W, LANES), F32)
        for p in range(NA_PAIR_TILES):
            lo = tile_ref[p - NA_WIN_ROWS] if 0 <= p - NA_WIN_ROWS < NA_DR else zero
            hi = tile_ref[p - NA_WIN_ROWS + 1] if 0 <= p - NA_WIN_ROWS + 1 < NA_DR else zero
            pair_ref[p] = jnp.where(left, lo, hi)

    wr = NA_WIN_ROWS
    kb0 = jnp.clip(a * NA_Q_ROWS - wr // 2, 0, rows - NA_BAND_ROWS)
    k_band = k_ref[pl.ds(pl.multiple_of(kb0 * GRID_W, 4 * GRID_W), nb), :]
    v_band = v_ref[pl.ds(pl.multiple_of(kb0 * GRID_W, 4 * GRID_W), nb), :]
    q = q_ref[...]
    s_loc = _nt_dot(q, k_band)
    s_ref[:, nb:] = _nt_dot(q, kc_ref[...])

    band_row = lax.broadcasted_iota(I32, (1, nb), 1) // GRID_W
    for rq in range(NA_Q_ROWS):
        r = a * NA_Q_ROWS + rq
        r0 = jnp.clip(r - wr // 2, 0, rows - wr)
        lo = r0 - kb0
        row_mask = jnp.where((band_row >= lo) & (band_row < lo + wr), 0.0, NEG_INF)
        j0 = kb0 - r + 2 * wr - 1
        bias = jnp.concatenate([pair_ref[j0 + 2 * i] for i in range(NA_BAND_ROWS // 2)], axis=1)
        rs = slice(rq * GRID_W, (rq + 1) * GRID_W)
        s_ref[rs, :nb] = s_loc[rs, :] + bias + row_mask

    s = s_ref[...]
    m = jnp.max(s, axis=-1, keepdims=True)
    p = jnp.exp2(s - m)
    l = jnp.sum(p, axis=-1, keepdims=True)
    pb = p.astype(BF16)
    acc = (jnp.dot(pb[:, :nb], v_band, preferred_element_type=F32)
           + jnp.dot(pb[:, nb:], vc_ref[...], preferred_element_type=F32))
    o_ref[...] = (acc / l).astype(BF16)
    del nq, l_ctx


def _neighborhood_attention(qkv_x, qkv_c, rel_bias, *, n_heads):
    b, s, _ = qkv_x.shape
    l = qkv_c.shape[1]
    rows = s // GRID_W
    nq = NA_Q_ROWS * GRID_W
    nb = NA_BAND_ROWS * GRID_W
    grid_spec = pltpu.PrefetchScalarGridSpec(
        num_scalar_prefetch=0,
        grid=(b, n_heads, rows // NA_Q_ROWS),
        in_specs=[pl.BlockSpec(memory_space=pltpu.SMEM),
                  pl.BlockSpec((None, nq, HEAD_DIM), lambda bi, h, a: (bi, a, h)),
                  pl.BlockSpec((None, s, HEAD_DIM), lambda bi, h, a: (bi, 0, n_heads + h)),
                  pl.BlockSpec((None, s, HEAD_DIM), lambda bi, h, a: (bi, 0, 2 * n_heads + h)),
                  pl.BlockSpec((None, l, HEAD_DIM), lambda bi, h, a: (bi, 0, n_heads + h)),
                  pl.BlockSpec((None, l, HEAD_DIM), lambda bi, h, a: (bi, 0, 2 * n_heads + h))],
        out_specs=pl.BlockSpec((None, nq, HEAD_DIM), lambda bi, h, a: (bi, a, h)),
        scratch_shapes=[pltpu.VMEM((NA_DR, GRID_W, LANES), F32),
                        pltpu.VMEM((NA_PAIR_TILES, GRID_W, LANES), F32),
                        pltpu.VMEM((nq, nb + l), F32)])
    vmem = 4 * (s + l) * HEAD_DIM * 2 + 4 * nq * (nb + l) * 4 + 2 * 1024 * 1024
    return pl.pallas_call(
        functools.partial(_na_kernel, rows=rows), grid_spec=grid_spec,
        out_shape=jax.ShapeDtypeStruct((b, s, n_heads * HEAD_DIM), BF16),
        compiler_params=_params(("arbitrary", "arbitrary", "arbitrary"), vmem),
        name="neighborhood_attention",
    )(rel_bias.reshape(-1), qkv_x, qkv_x, qkv_x, qkv_c, qkv_c)


def _oproj_kernel(o_ref, w_ref, x_ref, g_ref, out_ref):
    y = jnp.dot(o_ref[...], w_ref[...].astype(BF16), preferred_element_type=F32)
    out_ref[...] = x_ref[...] + g_ref[...] * y


def _out_project(o, w, x, gate, *, tm):
    b, t, k = o.shape
    d = w.shape[1]
    tn = 512
    vmem = 2 * tm * k * 2 + 2 * k * tn * 4 + k * tn * 2 + 6 * tm * tn * 4
    return pl.pallas_call(
        _oproj_kernel,
        grid=(b, t // tm, d // tn),
        in_specs=[pl.BlockSpec((None, tm, k), lambda bi, m, j: (bi, m, 0)),
                  pl.BlockSpec((k, tn), lambda bi, m, j: (0, j)),
                  pl.BlockSpec((None, tm, tn), lambda bi, m, j: (bi, m, j)),
                  pl.BlockSpec((None, 1, tn), lambda bi, m, j: (bi, 0, j))],
        out_specs=pl.BlockSpec((None, tm, tn), lambda bi, m, j: (bi, m, j)),
        out_shape=jax.ShapeDtypeStruct((b, t, d), F32),
        compiler_params=_params(("arbitrary", "arbitrary", "arbitrary"), vmem),
        name="out_project",
    )(o, w, x, gate)


ROUTER_BLOCK = 256


def _router_kernel(x_ref, sh_ref, sc_ref, g_ref, rw_ref, rb_ref, cnt_in_ref,
                   hp_ref, idx_ref, wgt_ref, rank_ref, cnt_ref, run_ref):
    i = pl.program_id(0)

    @pl.when(i == 0)
    def _():
        run_ref[...] = cnt_in_ref[...]

    h = _norm_modulate(x_ref[...], g_ref[...], sh_ref[...], sc_ref[...])
    tb, d = h.shape
    hp_ref[...] = pltpu.pack_elementwise([h[:, :d // 2], h[:, d // 2:]], packed_dtype=BF16)

    logits = jnp.dot(h, rw_ref[...], preferred_element_type=F32,
                     precision=lax.Precision.HIGHEST) + rb_ref[...]
    lane = lax.broadcasted_iota(I32, (tb, LANES), 1)
    lane_f = lane.astype(F32)
    work = logits
    idx_acc = jnp.zeros((tb, LANES), F32)
    val_acc = jnp.full((tb, LANES), NEG_INF, F32)
    hots = []
    for k in range(TOP_K):
        m = jnp.max(work, axis=-1, keepdims=True)
        idx = jnp.min(jnp.where(work == m, lane_f, float(LANES)), axis=-1, keepdims=True)
        hot = lane_f == idx
        hots.append(hot)
        idx_acc = jnp.where(lane == k, idx, idx_acc)
        val_acc = jnp.where(lane == k, m, val_acc)
        work = jnp.where(hot, NEG_INF, work)

    e = jnp.exp(val_acc - jnp.max(val_acc, axis=-1, keepdims=True))
    wgt_ref[...] = e / jnp.sum(e, axis=-1, keepdims=True)
    idx_ref[...] = idx_acc.astype(I32)

    chosen = jnp.zeros((tb, LANES), F32)
    for hot in hots:
        chosen = chosen + jnp.where(hot, 1.0, 0.0)
    row = lax.broadcasted_iota(I32, (tb, tb), 0)
    col = lax.broadcasted_iota(I32, (tb, tb), 1)
    earlier = jnp.where(col < row, 1.0, 0.0).astype(BF16)
    before = jnp.dot(earlier, chosen.astype(BF16), preferred_element_type=F32) + run_ref[...]
    rank_acc = jnp.zeros((tb, LANES), F32)
    for k, hot in enumerate(hots):
        rk = jnp.sum(jnp.where(hot, before, 0.0), axis=-1, keepdims=True)
        rank_acc = jnp.where(lane == k, rk, rank_acc)
    rank_ref[...] = rank_acc.astype(I32)
    run_ref[...] = run_ref[...] + jnp.sum(chosen, axis=0, keepdims=True)
    cnt_ref[...] = run_ref[...]


def _router(x2d, shift, scale, g, rw_pad, rb_pad, counts_in, *, rows_per_mod):
    n, d = x2d.shape
    tb = ROUTER_BLOCK
    per = rows_per_mod // tb
    vmem = 2 * tb * d * 4 + 6 * tb * d * 4 + 2 * d * LANES * 4
    outs = pl.pallas_call(
        _router_kernel,
        grid=(n // tb,),
        in_specs=[pl.BlockSpec((tb, d), lambda i: (i, 0)),
                  pl.BlockSpec((None, 1, d), lambda i: (i // per, 0, 0)),
                  pl.BlockSpec((None, 1, d), lambda i: (i // per, 0, 0)),
                  pl.BlockSpec((1, d), lambda i: (0, 0)),
                  pl.BlockSpec((d, LANES), lambda i: (0, 0)),
                  pl.BlockSpec((1, LANES), lambda i: (0, 0)),
                  pl.BlockSpec((1, LANES), lambda i: (0, 0))],
        out_specs=[pl.BlockSpec((tb, d // 2), lambda i: (i, 0)),
                   pl.BlockSpec((tb, LANES), lambda i: (i, 0)),
                   pl.BlockSpec((tb, LANES), lambda i: (i, 0)),
                   pl.BlockSpec((tb, LANES), lambda i: (i, 0)),
                   pl.BlockSpec((1, LANES), lambda i: (0, 0))],
        out_shape=[jax.ShapeDtypeStruct((n, d // 2), jnp.uint32),
                   jax.ShapeDtypeStruct((n, LANES), I32),
                   jax.ShapeDtypeStruct((n, LANES), F32),
                   jax.ShapeDtypeStruct((n, LANES), I32),
                   jax.ShapeDtypeStruct((1, LANES), F32)],
        scratch_shapes=[pltpu.VMEM((1, LANES), F32)],
        compiler_params=_params(("arbitrary",), vmem),
        name="router",
    )(x2d, shift, scale, g, rw_pad, rb_pad, counts_in)
    return outs


def _dispatch_kernel(fill_ref, slot_ref, hp_ref, xs_ref, zero_ref, sem, fill_sem):
    tb = hp_ref.shape[0]
    t_rows = zero_ref.shape[0]

    @pl.when(pl.program_id(0) == 0)
    def _():
        zero_ref[...] = jnp.zeros_like(zero_ref)

        def tile_copy(t):
            return pltpu.make_async_copy(zero_ref, xs_ref.at[pl.ds(t * t_rows, t_rows), :], fill_sem)

        def fill(t, carry):
            @pl.when(fill_ref[t] == 1)
            def _():
                tile_copy(t).start()
            return carry

        def fill_wait(t, carry):
            @pl.when(fill_ref[t] == 1)
            def _():
                tile_copy(t).wait()
            return carry

        lax.fori_loop(0, fill_ref.shape[0], fill, 0)
        lax.fori_loop(0, fill_ref.shape[0], fill_wait, 0)

    def row_copy(j, s):
        return pltpu.make_async_copy(hp_ref.at[pl.ds(j, 1), :], xs_ref.at[pl.ds(s, 1), :], sem)

    def issue(j, carry):
        for k in range(TOP_K):
            row_copy(j, slot_ref[k, j]).start()
        return carry

    lax.fori_loop(0, tb, issue, 0)

    def drain(j, carry):
        for k in range(TOP_K):
            row_copy(0, 0).wait()
        return carry

    lax.fori_loop(0, tb, drain, 0)


def _dispatch(fill_tiles, slots_t, hp, n_slots):
    n, w = hp.shape
    tb = ROUTER_BLOCK
    grid_spec = pltpu.PrefetchScalarGridSpec(
        num_scalar_prefetch=1,
        grid=(n // tb,),
        in_specs=[pl.BlockSpec((TOP_K, tb), lambda i, f: (0, i), memory_space=pltpu.SMEM),
                  pl.BlockSpec((tb, w), lambda i, f: (i, 0))],
        out_specs=pl.BlockSpec(memory_space=pl.ANY),
        scratch_shapes=[pltpu.VMEM((MOE_TILE, w), jnp.uint32),
                        pltpu.SemaphoreType.DMA(()), pltpu.SemaphoreType.DMA(())])
    return pl.pallas_call(
        _dispatch_kernel, grid_spec=grid_spec,
        out_shape=jax.ShapeDtypeStruct((n_slots, w), jnp.uint32),
        compiler_params=_params(("arbitrary",), 2 * tb * w * 4 + MOE_TILE * w * 4),
        name="dispatch",
    )(fill_tiles, slots_t, hp)


EXPERT_TILES = 6
HIDDEN_CHUNK = 256


def _swiglu_pairs(a):
    even = (lax.broadcasted_iota(I32, (1, LANES), 1) % 2) == 0
    outs = []
    for c in range(a.shape[1] // (2 * LANES)):
        ga = a[:, (2 * c) * LANES:(2 * c + 1) * LANES]
        gb = a[:, (2 * c + 1) * LANES:(2 * c + 2) * LANES]
        glu = jnp.where(even, ga, pltpu.roll(gb, 1, 1))
        lin = jnp.where(even, pltpu.roll(ga, LANES - 1, 1), gb)
        glu = jnp.minimum(glu, SWIGLU_LIMIT)
        lin = jnp.clip(lin, -SWIGLU_LIMIT, SWIGLU_LIMIT)
        outs.append((glu * jax.nn.sigmoid(SWIGLU_ALPHA * glu) * (lin + 1.0)).astype(BF16))
    return outs[0] if len(outs) == 1 else jnp.concatenate(outs, axis=1)


def _interleaved_rows_bf16(w_ref, dst_ref):
    half = LANES // 2
    for qd in range(w_ref.shape[0] // LANES):
        wa = w_ref[qd * LANES:qd * LANES + half, :]
        wc = w_ref[qd * LANES + half:(qd + 1) * LANES, :]
        packed = pltpu.pack_elementwise([wa, wc], packed_dtype=BF16)
        dst_ref[qd * LANES:(qd + 1) * LANES, :] = pltpu.bitcast(packed, BF16)


def _expert_ffn_kernel(ie_ref, row0_ref, nt_ref, zt_ref, xs_ref, w1_ref, b1_ref, w2_ref, b2_ref, y_ref,
                       xb_ref, stage_ref, ybuf_ref, w1b_ref, w2b_ref, zero_ref, pend_ref,
                       sem_in, sem_out, *, n_chunks, n_items):
    del ie_ref
    i = pl.program_id(0)
    c = pl.program_id(1)
    t_rows = MOE_TILE
    nt = nt_ref[i]
    row0 = row0_ref[i]

    def out_copy(src_ref, r_src, r_dst):
        dst = r_dst if isinstance(r_dst, int) else pl.multiple_of(r_dst, t_rows)
        return pltpu.make_async_copy(src_ref.at[pl.ds(r_src, t_rows), :],
                                     y_ref.at[pl.ds(dst, t_rows), :], sem_out)

    def wait_pending():
        def body(_, carry):
            out_copy(zero_ref, 0, 0).wait()
            return carry
        lax.fori_loop(0, pend_ref[0], body, 0)
        pend_ref[0] = 0

    @pl.when((i == 0) & (c == 0))
    def _():
        pend_ref[0] = 0
        zero_ref[...] = jnp.zeros_like(zero_ref)

    @pl.when(c == 0)
    def _():
        wait_pending()

        def in_copy(t, slot):
            src = pl.multiple_of(row0 + t * t_rows, t_rows)
            return pltpu.make_async_copy(xs_ref.at[pl.ds(src, t_rows), :],
                                         stage_ref.at[slot], sem_in.at[slot])

        @pl.when(nt > 0)
        def _():
            in_copy(0, 0).start()

        def load(t, carry):
            slot = t % 2
            in_copy(t, slot).wait()

            @pl.when(t + 1 < nt)
            def _():
                in_copy(t + 1, 1 - slot).start()

            xw = stage_ref[slot]
            lo = pltpu.unpack_elementwise(xw, index=0, packed_dtype=BF16, unpacked_dtype=F32)
            hi = pltpu.unpack_elementwise(xw, index=1, packed_dtype=BF16, unpacked_dtype=F32)
            r = pl.multiple_of(t * t_rows, t_rows)
            xb_ref[pl.ds(r, t_rows), :] = jnp.concatenate([lo.astype(BF16), hi.astype(BF16)], axis=1)
            ybuf_ref[pl.ds(r, t_rows), :] = jnp.broadcast_to(b2_ref[...], (t_rows, ybuf_ref.shape[1]))
            return carry

        lax.fori_loop(0, nt, load, 0)

        def zfill(t, carry):
            out_copy(zero_ref, 0, row0 + t * t_rows).start()
            return carry

        lax.fori_loop(0, zt_ref[i], zfill, 0)
        pend_ref[0] = zt_ref[i]

    @pl.when(nt > 0)
    def _():
        w1b_ref[...] = w1_ref[...].astype(BF16)
        _interleaved_rows_bf16(w2_ref, w2b_ref)

        def rows_step(r, n_rows):
            xb = xb_ref[pl.ds(r, n_rows), :]
            a = jnp.dot(xb, w1b_ref[...], preferred_element_type=F32) + b1_ref[...]
            act = _swiglu_pairs(a)
            ybuf_ref[pl.ds(r, n_rows), :] += jnp.dot(act, w2b_ref[...], preferred_element_type=F32)

        def pair(p, carry):
            rows_step(pl.multiple_of(p * (2 * t_rows), 2 * t_rows), 2 * t_rows)
            return carry

        lax.fori_loop(0, nt // 2, pair, 0)

        @pl.when(nt % 2 == 1)
        def _():
            rows_step(pl.multiple_of((nt - 1) * t_rows, t_rows), t_rows)

        @pl.when(c == n_chunks - 1)
        def _():
            def send(t, carry):
                r = pl.multiple_of(t * t_rows, t_rows)
                out_copy(ybuf_ref, r, row0 + r).start()
                return carry
            lax.fori_loop(0, nt, send, 0)
            pend_ref[0] = nt

    @pl.when((i == n_items - 1) & (c == n_chunks - 1))
    def _():
        wait_pending()


def _expert_ffn(item_tabs, xs, w1, b1, w2, b2, layer, max_items):
    p, half = xs.shape
    _, e, d, n = w1.shape
    hc = HIDDEN_CHUNK
    n_chunks = (n // 2) // hc
    rows = EXPERT_TILES * MOE_TILE

    def chunk(c, nt, i):
        return jnp.where(nt[i] > 0, c, n_chunks - 1)

    grid_spec = pltpu.PrefetchScalarGridSpec(
        num_scalar_prefetch=4,
        grid=(max_items, n_chunks),
        in_specs=[pl.BlockSpec(memory_space=pl.ANY),
                  pl.BlockSpec((None, None, d, 2 * hc), lambda i, c, ie, r0, nt, zt: (layer, ie[i], 0, chunk(c, nt, i))),
                  pl.BlockSpec((None, None, 1, 2 * hc), lambda i, c, ie, r0, nt, zt: (layer, ie[i], 0, chunk(c, nt, i))),
                  pl.BlockSpec((None, None, hc, d), lambda i, c, ie, r0, nt, zt: (layer, ie[i], chunk(c, nt, i), 0)),
                  pl.BlockSpec((None, None, 1, d), lambda i, c, ie, r0, nt, zt: (layer, ie[i], 0, 0))],
        out_specs=pl.BlockSpec(memory_space=pl.ANY),
        scratch_shapes=[pltpu.VMEM((rows, d), BF16),
                        pltpu.VMEM((2, MOE_TILE, half), jnp.uint32),
                        pltpu.VMEM((rows, d), F32),
                        pltpu.VMEM((d, 2 * hc), BF16),
                        pltpu.VMEM((hc, d), BF16),
                        pltpu.VMEM((MOE_TILE, d), F32),
                        pltpu.SMEM((1,), I32),
                        pltpu.SemaphoreType.DMA((2,)),
                        pltpu.SemaphoreType.DMA(())])
    vmem = (rows * d * 6 + 2 * MOE_TILE * half * 4 + MOE_TILE * d * 4
            + 3 * d * 2 * hc * 4 + d * 2 * hc * 2 + 3 * hc * d * 4 + hc * d * 2
            + 8 * MOE_TILE * 2 * hc * 4 + 4 * MOE_TILE * d * 4)
    depth = w1.shape[0]
    return pl.pallas_call(
        functools.partial(_expert_ffn_kernel, n_chunks=n_chunks, n_items=max_items), grid_spec=grid_spec,
        out_shape=jax.ShapeDtypeStruct((p, d), F32),
        compiler_params=_params(("arbitrary", "arbitrary"), vmem),
        name="expert_ffn",
    )(*item_tabs, xs, w1, b1.reshape(depth, e, 1, n), w2, b2.reshape(depth, e, 1, d))


COMBINE_BLOCK = 128


def _combine_kernel(slot_ref, slot_next_ref, y_ref, w_ref, x_ref, g_ref, o_ref, buf, sem, *, nblk):
    i = pl.program_id(0)
    tb = x_ref.shape[0]

    def row_copy(s, b, k, j):
        return pltpu.make_async_copy(y_ref.at[pl.ds(s, 1), :], buf.at[b, k, pl.ds(j, 1), :], sem.at[b])

    def issue(slots, b):
        def body(j, carry):
            for k in range(TOP_K):
                row_copy(slots[k, j], b, k, j).start()
            return carry
        lax.fori_loop(0, tb, body, 0)

    @pl.when(i == 0)
    def _():
        issue(slot_ref, 0)

    @pl.when(i + 1 < nblk)
    def _():
        issue(slot_next_ref, (i + 1) % 2)

    cur = i % 2

    def drain(j, carry):
        for k in range(TOP_K):
            row_copy(0, cur, 0, 0).wait()
        return carry

    lax.fori_loop(0, tb, drain, 0)

    w = w_ref[...]
    acc = w[:, 0:1] * buf[cur, 0]
    for k in range(1, TOP_K):
        acc = acc + w[:, k:k + 1] * buf[cur, k]
    o_ref[...] = x_ref[...] + g_ref[...] * acc


def _combine(slots_t, y, gates, x2d, gate_mod, *, rows_per_mod):
    n, d = x2d.shape
    tb = COMBINE_BLOCK
    nblk = n // tb
    per = rows_per_mod // tb
    vmem = 2 * TOP_K * tb * d * 4 + 6 * tb * d * 4
    return pl.pallas_call(
        functools.partial(_combine_kernel, nblk=nblk),
        grid=(nblk,),
        in_specs=[pl.BlockSpec((TOP_K, tb), lambda i: (0, i), memory_space=pltpu.SMEM),
                  pl.BlockSpec((TOP_K, tb), lambda i: (0, jnp.minimum(i + 1, nblk - 1)),
                               memory_space=pltpu.SMEM),
                  pl.BlockSpec(memory_space=pl.ANY),
                  pl.BlockSpec((tb, LANES), lambda i: (i, 0)),
                  pl.BlockSpec((tb, d), lambda i: (i, 0)),
                  pl.BlockSpec((None, 1, d), lambda i: (i // per, 0, 0))],
        out_specs=pl.BlockSpec((tb, d), lambda i: (i, 0)),
        out_shape=jax.ShapeDtypeStruct((n, d), F32),
        scratch_shapes=[pltpu.VMEM((2, TOP_K, tb, d), F32), pltpu.SemaphoreType.DMA((2,))],
        compiler_params=_params(("arbitrary",), vmem),
        name="combine",
    )(slots_t, slots_t, y, gates, x2d, gate_mod)


def _owner(ends, idx):
    return jnp.minimum(jnp.sum((ends[None, :] <= idx[:, None]).astype(I32), axis=1), ends.shape[0] - 1)


def _schedule(counts, max_tiles, max_items):
    t_rows = MOE_TILE
    tiles_e = (counts + t_rows - 1) // t_rows
    tile_end = jnp.cumsum(tiles_e)
    tile_start = tile_end - tiles_e
    num_tiles = tile_end[-1]
    t = jnp.arange(max_tiles, dtype=I32)
    te = _owner(tile_end, jnp.minimum(t, num_tiles - 1))
    fill = jnp.where((t < num_tiles) & (t != tile_end[te] - 1), 0, 1)

    items_e = (tiles_e + EXPERT_TILES - 1) // EXPERT_TILES
    item_end = jnp.cumsum(items_e)
    item_start = item_end - items_e
    n_items = item_end[-1]
    i = jnp.arange(max_items, dtype=I32)
    live = i < n_items
    ie = _owner(item_end, jnp.minimum(i, n_items - 1))
    local = jnp.minimum(i, n_items - 1) - item_start[ie]
    nt = jnp.where(live, jnp.clip(tiles_e[ie] - local * EXPERT_TILES, 0, EXPERT_TILES), 0)
    dead_tile0 = num_tiles + (i - n_items) * EXPERT_TILES
    zt = jnp.where(live, 0, jnp.clip(max_tiles - dead_tile0, 0, EXPERT_TILES))
    row0 = jnp.where(live, (tile_start[ie] + local * EXPERT_TILES) * t_rows,
                     jnp.clip(dead_tile0, 0, max_tiles - 1) * t_rows)
    return ((tile_start * t_rows).astype(I32), fill.astype(I32),
            (ie.astype(I32), row0.astype(I32), nt.astype(I32), zt.astype(I32)))


def _moe(parts, shift2, scale2, gate2, norm_g, router_w, router_b, w1, b1, w2, b2, layer):
    d = router_w.shape[0]
    rw_pad = jnp.zeros((d, LANES), F32).at[:, :N_EXPERTS].set(router_w)
    rb_pad = jnp.full((1, LANES), NEG_INF, F32).at[0, :N_EXPERTS].set(router_b)
    counts = jnp.zeros((1, LANES), F32)
    routed = []
    for x2d, mod_rows, per in parts:
        hp, idx, wgt, rank, counts = _router(
            x2d, shift2[mod_rows][:, None, :], scale2[mod_rows][:, None, :], norm_g[None, :],
            rw_pad, rb_pad, counts, rows_per_mod=per)
        routed.append((hp, idx[:, :TOP_K], wgt, rank[:, :TOP_K]))
    n_total = sum(x2d.shape[0] for x2d, _, _ in parts)
    max_tiles = (n_total * TOP_K) // MOE_TILE + N_EXPERTS
    n_slots = max_tiles * MOE_TILE
    max_items = N_EXPERTS + -(-max_tiles // EXPERT_TILES)
    group_start, fill_tiles, tabs = _schedule(counts[0, :N_EXPERTS].astype(I32), max_tiles, max_items)

    slots = [(group_start[idx] + rank).T for _, idx, _, rank in routed]
    hp_all = jnp.concatenate([r[0] for r in routed], axis=0) if len(routed) > 1 else routed[0][0]
    slots_all = jnp.concatenate(slots, axis=1) if len(slots) > 1 else slots[0]
    xs = _dispatch(fill_tiles, slots_all, hp_all, n_slots)
    y = _expert_ffn(tabs, xs, w1, b1, w2, b2, layer, max_items)
    outs = []
    for (x2d, mod_rows, per), sl, (_, _, wgt, _) in zip(parts, slots, routed):
        outs.append(_combine(sl, y, wgt, x2d, gate2[mod_rows][:, None, :], rows_per_mod=per))
    return outs


def _rope_tables(seq):
    t = jnp.arange(seq, dtype=I32)
    m = HEAD_DIM // 4
    inv_freq = ROPE_THETA ** (-jnp.arange(m, dtype=F32) / m)
    ang_r = (t // GRID_W).astype(F32)[:, None] * inv_freq[None, :]
    ang_c = (t % GRID_W).astype(F32)[:, None] * inv_freq[None, :]
    cos = jnp.concatenate([jnp.cos(ang_r)] * 2 + [jnp.cos(ang_c)] * 2, axis=-1)
    sin = jnp.concatenate([-jnp.sin(ang_r), jnp.sin(ang_r), -jnp.sin(ang_c), jnp.sin(ang_c)], axis=-1)
    return cos, sin


def kernel(x, c, ctx, c_ctx, ada_w, ada_b, norm_mix_g, norm_ffn_g, a_w_qkv, a_w_o, a_q_gain, a_k_gain,
           b_w_qkv, b_w_o, b_q_gain, b_k_gain, b_rel_bias, router_w, router_b, exp_w1, exp_b1, exp_w2, exp_b2):
    bsz, seq, d = x.shape
    l_ctx = ctx.shape[1]
    depth = ada_w.shape[0]
    n_heads = d // HEAD_DIM
    q_scale = HEAD_DIM ** -0.5 * LOG2E

    c_rows = jnp.zeros((SUBLANES, d), F32).at[:bsz].set(c).at[bsz].set(c_ctx)
    mod = _ada_modulation(c_rows, ada_w, ada_b)
    cos, sin = _rope_tables(seq)
    ones_tab = jnp.ones((l_ctx, HEAD_DIM), F32)
    x_rows = jnp.arange(bsz)
    c_rows_idx = jnp.full((bsz,), bsz)

    for i in range(depth):
        last = i == depth - 1
        sh1, sc1, g1, sh2, sc2, g2 = [mod[i, :, k * d:(k + 1) * d] for k in range(N_MOD)]
        j = i // 2
        if i % 2 == 0:
            w_qkv, w_o, qg, kg = a_w_qkv[j], a_w_o[j], a_q_gain[j], a_k_gain[j]
            n_q, n_k, n_v = n_heads, A_KV_HEADS, A_KV_HEADS
        else:
            w_qkv, w_o, qg, kg = b_w_qkv[j], b_w_o[j], b_q_gain[j], b_k_gain[j]
            n_q, n_k, n_v = n_heads, n_heads, n_heads
        heads_per_tile = 512 // HEAD_DIM
        gain_cols = jnp.concatenate([jnp.tile(qg * q_scale, n_q), jnp.tile(kg, n_k),
                                     jnp.ones((n_v * HEAD_DIM,), F32)])[None, :]
        flags = jnp.concatenate([jnp.ones(((n_q + n_k) // heads_per_tile,), I32),
                                 jnp.zeros((n_v // heads_per_tile,), I32)])
        rope = i % 2 == 0
        qkv_x = _qkv_project(x, sh1[x_rows][:, None, :], sc1[x_rows][:, None, :], norm_mix_g[i][None, :],
                             w_qkv, gain_cols, flags, cos, sin, rope=rope, tm=1024)
        qkv_c = _qkv_project(ctx, sh1[c_rows_idx][:, None, :], sc1[c_rows_idx][:, None, :],
                             norm_mix_g[i][None, :], w_qkv, gain_cols, flags, ones_tab, ones_tab,
                             rope=False, tm=l_ctx)
        if i % 2 == 0:
            ox = _attention(qkv_x, qkv_c, qkv_x, n_q_heads=n_q, n_kv_heads=n_k,
                            k_col0=n_q, v_col0=n_q + n_k, tq=256)
        else:
            ox = _neighborhood_attention(qkv_x, qkv_c, b_rel_bias[j], n_heads=n_heads)
        x = _out_project(ox, w_o, x, g1[x_rows][:, None, :], tm=1024)
        parts = [(x.reshape(bsz * seq, d), x_rows, seq)]
        if not last:
            oc = _attention(qkv_c, qkv_c, None, n_q_heads=n_q, n_kv_heads=n_k,
                            k_col0=n_q, v_col0=n_q + n_k, tq=l_ctx)
            ctx = _out_project(oc, w_o, ctx, g1[c_rows_idx][:, None, :], tm=l_ctx)
            parts.append((ctx.reshape(bsz * l_ctx, d), jnp.full((1,), bsz), bsz * l_ctx))
        outs = _moe(parts, sh2, sc2, g2, norm_ffn_g[i], router_w[i], router_b[i],
                    exp_w1, exp_b1, exp_w2, exp_b2, i)
        x = outs[0].reshape(bsz, seq, d)
        if not last:
            ctx = outs[1].reshape(bsz, l_ctx, d)
    return x
```

```python
import functools
import math

import jax
import jax.numpy as jnp
from jax import lax
from jax.experimental import pallas as pl
from jax.experimental.pallas import tpu as pltpu

F32 = jnp.float32
BF16 = jnp.bfloat16
I32 = jnp.int32

LANES = 128
SUBLANES = 8
VMEM_BYTES_V7X = 64 * 1024 * 1024

HEAD_DIM = 128
GRID_W = 64
A_KV_HEADS = 4
NA_WIN_ROWS = 8
NA_WIN_COLS = 16
N_EXPERTS = 32
TOP_K = 4
ROPE_THETA = 10000.0
SWIGLU_ALPHA = 1.702
SWIGLU_LIMIT = 7.0
NORM_EPS = 1e-6
N_MOD = 6
LOG2E = math.log2(math.e)
NEG_INF = float("-inf")

MOE_TILE = 256


def _vmem_limit(nbytes):
    return int(min(nbytes + 8 * 1024 * 1024, VMEM_BYTES_V7X - 8 * 1024 * 1024))


def _params(semantics, vmem_bytes):
    return pltpu.CompilerParams(dimension_semantics=semantics,
                                vmem_limit_bytes=_vmem_limit(vmem_bytes))


def _ada_kernel(c_ref, w_ref, b_ref, o_ref):
    c = c_ref[...]
    a = (c * jax.nn.sigmoid(c)).astype(BF16)
    o_ref[0] = jnp.dot(a, w_ref[0].astype(BF16), preferred_element_type=F32) + b_ref[0]


def _ada_modulation(c_rows, ada_w, ada_b):
    depth, d, n = ada_w.shape
    rows = c_rows.shape[0]
    tn = 1024
    return pl.pallas_call(
        _ada_kernel,
        grid=(depth, n // tn),
        in_specs=[pl.BlockSpec((rows, d), lambda l, j: (0, 0)),
                  pl.BlockSpec((1, d, tn), lambda l, j: (l, 0, j)),
                  pl.BlockSpec((1, 1, tn), lambda l, j: (l, 0, j))],
        out_specs=pl.BlockSpec((1, rows, tn), lambda l, j: (l, 0, j)),
        out_shape=jax.ShapeDtypeStruct((depth, rows, n), F32),
        compiler_params=_params(("arbitrary", "arbitrary"), 2 * d * tn * 4 + d * tn * 2),
        name="ada_modulation",
    )(c_rows, ada_w, ada_b.reshape(depth, 1, n))


def _norm_modulate(x, g, shift, scale):
    ms = jnp.mean(x * x, axis=-1, keepdims=True)
    y = x * lax.rsqrt(ms + NORM_EPS) * g
    return y * (1.0 + scale) + shift


def _qkv_kernel(flag_ref, x_ref, sh_ref, sc_ref, g_ref, w_ref, gain_ref, cos_ref, sin_ref,
                o_ref, h_ref, *, rope):
    n = pl.program_id(2)

    @pl.when(n == 0)
    def _():
        h_ref[...] = _norm_modulate(x_ref[...], g_ref[...], sh_ref[...], sc_ref[...]).astype(BF16)

    a = jnp.dot(h_ref[...], w_ref[...].astype(BF16), preferred_element_type=F32)
    tn = a.shape[1]

    @pl.when(flag_ref[n] == 0)
    def _():
        o_ref[...] = a.astype(BF16)

    @pl.when(flag_ref[n] == 1)
    def _():
        lane = lax.broadcasted_iota(I32, (1, HEAD_DIM), 1)
        first_half = (lane % (HEAD_DIM // 2)) < (HEAD_DIM // 4)
        for hh in range(tn // HEAD_DIM):
            sl = slice(hh * HEAD_DIM, (hh + 1) * HEAD_DIM)
            ah = a[:, sl]
            ms = jnp.mean(ah * ah, axis=-1, keepdims=True)
            yh = ah * lax.rsqrt(ms + NORM_EPS) * gain_ref[:, sl]
            if rope:
                partner = jnp.where(first_half,
                                    pltpu.roll(yh, HEAD_DIM - HEAD_DIM // 4, 1),
                                    pltpu.roll(yh, HEAD_DIM // 4, 1))
                yh = yh * cos_ref[...] + partner * sin_ref[...]
            o_ref[:, sl] = yh.astype(BF16)


def _qkv_project(x, shift, scale, g, w, gain_cols, norm_flags, cos, sin, *, rope, tm):
    b, t, d = x.shape
    n = w.shape[1]
    tn = 512
    kernel = functools.partial(_qkv_kernel, rope=rope)
    grid_spec = pltpu.PrefetchScalarGridSpec(
        num_scalar_prefetch=1,
        grid=(b, t // tm, n // tn),
        in_specs=[pl.BlockSpec((None, tm, d), lambda bi, m, j, f: (bi, m, 0)),
                  pl.BlockSpec((None, 1, d), lambda bi, m, j, f: (bi, 0, 0)),
                  pl.BlockSpec((None, 1, d), lambda bi, m, j, f: (bi, 0, 0)),
                  pl.BlockSpec((1, d), lambda bi, m, j, f: (0, 0)),
                  pl.BlockSpec((d, tn), lambda bi, m, j, f: (0, j)),
                  pl.BlockSpec((1, tn), lambda bi, m, j, f: (0, j)),
                  pl.BlockSpec((tm, HEAD_DIM), lambda bi, m, j, f: (m, 0)),
                  pl.BlockSpec((tm, HEAD_DIM), lambda bi, m, j, f: (m, 0))],
        out_specs=pl.BlockSpec((None, tm, tn), lambda bi, m, j, f: (bi, m, j)),
        scratch_shapes=[pltpu.VMEM((tm, d), BF16)])
    vmem = 2 * tm * d * 4 + tm * d * 2 + 2 * d * tn * 4 + d * tn * 2 + 4 * tm * tn * 4
    return pl.pallas_call(
        kernel, grid_spec=grid_spec,
        out_shape=jax.ShapeDtypeStruct((b, t, n), BF16),
        compiler_params=_params(("arbitrary", "arbitrary", "arbitrary"), vmem),
        name="qkv_project",
    )(norm_flags, x, shift, scale, g, w, gain_cols, cos, sin)


def _nt_dot(a, b):
    return lax.dot_general(a, b, (((1,), (1,)), ((), ())), preferred_element_type=F32)


def _values_with_ones(v_ref, dst_ref):
    dst_ref[:, :HEAD_DIM] = v_ref[...]
    dst_ref[:, HEAD_DIM:] = jnp.ones((v_ref.shape[0], HEAD_DIM), BF16)


def _attn_kernel(q_ref, kc_ref, vc_ref, *rest, group, has_x):
    if has_x:
        kx_ref, vx_ref, o_ref = rest
    else:
        (o_ref,) = rest
    kc = kc_ref[...]
    vc = vc_ref[...]
    for g in range(group):
        sl = slice(g * HEAD_DIM, (g + 1) * HEAD_DIM)
        q = q_ref[:, sl]
        sc = _nt_dot(q, kc)
        m = jnp.max(sc, axis=-1, keepdims=True)
        if has_x:
            sx = _nt_dot(q, kx_ref[...])
            m = jnp.maximum(m, jnp.max(sx, axis=-1, keepdims=True))
        pc = jnp.exp2(sc - m)
        l = jnp.sum(pc, axis=-1, keepdims=True)
        acc = jnp.dot(pc.astype(BF16), vc, preferred_element_type=F32)
        if has_x:
            px = jnp.exp2(sx - m)
            l = l + jnp.sum(px, axis=-1, keepdims=True)
            acc = acc + jnp.dot(px.astype(BF16), vx_ref[...], preferred_element_type=F32)
        o_ref[:, sl] = (acc / l).astype(BF16)


def _attention(q_src, ctx_src, x_src, *, n_q_heads, n_kv_heads, k_col0, v_col0, tq):
    b, t_q, _ = q_src.shape
    l = ctx_src.shape[1]
    group = n_q_heads // n_kv_heads
    has_x = x_src is not None
    gw = group * HEAD_DIM
    in_specs = [pl.BlockSpec((None, tq, gw), lambda bi, kv, i: (bi, i, kv)),
                pl.BlockSpec((None, l, HEAD_DIM), lambda bi, kv, i: (bi, 0, k_col0 + kv)),
                pl.BlockSpec((None, l, HEAD_DIM), lambda bi, kv, i: (bi, 0, v_col0 + kv))]
    args = [q_src, ctx_src, ctx_src]
    s = 0
    if has_x:
        s = x_src.shape[1]
        in_specs += [pl.BlockSpec((None, s, HEAD_DIM), lambda bi, kv, i: (bi, 0, k_col0 + kv)),
                     pl.BlockSpec((None, s, HEAD_DIM), lambda bi, kv, i: (bi, 0, v_col0 + kv))]
        args += [x_src, x_src]
    vmem = 4 * (l + s) * HEAD_DIM * 2 + 4 * tq * gw * 2 + 4 * tq * (l + s) * 4
    return pl.pallas_call(
        functools.partial(_attn_kernel, group=group, has_x=has_x),
        grid=(b, n_kv_heads, t_q // tq),
        in_specs=in_specs,
        out_specs=pl.BlockSpec((None, tq, gw), lambda bi, kv, i: (bi, i, kv)),
        out_shape=jax.ShapeDtypeStruct((b, t_q, n_q_heads * HEAD_DIM), BF16),
        compiler_params=_params(("arbitrary", "arbitrary", "arbitrary"), vmem),
        name="attention_x" if has_x else "attention_ctx",
    )(*args)


NA_Q_ROWS = 8
NA_BAND_ROWS = 2 * NA_WIN_ROWS
NA_DR = 2 * NA_WIN_ROWS - 1
NA_DC = 2 * NA_WIN_COLS - 1
NA_PAIR_TILES = 30


def _na_kernel(tab_ref, q_ref, k_ref, v_ref, kc_ref, vc_ref, o_ref, tile_ref, pair_ref, s_ref,
               v1_ref, vc1_ref, *, rows):
    h = pl.program_id(1)
    a = pl.program_id(2)
    nb = NA_BAND_ROWS * GRID_W

    @pl.when(a == 0)
    def _():
        _values_with_ones(v_ref, v1_ref)
        _values_with_ones(vc_ref, vc1_ref)
        qc = lax.broadcasted_iota(I32, (GRID_W, LANES), 0)
        kc = lax.broadcasted_iota(I32, (GRID_W, LANES), 1) % GRID_W
        dc = jnp.clip(kc - qc, -(NA_WIN_COLS - 1), NA_WIN_COLS - 1) + NA_WIN_COLS - 1
        c0 = jnp.clip(qc - NA_WIN_COLS // 2, 0, GRID_W - NA_WIN_COLS)
        col_ok = (kc >= c0) & (kc < c0 + NA_WIN_COLS)
        for dr in range(NA_DR):
            acc = jnp.zeros((GRID_W, LANES), F32)
            for d in range(NA_DC):
                acc = jnp.where(dc == d, tab_ref[h * (NA_DR * NA_DC) + dr * NA_DC + d] * LOG2E, acc)
            tile_ref[dr] = jnp.where(col_ok, acc, NEG_INF)
        left = lax.broadcasted_iota(I32, (GRID_W, LANES), 1) < GRID_W
        zero = jnp.zeros((GRID_W, LANES), F32)
        for p in range(NA_PAIR_TILES):
            lo = tile_ref[p - NA_WIN_ROWS] if 0 <= p - NA_WIN_ROWS < NA_DR else zero
            hi = tile_ref[p - NA_WIN_ROWS + 1] if 0 <= p - NA_WIN_ROWS + 1 < NA_DR else zero
            pair_ref[p] = jnp.where(left, lo, hi)

    wr = NA_WIN_ROWS
    kb0 = jnp.clip(a * NA_Q_ROWS - wr // 2, 0, rows - NA_BAND_ROWS)
    k_band = k_ref[pl.ds(pl.multiple_of(kb0 * GRID_W, 4 * GRID_W), nb), :]
    v_band = v1_ref[pl.ds(pl.multiple_of(kb0 * GRID_W, 4 * GRID_W), nb), :]
    q = q_ref[...]
    s_loc = _nt_dot(q, k_band)
    s_ref[:, nb:] = _nt_dot(q, kc_ref[...])

    band_row = lax.broadcasted_iota(I32, (1, nb), 1) // GRID_W
    for rq in range(NA_Q_ROWS):
        r = a * NA_Q_ROWS + rq
        r0 = jnp.clip(r - wr // 2, 0, rows - wr)
        lo = r0 - kb0
        row_mask = jnp.where((band_row >= lo) & (band_row < lo + wr), 0.0, NEG_INF)
        j0 = kb0 - r + 2 * wr - 1
        bias = jnp.concatenate([pair_ref[j0 + 2 * i] for i in range(NA_BAND_ROWS // 2)], axis=1)
        rs = slice(rq * GRID_W, (rq + 1) * GRID_W)
        s_ref[rs, :nb] = s_loc[rs, :] + bias + row_mask

    s = s_ref[...]
    m = jnp.max(s, axis=-1, keepdims=True)
    pb = jnp.exp2(s - m).astype(BF16)
    acc = (jnp.dot(pb[:, :nb], v_band, preferred_element_type=F32)
           + jnp.dot(pb[:, nb:], vc1_ref[...], preferred_element_type=F32))
    o_ref[...] = (acc[:, :HEAD_DIM] / acc[:, HEAD_DIM:]).astype(BF16)


def _neighborhood_attention(qkv_x, qkv_c, rel_bias, *, n_heads):
    b, s, _ = qkv_x.shape
    l = qkv_c.shape[1]
    rows = s // GRID_W
    nq = NA_Q_ROWS * GRID_W
    nb = NA_BAND_ROWS * GRID_W
    grid_spec = pltpu.PrefetchScalarGridSpec(
        num_scalar_prefetch=0,
        grid=(b, n_heads, rows // NA_Q_ROWS),
        in_specs=[pl.BlockSpec(memory_space=pltpu.SMEM),
                  pl.BlockSpec((None, nq, HEAD_DIM), lambda bi, h, a: (bi, a, h)),
                  pl.BlockSpec((None, s, HEAD_DIM), lambda bi, h, a: (bi, 0, n_heads + h)),
                  pl.BlockSpec((None, s, HEAD_DIM), lambda bi, h, a: (bi, 0, 2 * n_heads + h)),
                  pl.BlockSpec((None, l, HEAD_DIM), lambda bi, h, a: (bi, 0, n_heads + h)),
                  pl.BlockSpec((None, l, HEAD_DIM), lambda bi, h, a: (bi, 0, 2 * n_heads + h))],
        out_specs=pl.BlockSpec((None, nq, HEAD_DIM), lambda bi, h, a: (bi, a, h)),
        scratch_shapes=[pltpu.VMEM((NA_DR, GRID_W, LANES), F32),
                        pltpu.VMEM((NA_PAIR_TILES, GRID_W, LANES), F32),
                        pltpu.VMEM((nq, nb + l), F32),
                        pltpu.VMEM((s, 2 * HEAD_DIM), BF16),
                        pltpu.VMEM((l, 2 * HEAD_DIM), BF16)])
    vmem = 6 * (s + l) * HEAD_DIM * 2 + 4 * nq * (nb + l) * 4 + 2 * 1024 * 1024
    return pl.pallas_call(
        functools.partial(_na_kernel, rows=rows), grid_spec=grid_spec,
        out_shape=jax.ShapeDtypeStruct((b, s, n_heads * HEAD_DIM), BF16),
        compiler_params=_params(("arbitrary", "arbitrary", "arbitrary"), vmem),
        name="neighborhood_attention",
    )(rel_bias.reshape(-1), qkv_x, qkv_x, qkv_x, qkv_c, qkv_c)


def _oproj_kernel(o_ref, w_ref, x_ref, g_ref, out_ref):
    y = jnp.dot(o_ref[...], w_ref[...].astype(BF16), preferred_element_type=F32)
    out_ref[...] = x_ref[...] + g_ref[...] * y


def _out_project(o, w, x, gate, *, tm):
    b, t, k = o.shape
    d = w.shape[1]
    tn = 512
    vmem = 2 * tm * k * 2 + 2 * k * tn * 4 + k * tn * 2 + 6 * tm * tn * 4
    return pl.pallas_call(
        _oproj_kernel,
        grid=(b, t // tm, d // tn),
        in_specs=[pl.BlockSpec((None, tm, k), lambda bi, m, j: (bi, m, 0)),
                  pl.BlockSpec((k, tn), lambda bi, m, j: (0, j)),
                  pl.BlockSpec((None, tm, tn), lambda bi, m, j: (bi, m, j)),
                  pl.BlockSpec((None, 1, tn), lambda bi, m, j: (bi, 0, j))],
        out_specs=pl.BlockSpec((None, tm, tn), lambda bi, m, j: (bi, m, j)),
        out_shape=jax.ShapeDtypeStruct((b, t, d), F32),
        compiler_params=_params(("arbitrary", "arbitrary", "arbitrary"), vmem),
        name="out_project",
    )(o, w, x, gate)


ROUTER_BLOCK = 256


def _router_kernel(x_ref, sh_ref, sc_ref, g_ref, rw_ref, rb_ref, cnt_in_ref,
                   hp_ref, idx_ref, wgt_ref, rank_ref, cnt_ref, run_ref):
    i = pl.program_id(0)

    @pl.when(i == 0)
    def _():
        run_ref[...] = cnt_in_ref[...]

    h = _norm_modulate(x_ref[...], g_ref[...], sh_ref[...], sc_ref[...])
    tb, d = h.shape
    packed = pltpu.pack_elementwise([h[:, :d // 2], h[:, d // 2:]], packed_dtype=BF16)
    for k in range(SUBLANES):
        hp_ref[pl.ds(k, tb, stride=SUBLANES), :] = packed[:, k * LANES:(k + 1) * LANES]

    h_hi = h.astype(BF16)
    h_lo = (h - h_hi.astype(F32)).astype(BF16)
    logits = (jnp.dot(h_hi, rw_ref[0], preferred_element_type=F32)
              + (jnp.dot(h_hi, rw_ref[1], preferred_element_type=F32)
                 + jnp.dot(h_lo, rw_ref[0], preferred_element_type=F32))
              + rb_ref[...])
    lane = lax.broadcasted_iota(I32, (tb, LANES), 1)
    lane_f = lane.astype(F32)
    work = logits
    idx_acc = jnp.zeros((tb, LANES), F32)
    val_acc = jnp.full((tb, LANES), NEG_INF, F32)
    hots = []
    for k in range(TOP_K):
        m = jnp.max(work, axis=-1, keepdims=True)
        idx = jnp.min(jnp.where(work == m, lane_f, float(LANES)), axis=-1, keepdims=True)
        hot = lane_f == idx
        hots.append(hot)
        idx_acc = jnp.where(lane == k, idx, idx_acc)
        val_acc = jnp.where(lane == k, m, val_acc)
        work = jnp.where(hot, NEG_INF, work)

    e = jnp.exp(val_acc - jnp.max(val_acc, axis=-1, keepdims=True))
    wgt_ref[...] = e / jnp.sum(e, axis=-1, keepdims=True)
    idx_ref[...] = idx_acc.astype(I32)

    chosen = jnp.zeros((tb, LANES), F32)
    for hot in hots:
        chosen = chosen + jnp.where(hot, 1.0, 0.0)
    row = lax.broadcasted_iota(I32, (tb, tb), 0)
    col = lax.broadcasted_iota(I32, (tb, tb), 1)
    earlier = jnp.where(col < row, 1.0, 0.0).astype(BF16)
    before = jnp.dot(earlier, chosen.astype(BF16), preferred_element_type=F32) + run_ref[...]
    rank_acc = jnp.zeros((tb, LANES), F32)
    for k, hot in enumerate(hots):
        rk = jnp.sum(jnp.where(hot, before, 0.0), axis=-1, keepdims=True)
        rank_acc = jnp.where(lane == k, rk, rank_acc)
    rank_ref[...] = rank_acc.astype(I32)
    run_ref[...] = run_ref[...] + jnp.sum(chosen, axis=0, keepdims=True)
    cnt_ref[...] = run_ref[...]


def _router(x2d, shift, scale, g, rw_pad, rb_pad, counts_in, *, rows_per_mod):
    n, d = x2d.shape
    tb = ROUTER_BLOCK
    per = rows_per_mod // tb
    vmem = 2 * tb * d * 4 + 6 * tb * d * 4 + 2 * d * LANES * 4
    outs = pl.pallas_call(
        _router_kernel,
        grid=(n // tb,),
        in_specs=[pl.BlockSpec((tb, d), lambda i: (i, 0)),
                  pl.BlockSpec((None, 1, d), lambda i: (i // per, 0, 0)),
                  pl.BlockSpec((None, 1, d), lambda i: (i // per, 0, 0)),
                  pl.BlockSpec((1, d), lambda i: (0, 0)),
                  pl.BlockSpec((2, d, LANES), lambda i: (0, 0, 0)),
                  pl.BlockSpec((1, LANES), lambda i: (0, 0)),
                  pl.BlockSpec((1, LANES), lambda i: (0, 0))],
        out_specs=[pl.BlockSpec((tb * SUBLANES, LANES), lambda i: (i, 0)),
                   pl.BlockSpec((tb, LANES), lambda i: (i, 0)),
                   pl.BlockSpec((tb, LANES), lambda i: (i, 0)),
                   pl.BlockSpec((tb, LANES), lambda i: (i, 0)),
                   pl.BlockSpec((1, LANES), lambda i: (0, 0))],
        out_shape=[jax.ShapeDtypeStruct((n * SUBLANES, LANES), jnp.uint32),
                   jax.ShapeDtypeStruct((n, LANES), I32),
                   jax.ShapeDtypeStruct((n, LANES), F32),
                   jax.ShapeDtypeStruct((n, LANES), I32),
                   jax.ShapeDtypeStruct((1, LANES), F32)],
        scratch_shapes=[pltpu.VMEM((1, LANES), F32)],
        compiler_params=_params(("arbitrary",), vmem),
        name="router",
    )(x2d, shift, scale, g, rw_pad, rb_pad, counts_in)
    return outs


def _dispatch_kernel(fill_ref, slot_ref, hp_ref, xs_ref, zero_ref, sem, fill_sem):
    tb = hp_ref.shape[0] // SUBLANES
    t_rows = zero_ref.shape[0]

    @pl.when(pl.program_id(0) == 0)
    def _():
        zero_ref[...] = jnp.zeros_like(zero_ref)

        def tile_copy(t):
            return pltpu.make_async_copy(zero_ref, xs_ref.at[pl.ds(t * t_rows, t_rows), :], fill_sem)

        def fill(t, carry):
            @pl.when(fill_ref[t] == 1)
            def _():
                tile_copy(t).start()
            return carry

        def fill_wait(t, carry):
            @pl.when(fill_ref[t] == 1)
            def _():
                tile_copy(t).wait()
            return carry

        lax.fori_loop(0, fill_ref.shape[0], fill, 0)
        lax.fori_loop(0, fill_ref.shape[0], fill_wait, 0)

    def row_copy(j, s):
        src = j * SUBLANES if isinstance(j, int) else pl.multiple_of(j * SUBLANES, SUBLANES)
        dst = s * SUBLANES if isinstance(s, int) else pl.multiple_of(s * SUBLANES, SUBLANES)
        return pltpu.make_async_copy(hp_ref.at[pl.ds(src, SUBLANES), :],
                                     xs_ref.at[pl.ds(dst, SUBLANES), :], sem)

    def issue(j, carry):
        for k in range(TOP_K):
            row_copy(j, slot_ref[k, j]).start()
        return carry

    lax.fori_loop(0, tb, issue, 0)

    def drain(j, carry):
        for k in range(TOP_K):
            row_copy(0, 0).wait()
        return carry

    lax.fori_loop(0, tb, drain, 0)


def _dispatch(fill_tiles, slots_t, hp, n_slots):
    n = hp.shape[0] // SUBLANES
    tb = ROUTER_BLOCK
    grid_spec = pltpu.PrefetchScalarGridSpec(
        num_scalar_prefetch=1,
        grid=(n // tb,),
        in_specs=[pl.BlockSpec((TOP_K, tb), lambda i, f: (0, i), memory_space=pltpu.SMEM),
                  pl.BlockSpec((tb * SUBLANES, LANES), lambda i, f: (i, 0))],
        out_specs=pl.BlockSpec(memory_space=pl.ANY),
        scratch_shapes=[pltpu.VMEM((MOE_TILE * SUBLANES, LANES), jnp.uint32),
                        pltpu.SemaphoreType.DMA(()), pltpu.SemaphoreType.DMA(())])
    return pl.pallas_call(
        _dispatch_kernel, grid_spec=grid_spec,
        out_shape=jax.ShapeDtypeStruct((n_slots * SUBLANES, LANES), jnp.uint32),
        compiler_params=_params(("arbitrary",), (2 * tb + MOE_TILE) * SUBLANES * LANES * 4),
        name="dispatch",
    )(fill_tiles, slots_t, hp)


EXPERT_TILES = 6
HIDDEN_CHUNK = 256


def _swiglu_pairs(a):
    even = (lax.broadcasted_iota(I32, (1, LANES), 1) % 2) == 0
    outs = []
    for c in range(a.shape[1] // (2 * LANES)):
        ga = a[:, (2 * c) * LANES:(2 * c + 1) * LANES]
        gb = a[:, (2 * c + 1) * LANES:(2 * c + 2) * LANES]
        glu = jnp.where(even, ga, pltpu.roll(gb, 1, 1))
        lin = jnp.where(even, pltpu.roll(ga, LANES - 1, 1), gb)
        glu = jnp.minimum(glu, SWIGLU_LIMIT)
        lin = jnp.clip(lin, -SWIGLU_LIMIT, SWIGLU_LIMIT)
        outs.append((glu * jax.nn.sigmoid(SWIGLU_ALPHA * glu) * (lin + 1.0)).astype(BF16))
    return outs[0] if len(outs) == 1 else jnp.concatenate(outs, axis=1)


def _interleaved_rows_bf16(w_ref, dst_ref):
    half = LANES // 2
    for qd in range(w_ref.shape[0] // LANES):
        wa = w_ref[qd * LANES:qd * LANES + half, :]
        wc = w_ref[qd * LANES + half:(qd + 1) * LANES, :]
        packed = pltpu.pack_elementwise([wa, wc], packed_dtype=BF16)
        dst_ref[qd * LANES:(qd + 1) * LANES, :] = pltpu.bitcast(packed, BF16)


def _expert_ffn_kernel(ie_ref, row0_ref, nt_ref, zt_ref, xs_ref, w1_ref, b1_ref, w2_ref, b2_ref, y_ref,
                       xb_ref, stage_ref, ybuf_ref, w1b_ref, w2b_ref, zero_ref, pend_ref,
                       sem_in, sem_out, *, n_chunks, n_items):
    del ie_ref
    i = pl.program_id(0)
    c = pl.program_id(1)
    t_rows = MOE_TILE
    nt = nt_ref[i]
    row0 = row0_ref[i]

    def out_copy(src_ref, r_src, r_dst):
        dst = r_dst if isinstance(r_dst, int) else pl.multiple_of(r_dst, t_rows)
        return pltpu.make_async_copy(src_ref.at[pl.ds(r_src, t_rows), :],
                                     y_ref.at[pl.ds(dst, t_rows), :], sem_out)

    def wait_pending():
        def body(_, carry):
            out_copy(zero_ref, 0, 0).wait()
            return carry
        lax.fori_loop(0, pend_ref[0], body, 0)
        pend_ref[0] = 0

    @pl.when((i == 0) & (c == 0))
    def _():
        pend_ref[0] = 0
        zero_ref[...] = jnp.zeros_like(zero_ref)

    @pl.when(c == 0)
    def _():
        wait_pending()

        def in_copy(t, slot):
            src = pl.multiple_of((row0 + t * t_rows) * SUBLANES, t_rows * SUBLANES)
            return pltpu.make_async_copy(xs_ref.at[pl.ds(src, t_rows * SUBLANES), :],
                                         stage_ref.at[slot], sem_in.at[slot])

        @pl.when(nt > 0)
        def _():
            in_copy(0, 0).start()

        def load(t, carry):
            slot = t % 2
            in_copy(t, slot).wait()

            @pl.when(t + 1 < nt)
            def _():
                in_copy(t + 1, 1 - slot).start()

            r = pl.multiple_of(t * t_rows, t_rows)
            half = xb_ref.shape[1] // 2
            for k in range(SUBLANES):
                xw = stage_ref[slot, pl.ds(k, t_rows, stride=SUBLANES), :]
                lo = pltpu.unpack_elementwise(xw, index=0, packed_dtype=BF16, unpacked_dtype=F32)
                hi = pltpu.unpack_elementwise(xw, index=1, packed_dtype=BF16, unpacked_dtype=F32)
                xb_ref[pl.ds(r, t_rows), k * LANES:(k + 1) * LANES] = lo.astype(BF16)
                xb_ref[pl.ds(r, t_rows), half + k * LANES:half + (k + 1) * LANES] = hi.astype(BF16)
            ybuf_ref[pl.ds(r, t_rows), :] = jnp.broadcast_to(b2_ref[...], (t_rows, ybuf_ref.shape[1]))
            return carry

        lax.fori_loop(0, nt, load, 0)

        def zfill(t, carry):
            out_copy(zero_ref, 0, row0 + t * t_rows).start()
            return carry

        lax.fori_loop(0, zt_ref[i], zfill, 0)
        pend_ref[0] = zt_ref[i]

    @pl.when(nt > 0)
    def _():
        w1b_ref[...] = w1_ref[...].astype(BF16)
        _interleaved_rows_bf16(w2_ref, w2b_ref)
        last = c == n_chunks - 1

        def rows_step(r, n_tiles):
            n_rows = n_tiles * t_rows
            xb = xb_ref[pl.ds(r, n_rows), :]
            a = jnp.dot(xb, w1b_ref[...], preferred_element_type=F32) + b1_ref[...]
            act = _swiglu_pairs(a)
            ybuf_ref[pl.ds(r, n_rows), :] += jnp.dot(act, w2b_ref[...], preferred_element_type=F32)

            @pl.when(last)
            def _():
                for k in range(n_tiles):
                    out_copy(ybuf_ref, r + k * t_rows, row0 + r + k * t_rows).start()

        def pair(p, carry):
            rows_step(pl.multiple_of(p * (2 * t_rows), 2 * t_rows), 2)
            return carry

        lax.fori_loop(0, nt // 2, pair, 0)

        @pl.when(nt % 2 == 1)
        def _():
            rows_step(pl.multiple_of((nt - 1) * t_rows, t_rows), 1)

        @pl.when(last)
        def _():
            pend_ref[0] = nt

    @pl.when((i == n_items - 1) & (c == n_chunks - 1))
    def _():
        wait_pending()


def _expert_ffn(item_tabs, xs, w1, b1, w2, b2, layer, max_items):
    p = xs.shape[0] // SUBLANES
    half = w1.shape[2] // 2
    _, e, d, n = w1.shape
    hc = HIDDEN_CHUNK
    n_chunks = (n // 2) // hc
    rows = EXPERT_TILES * MOE_TILE

    def chunk(c, nt, i):
        return jnp.where(nt[i] > 0, c, n_chunks - 1)

    grid_spec = pltpu.PrefetchScalarGridSpec(
        num_scalar_prefetch=4,
        grid=(max_items, n_chunks),
        in_specs=[pl.BlockSpec(memory_space=pl.ANY),
                  pl.BlockSpec((None, None, d, 2 * hc), lambda i, c, ie, r0, nt, zt: (layer, ie[i], 0, chunk(c, nt, i))),
                  pl.BlockSpec((None, None, 1, 2 * hc), lambda i, c, ie, r0, nt, zt: (layer, ie[i], 0, chunk(c, nt, i))),
                  pl.BlockSpec((None, None, hc, d), lambda i, c, ie, r0, nt, zt: (layer, ie[i], chunk(c, nt, i), 0)),
                  pl.BlockSpec((None, None, 1, d), lambda i, c, ie, r0, nt, zt: (layer, ie[i], 0, 0))],
        out_specs=pl.BlockSpec(memory_space=pl.ANY),
        scratch_shapes=[pltpu.VMEM((rows, d), BF16),
                        pltpu.VMEM((2, MOE_TILE * SUBLANES, LANES), jnp.uint32),
                        pltpu.VMEM((rows, d), F32),
                        pltpu.VMEM((d, 2 * hc), BF16),
                        pltpu.VMEM((hc, d), BF16),
                        pltpu.VMEM((MOE_TILE, d), F32),
                        pltpu.SMEM((1,), I32),
                        pltpu.SemaphoreType.DMA((2,)),
                        pltpu.SemaphoreType.DMA(())])
    vmem = (rows * d * 6 + 2 * MOE_TILE * half * 4 + MOE_TILE * d * 4
            + 3 * d * 2 * hc * 4 + d * 2 * hc * 2 + 3 * hc * d * 4 + hc * d * 2
            + 8 * MOE_TILE * 2 * hc * 4 + 4 * MOE_TILE * d * 4)
    depth = w1.shape[0]
    return pl.pallas_call(
        functools.partial(_expert_ffn_kernel, n_chunks=n_chunks, n_items=max_items), grid_spec=grid_spec,
        out_shape=jax.ShapeDtypeStruct((p, d), F32),
        compiler_params=_params(("arbitrary", "arbitrary"), vmem),
        name="expert_ffn",
    )(*item_tabs, xs, w1, b1.reshape(depth, e, 1, n), w2, b2.reshape(depth, e, 1, d))


COMBINE_BLOCK = 128


def _combine_kernel(slot_ref, slot_next_ref, y_ref, w_ref, x_ref, g_ref, o_ref, buf, sem, *, nblk):
    i = pl.program_id(0)
    tb = x_ref.shape[0]

    def row_copy(s, b, k, j):
        return pltpu.make_async_copy(y_ref.at[pl.ds(s, 1), :], buf.at[b, k, pl.ds(j, 1), :], sem.at[b])

    def issue(slots, b):
        def body(j, carry):
            for k in range(TOP_K):
                row_copy(slots[k, j], b, k, j).start()
            return carry
        lax.fori_loop(0, tb, body, 0)

    @pl.when(i == 0)
    def _():
        issue(slot_ref, 0)

    @pl.when(i + 1 < nblk)
    def _():
        issue(slot_next_ref, (i + 1) % 2)

    cur = i % 2

    def drain(j, carry):
        for k in range(TOP_K):
            row_copy(0, cur, 0, 0).wait()
        return carry

    lax.fori_loop(0, tb, drain, 0)

    w = w_ref[...]
    acc = w[:, 0:1] * buf[cur, 0]
    for k in range(1, TOP_K):
        acc = acc + w[:, k:k + 1] * buf[cur, k]
    o_ref[...] = x_ref[...] + g_ref[...] * acc


def _combine(slots_t, y, gates, x2d, gate_mod, *, rows_per_mod):
    n, d = x2d.shape
    tb = COMBINE_BLOCK
    nblk = n // tb
    per = rows_per_mod // tb
    vmem = 2 * TOP_K * tb * d * 4 + 6 * tb * d * 4
    return pl.pallas_call(
        functools.partial(_combine_kernel, nblk=nblk),
        grid=(nblk,),
        in_specs=[pl.BlockSpec((TOP_K, tb), lambda i: (0, i), memory_space=pltpu.SMEM),
                  pl.BlockSpec((TOP_K, tb), lambda i: (0, jnp.minimum(i + 1, nblk - 1)),
                               memory_space=pltpu.SMEM),
                  pl.BlockSpec(memory_space=pl.ANY),
                  pl.BlockSpec((tb, LANES), lambda i: (i, 0)),
                  pl.BlockSpec((tb, d), lambda i: (i, 0)),
                  pl.BlockSpec((None, 1, d), lambda i: (i // per, 0, 0))],
        out_specs=pl.BlockSpec((tb, d), lambda i: (i, 0)),
        out_shape=jax.ShapeDtypeStruct((n, d), F32),
        scratch_shapes=[pltpu.VMEM((2, TOP_K, tb, d), F32), pltpu.SemaphoreType.DMA((2,))],
        compiler_params=_params(("arbitrary",), vmem),
        name="combine",
    )(slots_t, slots_t, y, gates, x2d, gate_mod)


def _owner(ends, idx):
    return jnp.minimum(jnp.sum((ends[None, :] <= idx[:, None]).astype(I32), axis=1), ends.shape[0] - 1)


def _schedule(counts, max_tiles, max_items):
    t_rows = MOE_TILE
    tiles_e = (counts + t_rows - 1) // t_rows
    tile_end = jnp.cumsum(tiles_e)
    tile_start = tile_end - tiles_e
    num_tiles = tile_end[-1]
    t = jnp.arange(max_tiles, dtype=I32)
    te = _owner(tile_end, jnp.minimum(t, num_tiles - 1))
    fill = jnp.where((t < num_tiles) & (t != tile_end[te] - 1), 0, 1)

    items_e = (tiles_e + EXPERT_TILES - 1) // EXPERT_TILES
    item_end = jnp.cumsum(items_e)
    item_start = item_end - items_e
    n_items = item_end[-1]
    i = jnp.arange(max_items, dtype=I32)
    live = i < n_items
    ie = _owner(item_end, jnp.minimum(i, n_items - 1))
    local = jnp.minimum(i, n_items - 1) - item_start[ie]
    nt = jnp.where(live, jnp.clip(tiles_e[ie] - local * EXPERT_TILES, 0, EXPERT_TILES), 0)
    dead_tile0 = num_tiles + (i - n_items) * EXPERT_TILES
    zt = jnp.where(live, 0, jnp.clip(max_tiles - dead_tile0, 0, EXPERT_TILES))
    row0 = jnp.where(live, (tile_start[ie] + local * EXPERT_TILES) * t_rows,
                     jnp.clip(dead_tile0, 0, max_tiles - 1) * t_rows)
    return ((tile_start * t_rows).astype(I32), fill.astype(I32),
            (ie.astype(I32), row0.astype(I32), nt.astype(I32), zt.astype(I32)))


def _moe(parts, shift2, scale2, gate2, norm_g, router_w, router_b, w1, b1, w2, b2, layer):
    d = router_w.shape[0]
    rw_f32 = jnp.zeros((d, LANES), F32).at[:, :N_EXPERTS].set(router_w)
    rw_hi = rw_f32.astype(BF16)
    rw_pad = jnp.stack([rw_hi, (rw_f32 - rw_hi.astype(F32)).astype(BF16)])
    rb_pad = jnp.full((1, LANES), NEG_INF, F32).at[0, :N_EXPERTS].set(router_b)
    counts = jnp.zeros((1, LANES), F32)
    routed = []
    for x2d, mod_rows, per in parts:
        hp, idx, wgt, rank, counts = _router(
            x2d, shift2[mod_rows][:, None, :], scale2[mod_rows][:, None, :], norm_g[None, :],
            rw_pad, rb_pad, counts, rows_per_mod=per)
        routed.append((hp, idx[:, :TOP_K], wgt, rank[:, :TOP_K]))
    n_total = sum(x2d.shape[0] for x2d, _, _ in parts)
    max_tiles = (n_total * TOP_K) // MOE_TILE + N_EXPERTS
    n_slots = max_tiles * MOE_TILE
    max_items = N_EXPERTS + -(-max_tiles // EXPERT_TILES)
    group_start, fill_tiles, tabs = _schedule(counts[0, :N_EXPERTS].astype(I32), max_tiles, max_items)

    slots = [(group_start[idx] + rank).T for _, idx, _, rank in routed]
    hp_all = jnp.concatenate([r[0] for r in routed], axis=0) if len(routed) > 1 else routed[0][0]
    slots_all = jnp.concatenate(slots, axis=1) if len(slots) > 1 else slots[0]
    xs = _dispatch(fill_tiles, slots_all, hp_all, n_slots)
    y = _expert_ffn(tabs, xs, w1, b1, w2, b2, layer, max_items)
    outs = []
    for (x2d, mod_rows, per), sl, (_, _, wgt, _) in zip(parts, slots, routed):
        outs.append(_combine(sl, y, wgt, x2d, gate2[mod_rows][:, None, :], rows_per_mod=per))
    return outs


def _rope_tables(seq):
    t = jnp.arange(seq, dtype=I32)
    m = HEAD_DIM // 4
    inv_freq = ROPE_THETA ** (-jnp.arange(m, dtype=F32) / m)
    ang_r = (t // GRID_W).astype(F32)[:, None] * inv_freq[None, :]
    ang_c = (t % GRID_W).astype(F32)[:, None] * inv_freq[None, :]
    cos = jnp.concatenate([jnp.cos(ang_r)] * 2 + [jnp.cos(ang_c)] * 2, axis=-1)
    sin = jnp.concatenate([-jnp.sin(ang_r), jnp.sin(ang_r), -jnp.sin(ang_c), jnp.sin(ang_c)], axis=-1)
    return cos, sin


def kernel(x, c, ctx, c_ctx, ada_w, ada_b, norm_mix_g, norm_ffn_g, a_w_qkv, a_w_o, a_q_gain, a_k_gain,
           b_w_qkv, b_w_o, b_q_gain, b_k_gain, b_rel_bias, router_w, router_b, exp_w1, exp_b1, exp_w2, exp_b2):
    bsz, seq, d = x.shape
    l_ctx = ctx.shape[1]
    depth = ada_w.shape[0]
    n_heads = d // HEAD_DIM
    q_scale = HEAD_DIM ** -0.5 * LOG2E

    c_rows = jnp.zeros((SUBLANES, d), F32).at[:bsz].set(c).at[bsz].set(c_ctx)
    mod = _ada_modulation(c_rows, ada_w, ada_b)
    cos, sin = _rope_tables(seq)
    ones_tab = jnp.ones((l_ctx, HEAD_DIM), F32)
    x_rows = jnp.arange(bsz)
    c_rows_idx = jnp.full((bsz,), bsz)

    for i in range(depth):
        last = i == depth - 1
        sh1, sc1, g1, sh2, sc2, g2 = [mod[i, :, k * d:(k + 1) * d] for k in range(N_MOD)]
        j = i // 2
        if i % 2 == 0:
            w_qkv, w_o, qg, kg = a_w_qkv[j], a_w_o[j], a_q_gain[j], a_k_gain[j]
            n_q, n_k, n_v = n_heads, A_KV_HEADS, A_KV_HEADS
        else:
            w_qkv, w_o, qg, kg = b_w_qkv[j], b_w_o[j], b_q_gain[j], b_k_gain[j]
            n_q, n_k, n_v = n_heads, n_heads, n_heads
        heads_per_tile = 512 // HEAD_DIM
        gain_cols = jnp.concatenate([jnp.tile(qg * q_scale, n_q), jnp.tile(kg, n_k),
                                     jnp.ones((n_v * HEAD_DIM,), F32)])[None, :]
        flags = jnp.concatenate([jnp.ones(((n_q + n_k) // heads_per_tile,), I32),
                                 jnp.zeros((n_v // heads_per_tile,), I32)])
        rope = i % 2 == 0
        qkv_x = _qkv_project(x, sh1[x_rows][:, None, :], sc1[x_rows][:, None, :], norm_mix_g[i][None, :],
                             w_qkv, gain_cols, flags, cos, sin, rope=rope, tm=1024)
        qkv_c = _qkv_project(ctx, sh1[c_rows_idx][:, None, :], sc1[c_rows_idx][:, None, :],
                             norm_mix_g[i][None, :], w_qkv, gain_cols, flags, ones_tab, ones_tab,
                             rope=False, tm=l_ctx)
        if i % 2 == 0:
            ox = _attention(qkv_x, qkv_c, qkv_x, n_q_heads=n_q, n_kv_heads=n_k,
                            k_col0=n_q, v_col0=n_q + n_k, tq=256)
        else:
            ox = _neighborhood_attention(qkv_x, qkv_c, b_rel_bias[j], n_heads=n_heads)
        x = _out_project(ox, w_o, x, g1[x_rows][:, None, :], tm=1024)
        parts = [(x.reshape(bsz * seq, d), x_rows, seq)]
        if not last:
            oc = _attention(qkv_c, qkv_c, None, n_q_heads=n_q, n_kv_heads=n_k,
                            k_col0=n_q, v_col0=n_q + n_k, tq=l_ctx)
            ctx = _out_project(oc, w_o, ctx, g1[c_rows_idx][:, None, :], tm=l_ctx)
            parts.append((ctx.reshape(bsz * l_ctx, d), jnp.full((1,), bsz), bsz * l_ctx))
        outs = _moe(parts, sh2, sc2, g2, norm_ffn_g[i], router_w[i], router_b[i],
                    exp_w1, exp_b1, exp_w2, exp_b2, i)
        x = outs[0].reshape(bsz, seq, d)
        if not last:
            ctx = outs[1].reshape(bsz, l_ctx, d)
    return x
```

```python
import functools
import math

import jax
import jax.numpy as jnp
from jax import lax
from jax.experimental import pallas as pl
from jax.experimental.pallas import tpu as pltpu

F32 = jnp.float32
BF16 = jnp.bfloat16
I32 = jnp.int32

LANES = 128
SUBLANES = 8
VMEM_BYTES_V7X = 64 * 1024 * 1024

HEAD_DIM = 128
GRID_W = 64
A_KV_HEADS = 4
NA_WIN_ROWS = 8
NA_WIN_COLS = 16
N_EXPERTS = 32
TOP_K = 4
ROPE_THETA = 10000.0
SWIGLU_ALPHA = 1.702
SWIGLU_LIMIT = 7.0
NORM_EPS = 1e-6
N_MOD = 6
LOG2E = math.log2(math.e)
NEG_INF = float("-inf")

MOE_TILE = 128


def _vmem_limit(nbytes):
    return int(min(nbytes + 8 * 1024 * 1024, VMEM_BYTES_V7X - 8 * 1024 * 1024))


def _params(semantics, vmem_bytes):
    return pltpu.CompilerParams(dimension_semantics=semantics,
                                vmem_limit_bytes=_vmem_limit(vmem_bytes))


def _ada_kernel(c_ref, w_ref, b_ref, o_ref):
    c = c_ref[...]
    a = (c * jax.nn.sigmoid(c)).astype(BF16)
    o_ref[0] = jnp.dot(a, w_ref[0].astype(BF16), preferred_element_type=F32) + b_ref[0]


def _ada_modulation(c_rows, ada_w, ada_b):
    depth, d, n = ada_w.shape
    rows = c_rows.shape[0]
    tn = 1024
    return pl.pallas_call(
        _ada_kernel,
        grid=(depth, n // tn),
        in_specs=[pl.BlockSpec((rows, d), lambda l, j: (0, 0)),
                  pl.BlockSpec((1, d, tn), lambda l, j: (l, 0, j)),
                  pl.BlockSpec((1, 1, tn), lambda l, j: (l, 0, j))],
        out_specs=pl.BlockSpec((1, rows, tn), lambda l, j: (l, 0, j)),
        out_shape=jax.ShapeDtypeStruct((depth, rows, n), F32),
        compiler_params=_params(("arbitrary", "arbitrary"), 2 * d * tn * 4 + d * tn * 2),
        name="ada_modulation",
    )(c_rows, ada_w, ada_b.reshape(depth, 1, n))


def _norm_modulate(x, g, shift, scale):
    ms = jnp.mean(x * x, axis=-1, keepdims=True)
    y = x * lax.rsqrt(ms + NORM_EPS) * g
    return y * (1.0 + scale) + shift


def _norm_mod_kernel(x_ref, sh_ref, sc_ref, g_ref, h_ref):
    h_ref[...] = _norm_modulate(x_ref[...], g_ref[...], sh_ref[...], sc_ref[...]).astype(BF16)


def _norm_mod(x, shift, scale, g, *, tm):
    b, t, d = x.shape
    return pl.pallas_call(
        _norm_mod_kernel,
        grid=(b, t // tm),
        in_specs=[pl.BlockSpec((None, tm, d), lambda bi, m: (bi, m, 0)),
                  pl.BlockSpec((None, 1, d), lambda bi, m: (bi, 0, 0)),
                  pl.BlockSpec((None, 1, d), lambda bi, m: (bi, 0, 0)),
                  pl.BlockSpec((1, d), lambda bi, m: (0, 0))],
        out_specs=pl.BlockSpec((None, tm, d), lambda bi, m: (bi, m, 0)),
        out_shape=jax.ShapeDtypeStruct((b, t, d), BF16),
        compiler_params=_params(("arbitrary", "arbitrary"), 2 * tm * d * 6 + 4 * tm * d * 4),
        name="norm_mod",
    )(x, shift, scale, g)


def _qkv_kernel(flag_ref, h_ref, w_ref, gain_ref, cos_ref, sin_ref, o_ref, wb_ref, *, rope):
    n = pl.program_id(0)

    @pl.when((pl.program_id(1) == 0) & (pl.program_id(2) == 0))
    def _():
        wb_ref[...] = w_ref[...].astype(BF16)

    a = jnp.dot(h_ref[...], wb_ref[...], preferred_element_type=F32)
    tn = a.shape[1]

    @pl.when(flag_ref[n] == 0)
    def _():
        o_ref[...] = a.astype(BF16)

    @pl.when(flag_ref[n] == 1)
    def _():
        lane = lax.broadcasted_iota(I32, (1, HEAD_DIM), 1)
        first_half = (lane % (HEAD_DIM // 2)) < (HEAD_DIM // 4)
        for hh in range(tn // HEAD_DIM):
            sl = slice(hh * HEAD_DIM, (hh + 1) * HEAD_DIM)
            ah = a[:, sl]
            ms = jnp.mean(ah * ah, axis=-1, keepdims=True)
            yh = ah * lax.rsqrt(ms + NORM_EPS) * gain_ref[:, sl]
            if rope:
                partner = jnp.where(first_half,
                                    pltpu.roll(yh, HEAD_DIM - HEAD_DIM // 4, 1),
                                    pltpu.roll(yh, HEAD_DIM // 4, 1))
                yh = yh * cos_ref[...] + partner * sin_ref[...]
            o_ref[:, sl] = yh.astype(BF16)


def _qkv_project(x, shift, scale, g, w, gain_cols, norm_flags, cos, sin, *, rope, tm):
    b, t, d = x.shape
    n = w.shape[1]
    tn = 512
    h = _norm_mod(x, shift, scale, g, tm=min(tm, 512))
    kernel = functools.partial(_qkv_kernel, rope=rope)
    grid_spec = pltpu.PrefetchScalarGridSpec(
        num_scalar_prefetch=1,
        grid=(n // tn, b, t // tm),
        in_specs=[pl.BlockSpec((None, tm, d), lambda j, bi, m, f: (bi, m, 0)),
                  pl.BlockSpec((d, tn), lambda j, bi, m, f: (0, j)),
                  pl.BlockSpec((1, tn), lambda j, bi, m, f: (0, j)),
                  pl.BlockSpec((tm, HEAD_DIM), lambda j, bi, m, f: (m, 0)),
                  pl.BlockSpec((tm, HEAD_DIM), lambda j, bi, m, f: (m, 0))],
        out_specs=pl.BlockSpec((None, tm, tn), lambda j, bi, m, f: (bi, m, j)),
        scratch_shapes=[pltpu.VMEM((d, tn), BF16)])
    vmem = 2 * tm * d * 2 + 2 * d * tn * 4 + d * tn * 2 + 6 * tm * tn * 4
    return pl.pallas_call(
        kernel, grid_spec=grid_spec,
        out_shape=jax.ShapeDtypeStruct((b, t, n), BF16),
        compiler_params=_params(("arbitrary", "arbitrary", "arbitrary"), vmem),
        name="qkv_project",
    )(norm_flags, h, w, gain_cols, cos, sin)


def _nt_dot(a, b):
    return lax.dot_general(a, b, (((1,), (1,)), ((), ())), preferred_element_type=F32)


def _values_with_ones(v_ref, dst_ref):
    dst_ref[:, :HEAD_DIM] = v_ref[...]
    dst_ref[:, HEAD_DIM:] = jnp.ones((v_ref.shape[0], HEAD_DIM), BF16)


def _attn_kernel(q_ref, kc_ref, vc_ref, *rest, group, has_x):
    if has_x:
        kx_ref, vx_ref, o_ref = rest
    else:
        (o_ref,) = rest
    kc = kc_ref[...]
    vc = vc_ref[...]
    for g in range(group):
        sl = slice(g * HEAD_DIM, (g + 1) * HEAD_DIM)
        q = q_ref[:, sl]
        sc = _nt_dot(q, kc)
        m = jnp.max(sc, axis=-1, keepdims=True)
        if has_x:
            sx = _nt_dot(q, kx_ref[...])
            m = jnp.maximum(m, jnp.max(sx, axis=-1, keepdims=True))
        pc = jnp.exp2(sc - m)
        l = jnp.sum(pc, axis=-1, keepdims=True)
        acc = jnp.dot(pc.astype(BF16), vc, preferred_element_type=F32)
        if has_x:
            px = jnp.exp2(sx - m)
            l = l + jnp.sum(px, axis=-1, keepdims=True)
            acc = acc + jnp.dot(px.astype(BF16), vx_ref[...], preferred_element_type=F32)
        o_ref[:, sl] = (acc / l).astype(BF16)


def _attention(q_src, ctx_src, x_src, *, n_q_heads, n_kv_heads, k_col0, v_col0, tq):
    b, t_q, _ = q_src.shape
    l = ctx_src.shape[1]
    group = n_q_heads // n_kv_heads
    has_x = x_src is not None
    gw = group * HEAD_DIM
    in_specs = [pl.BlockSpec((None, tq, gw), lambda bi, kv, i: (bi, i, kv)),
                pl.BlockSpec((None, l, HEAD_DIM), lambda bi, kv, i: (bi, 0, k_col0 + kv)),
                pl.BlockSpec((None, l, HEAD_DIM), lambda bi, kv, i: (bi, 0, v_col0 + kv))]
    args = [q_src, ctx_src, ctx_src]
    s = 0
    if has_x:
        s = x_src.shape[1]
        in_specs += [pl.BlockSpec((None, s, HEAD_DIM), lambda bi, kv, i: (bi, 0, k_col0 + kv)),
                     pl.BlockSpec((None, s, HEAD_DIM), lambda bi, kv, i: (bi, 0, v_col0 + kv))]
        args += [x_src, x_src]
    vmem = 4 * (l + s) * HEAD_DIM * 2 + 4 * tq * gw * 2 + 4 * tq * (l + s) * 4
    return pl.pallas_call(
        functools.partial(_attn_kernel, group=group, has_x=has_x),
        grid=(b, n_kv_heads, t_q // tq),
        in_specs=in_specs,
        out_specs=pl.BlockSpec((None, tq, gw), lambda bi, kv, i: (bi, i, kv)),
        out_shape=jax.ShapeDtypeStruct((b, t_q, n_q_heads * HEAD_DIM), BF16),
        compiler_params=_params(("arbitrary", "arbitrary", "arbitrary"), vmem),
        name="attention_x" if has_x else "attention_ctx",
    )(*args)


NA_Q_ROWS = 8
NA_BAND_ROWS = 2 * NA_WIN_ROWS
NA_DR = 2 * NA_WIN_ROWS - 1
NA_DC = 2 * NA_WIN_COLS - 1
NA_PAIR_TILES = 30


def _na_kernel(tab_ref, q_ref, k_ref, v_ref, kc_ref, vc_ref, o_ref, tile_ref, pair_ref, s_ref,
               v1_ref, vc1_ref, *, rows):
    h = pl.program_id(1)
    a = pl.program_id(2)
    nb = NA_BAND_ROWS * GRID_W

    @pl.when(a == 0)
    def _():
        _values_with_ones(v_ref, v1_ref)
        _values_with_ones(vc_ref, vc1_ref)
        qc = lax.broadcasted_iota(I32, (GRID_W, LANES), 0)
        kc = lax.broadcasted_iota(I32, (GRID_W, LANES), 1) % GRID_W
        dc = jnp.clip(kc - qc, -(NA_WIN_COLS - 1), NA_WIN_COLS - 1) + NA_WIN_COLS - 1
        c0 = jnp.clip(qc - NA_WIN_COLS // 2, 0, GRID_W - NA_WIN_COLS)
        col_ok = (kc >= c0) & (kc < c0 + NA_WIN_COLS)
        for dr in range(NA_DR):
            acc = jnp.zeros((GRID_W, LANES), F32)
            for d in range(NA_DC):
                acc = jnp.where(dc == d, tab_ref[h * (NA_DR * NA_DC) + dr * NA_DC + d] * LOG2E, acc)
            tile_ref[dr] = jnp.where(col_ok, acc, NEG_INF)
        left = lax.broadcasted_iota(I32, (GRID_W, LANES), 1) < GRID_W
        zero = jnp.zeros((GRID_W, LANES), F32)
        for p in range(NA_PAIR_TILES):
            lo = tile_ref[p - NA_WIN_ROWS] if 0 <= p - NA_WIN_ROWS < NA_DR else zero
            hi = tile_ref[p - NA_WIN_ROWS + 1] if 0 <= p - NA_WIN_ROWS + 1 < NA_DR else zero
            pair_ref[p] = jnp.where(left, lo, hi)

    wr = NA_WIN_ROWS
    kb0 = jnp.clip(a * NA_Q_ROWS - wr // 2, 0, rows - NA_BAND_ROWS)
    k_band = k_ref[pl.ds(pl.multiple_of(kb0 * GRID_W, 4 * GRID_W), nb), :]
    v_band = v1_ref[pl.ds(pl.multiple_of(kb0 * GRID_W, 4 * GRID_W), nb), :]
    q = q_ref[...]
    s_loc = _nt_dot(q, k_band)
    s_ref[:, nb:] = _nt_dot(q, kc_ref[...])

    band_row = lax.broadcasted_iota(I32, (1, nb), 1) // GRID_W
    for rq in range(NA_Q_ROWS):
        r = a * NA_Q_ROWS + rq
        r0 = jnp.clip(r - wr // 2, 0, rows - wr)
        lo = r0 - kb0
        row_mask = jnp.where((band_row >= lo) & (band_row < lo + wr), 0.0, NEG_INF)
        j0 = kb0 - r + 2 * wr - 1
        bias = jnp.concatenate([pair_ref[j0 + 2 * i] for i in range(NA_BAND_ROWS // 2)], axis=1)
        rs = slice(rq * GRID_W, (rq + 1) * GRID_W)
        s_ref[rs, :nb] = s_loc[rs, :] + bias + row_mask

    s = s_ref[...]
    m = jnp.max(s, axis=-1, keepdims=True)
    pb = jnp.exp2(s - m).astype(BF16)
    acc = (jnp.dot(pb[:, :nb], v_band, preferred_element_type=F32)
           + jnp.dot(pb[:, nb:], vc1_ref[...], preferred_element_type=F32))
    o_ref[...] = (acc[:, :HEAD_DIM] / acc[:, HEAD_DIM:]).astype(BF16)


def _neighborhood_attention(qkv_x, qkv_c, rel_bias, *, n_heads):
    b, s, _ = qkv_x.shape
    l = qkv_c.shape[1]
    rows = s // GRID_W
    nq = NA_Q_ROWS * GRID_W
    nb = NA_BAND_ROWS * GRID_W
    grid_spec = pltpu.PrefetchScalarGridSpec(
        num_scalar_prefetch=0,
        grid=(b, n_heads, rows // NA_Q_ROWS),
        in_specs=[pl.BlockSpec(memory_space=pltpu.SMEM),
                  pl.BlockSpec((None, nq, HEAD_DIM), lambda bi, h, a: (bi, a, h)),
                  pl.BlockSpec((None, s, HEAD_DIM), lambda bi, h, a: (bi, 0, n_heads + h)),
                  pl.BlockSpec((None, s, HEAD_DIM), lambda bi, h, a: (bi, 0, 2 * n_heads + h)),
                  pl.BlockSpec((None, l, HEAD_DIM), lambda bi, h, a: (bi, 0, n_heads + h)),
                  pl.BlockSpec((None, l, HEAD_DIM), lambda bi, h, a: (bi, 0, 2 * n_heads + h))],
        out_specs=pl.BlockSpec((None, nq, HEAD_DIM), lambda bi, h, a: (bi, a, h)),
        scratch_shapes=[pltpu.VMEM((NA_DR, GRID_W, LANES), F32),
                        pltpu.VMEM((NA_PAIR_TILES, GRID_W, LANES), F32),
                        pltpu.VMEM((nq, nb + l), F32),
                        pltpu.VMEM((s, 2 * HEAD_DIM), BF16),
                        pltpu.VMEM((l, 2 * HEAD_DIM), BF16)])
    vmem = 6 * (s + l) * HEAD_DIM * 2 + 4 * nq * (nb + l) * 4 + 2 * 1024 * 1024
    return pl.pallas_call(
        functools.partial(_na_kernel, rows=rows), grid_spec=grid_spec,
        out_shape=jax.ShapeDtypeStruct((b, s, n_heads * HEAD_DIM), BF16),
        compiler_params=_params(("arbitrary", "arbitrary", "arbitrary"), vmem),
        name="neighborhood_attention",
    )(rel_bias.reshape(-1), qkv_x, qkv_x, qkv_x, qkv_c, qkv_c)


def _oproj_kernel(o_ref, w_ref, x_ref, g_ref, out_ref, wb_ref):
    @pl.when((pl.program_id(1) == 0) & (pl.program_id(2) == 0))
    def _():
        wb_ref[...] = w_ref[...].astype(BF16)

    y = jnp.dot(o_ref[...], wb_ref[...], preferred_element_type=F32)
    out_ref[...] = x_ref[...] + g_ref[...] * y


def _out_project(o, w, x, gate, *, tm):
    b, t, k = o.shape
    d = w.shape[1]
    tn = 512
    vmem = 2 * tm * k * 2 + 2 * k * tn * 4 + k * tn * 2 + 6 * tm * tn * 4
    return pl.pallas_call(
        _oproj_kernel,
        grid=(d // tn, b, t // tm),
        in_specs=[pl.BlockSpec((None, tm, k), lambda j, bi, m: (bi, m, 0)),
                  pl.BlockSpec((k, tn), lambda j, bi, m: (0, j)),
                  pl.BlockSpec((None, tm, tn), lambda j, bi, m: (bi, m, j)),
                  pl.BlockSpec((None, 1, tn), lambda j, bi, m: (bi, 0, j))],
        out_specs=pl.BlockSpec((None, tm, tn), lambda j, bi, m: (bi, m, j)),
        scratch_shapes=[pltpu.VMEM((k, tn), BF16)],
        out_shape=jax.ShapeDtypeStruct((b, t, d), F32),
        compiler_params=_params(("arbitrary", "arbitrary", "arbitrary"), vmem),
        name="out_project",
    )(o, w, x, gate)


ROUTER_BLOCK = 256


def _router_kernel(x_ref, sh_ref, sc_ref, g_ref, rw_ref, rb_ref, cnt_in_ref,
                   hp_ref, idx_ref, wgt_ref, rank_ref, cnt_ref, run_ref):
    i = pl.program_id(0)

    @pl.when(i == 0)
    def _():
        run_ref[...] = cnt_in_ref[...]

    h = _norm_modulate(x_ref[...], g_ref[...], sh_ref[...], sc_ref[...])
    tb, d = h.shape
    packed = pltpu.pack_elementwise([h[:, :d // 2], h[:, d // 2:]], packed_dtype=BF16)
    for k in range(SUBLANES):
        hp_ref[pl.ds(k, tb, stride=SUBLANES), :] = packed[:, k * LANES:(k + 1) * LANES]

    h_hi = h.astype(BF16)
    h_lo = (h - h_hi.astype(F32)).astype(BF16)
    logits = (jnp.dot(h_hi, rw_ref[0], preferred_element_type=F32)
              + (jnp.dot(h_hi, rw_ref[1], preferred_element_type=F32)
                 + jnp.dot(h_lo, rw_ref[0], preferred_element_type=F32))
              + rb_ref[...])
    lane = lax.broadcasted_iota(I32, (tb, LANES), 1)
    lane_f = lane.astype(F32)
    work = logits
    idx_acc = jnp.zeros((tb, LANES), F32)
    val_acc = jnp.full((tb, LANES), NEG_INF, F32)
    hots = []
    for k in range(TOP_K):
        m = jnp.max(work, axis=-1, keepdims=True)
        idx = jnp.min(jnp.where(work == m, lane_f, float(LANES)), axis=-1, keepdims=True)
        hot = lane_f == idx
        hots.append(hot)
        idx_acc = jnp.where(lane == k, idx, idx_acc)
        val_acc = jnp.where(lane == k, m, val_acc)
        work = jnp.where(hot, NEG_INF, work)

    e = jnp.exp(val_acc - jnp.max(val_acc, axis=-1, keepdims=True))
    wgt_ref[...] = e / jnp.sum(e, axis=-1, keepdims=True)
    idx_ref[...] = idx_acc.astype(I32)

    chosen = jnp.zeros((tb, LANES), F32)
    for hot in hots:
        chosen = chosen + jnp.where(hot, 1.0, 0.0)
    row = lax.broadcasted_iota(I32, (tb, tb), 0)
    col = lax.broadcasted_iota(I32, (tb, tb), 1)
    earlier = jnp.where(col < row, 1.0, 0.0).astype(BF16)
    before = jnp.dot(earlier, chosen.astype(BF16), preferred_element_type=F32) + run_ref[...]
    rank_acc = jnp.zeros((tb, LANES), F32)
    for k, hot in enumerate(hots):
        rk = jnp.sum(jnp.where(hot, before, 0.0), axis=-1, keepdims=True)
        rank_acc = jnp.where(lane == k, rk, rank_acc)
    rank_ref[...] = rank_acc.astype(I32)
    run_ref[...] = run_ref[...] + jnp.sum(chosen, axis=0, keepdims=True)
    cnt_ref[...] = run_ref[...]


def _router(x2d, shift, scale, g, rw_pad, rb_pad, counts_in, *, rows_per_mod):
    n, d = x2d.shape
    tb = ROUTER_BLOCK
    per = rows_per_mod // tb
    vmem = 2 * tb * d * 4 + 6 * tb * d * 4 + 2 * d * LANES * 4
    outs = pl.pallas_call(
        _router_kernel,
        grid=(n // tb,),
        in_specs=[pl.BlockSpec((tb, d), lambda i: (i, 0)),
                  pl.BlockSpec((None, 1, d), lambda i: (i // per, 0, 0)),
                  pl.BlockSpec((None, 1, d), lambda i: (i // per, 0, 0)),
                  pl.BlockSpec((1, d), lambda i: (0, 0)),
                  pl.BlockSpec((2, d, LANES), lambda i: (0, 0, 0)),
                  pl.BlockSpec((1, LANES), lambda i: (0, 0)),
                  pl.BlockSpec((1, LANES), lambda i: (0, 0))],
        out_specs=[pl.BlockSpec((tb * SUBLANES, LANES), lambda i: (i, 0)),
                   pl.BlockSpec((tb, LANES), lambda i: (i, 0)),
                   pl.BlockSpec((tb, LANES), lambda i: (i, 0)),
                   pl.BlockSpec((tb, LANES), lambda i: (i, 0)),
                   pl.BlockSpec((1, LANES), lambda i: (0, 0))],
        out_shape=[jax.ShapeDtypeStruct((n * SUBLANES, LANES), jnp.uint32),
                   jax.ShapeDtypeStruct((n, LANES), I32),
                   jax.ShapeDtypeStruct((n, LANES), F32),
                   jax.ShapeDtypeStruct((n, LANES), I32),
                   jax.ShapeDtypeStruct((1, LANES), F32)],
        scratch_shapes=[pltpu.VMEM((1, LANES), F32)],
        compiler_params=_params(("arbitrary",), vmem),
        name="router",
    )(x2d, shift, scale, g, rw_pad, rb_pad, counts_in)
    return outs


def _dispatch_kernel(fill_ref, slot_ref, hp_ref, xs_ref, zero_ref, sem, fill_sem):
    tb = hp_ref.shape[0] // SUBLANES
    t_rows = zero_ref.shape[0]

    @pl.when(pl.program_id(0) == 0)
    def _():
        zero_ref[...] = jnp.zeros_like(zero_ref)

        def tile_copy(t):
            return pltpu.make_async_copy(zero_ref, xs_ref.at[pl.ds(t * t_rows, t_rows), :], fill_sem)

        def fill(t, carry):
            @pl.when(fill_ref[t] == 1)
            def _():
                tile_copy(t).start()
            return carry

        def fill_wait(t, carry):
            @pl.when(fill_ref[t] == 1)
            def _():
                tile_copy(t).wait()
            return carry

        lax.fori_loop(0, fill_ref.shape[0], fill, 0)
        lax.fori_loop(0, fill_ref.shape[0], fill_wait, 0)

    def row_copy(j, s):
        src = j * SUBLANES if isinstance(j, int) else pl.multiple_of(j * SUBLANES, SUBLANES)
        dst = s * SUBLANES if isinstance(s, int) else pl.multiple_of(s * SUBLANES, SUBLANES)
        return pltpu.make_async_copy(hp_ref.at[pl.ds(src, SUBLANES), :],
                                     xs_ref.at[pl.ds(dst, SUBLANES), :], sem)

    def issue(j, carry):
        for k in range(TOP_K):
            row_copy(j, slot_ref[k, j]).start()
        return carry

    lax.fori_loop(0, tb, issue, 0)

    def drain(j, carry):
        for k in range(TOP_K):
            row_copy(0, 0).wait()
        return carry

    lax.fori_loop(0, tb, drain, 0)


def _dispatch(fill_tiles, slots_t, hp, n_slots):
    n = hp.shape[0] // SUBLANES
    tb = ROUTER_BLOCK
    grid_spec = pltpu.PrefetchScalarGridSpec(
        num_scalar_prefetch=1,
        grid=(n // tb,),
        in_specs=[pl.BlockSpec((TOP_K, tb), lambda i, f: (0, i), memory_space=pltpu.SMEM),
                  pl.BlockSpec((tb * SUBLANES, LANES), lambda i, f: (i, 0))],
        out_specs=pl.BlockSpec(memory_space=pl.ANY),
        scratch_shapes=[pltpu.VMEM((MOE_TILE * SUBLANES, LANES), jnp.uint32),
                        pltpu.SemaphoreType.DMA(()), pltpu.SemaphoreType.DMA(())])
    return pl.pallas_call(
        _dispatch_kernel, grid_spec=grid_spec,
        out_shape=jax.ShapeDtypeStruct((n_slots * SUBLANES, LANES), jnp.uint32),
        compiler_params=_params(("arbitrary",), (2 * tb + MOE_TILE) * SUBLANES * LANES * 4),
        name="dispatch",
    )(fill_tiles, slots_t, hp)


EXPERT_TILES = 10
HIDDEN_CHUNK = 512


def _swiglu_pairs(a):
    even = (lax.broadcasted_iota(I32, (1, LANES), 1) % 2) == 0
    outs = []
    for c in range(a.shape[1] // (2 * LANES)):
        ga = a[:, (2 * c) * LANES:(2 * c + 1) * LANES]
        gb = a[:, (2 * c + 1) * LANES:(2 * c + 2) * LANES]
        glu = jnp.where(even, ga, pltpu.roll(gb, 1, 1))
        lin = jnp.where(even, pltpu.roll(ga, LANES - 1, 1), gb)
        glu = jnp.minimum(glu, SWIGLU_LIMIT)
        lin = jnp.clip(lin, -SWIGLU_LIMIT, SWIGLU_LIMIT)
        outs.append((glu * jax.nn.sigmoid(SWIGLU_ALPHA * glu) * (lin + 1.0)).astype(BF16))
    return outs[0] if len(outs) == 1 else jnp.concatenate(outs, axis=1)


def _interleaved_rows_bf16(w_ref, dst_ref):
    half = LANES // 2
    for qd in range(w_ref.shape[0] // LANES):
        wa = w_ref[qd * LANES:qd * LANES + half, :]
        wc = w_ref[qd * LANES + half:(qd + 1) * LANES, :]
        packed = pltpu.pack_elementwise([wa, wc], packed_dtype=BF16)
        dst_ref[qd * LANES:(qd + 1) * LANES, :] = pltpu.bitcast(packed, BF16)


def _expert_ffn_kernel(ie_ref, row0_ref, nt_ref, zt_ref, xs_ref, w1_ref, b1_ref, w2_ref, b2_ref, y_ref,
                       xb_ref, stage_ref, ybuf_ref, w1b_ref, w2b_ref, pend_ref,
                       sem_in, sem_out, *, n_chunks, n_items):
    del ie_ref
    i = pl.program_id(0)
    c = pl.program_id(1)
    t_rows = MOE_TILE
    nt = nt_ref[i]
    row0 = row0_ref[i]

    def out_copy(src_ref, r_src, r_dst):
        dst = r_dst if isinstance(r_dst, int) else pl.multiple_of(r_dst, t_rows)
        return pltpu.make_async_copy(src_ref.at[pl.ds(r_src, t_rows), :],
                                     y_ref.at[pl.ds(dst, t_rows), :], sem_out)

    def wait_pending():
        def body(_, carry):
            out_copy(ybuf_ref, 0, 0).wait()
            return carry
        lax.fori_loop(0, pend_ref[0], body, 0)
        pend_ref[0] = 0

    @pl.when((i == 0) & (c == 0))
    def _():
        pend_ref[0] = 0

    @pl.when(c == 0)
    def _():
        wait_pending()

        def in_copy(t, slot):
            src = pl.multiple_of((row0 + t * t_rows) * SUBLANES, t_rows * SUBLANES)
            return pltpu.make_async_copy(xs_ref.at[pl.ds(src, t_rows * SUBLANES), :],
                                         stage_ref.at[slot], sem_in.at[slot])

        @pl.when(nt > 0)
        def _():
            in_copy(0, 0).start()

        def load(t, carry):
            slot = t % 2
            in_copy(t, slot).wait()

            @pl.when(t + 1 < nt)
            def _():
                in_copy(t + 1, 1 - slot).start()

            r = pl.multiple_of(t * t_rows, t_rows)
            half = xb_ref.shape[1] // 2
            for k in range(SUBLANES):
                xw = stage_ref[slot, pl.ds(k, t_rows, stride=SUBLANES), :]
                lo = pltpu.unpack_elementwise(xw, index=0, packed_dtype=BF16, unpacked_dtype=F32)
                hi = pltpu.unpack_elementwise(xw, index=1, packed_dtype=BF16, unpacked_dtype=F32)
                xb_ref[pl.ds(r, t_rows), k * LANES:(k + 1) * LANES] = lo.astype(BF16)
                xb_ref[pl.ds(r, t_rows), half + k * LANES:half + (k + 1) * LANES] = hi.astype(BF16)
            ybuf_ref[pl.ds(r, t_rows), :] = jnp.broadcast_to(b2_ref[...], (t_rows, ybuf_ref.shape[1]))
            return carry

        lax.fori_loop(0, nt, load, 0)

        def zfill(t, carry):
            out_copy(ybuf_ref, 0, row0 + t * t_rows).start()
            return carry

        @pl.when(zt_ref[i] > 0)
        def _():
            ybuf_ref[0:t_rows, :] = jnp.zeros((t_rows, ybuf_ref.shape[1]), F32)

        lax.fori_loop(0, zt_ref[i], zfill, 0)
        pend_ref[0] = zt_ref[i]

    @pl.when(nt > 0)
    def _():
        w1b_ref[...] = w1_ref[...].astype(BF16)
        _interleaved_rows_bf16(w2_ref, w2b_ref)
        last = c == n_chunks - 1

        def rows_step(r, n_tiles):
            n_rows = n_tiles * t_rows
            xb = xb_ref[pl.ds(r, n_rows), :]
            a = jnp.dot(xb, w1b_ref[...], preferred_element_type=F32) + b1_ref[...]
            act = _swiglu_pairs(a)
            ybuf_ref[pl.ds(r, n_rows), :] += jnp.dot(act, w2b_ref[...], preferred_element_type=F32)

            @pl.when(last)
            def _():
                for k in range(n_tiles):
                    out_copy(ybuf_ref, r + k * t_rows, row0 + r + k * t_rows).start()

        def quad(p, carry):
            rows_step(pl.multiple_of(p * (4 * t_rows), 4 * t_rows), 4)
            return carry

        lax.fori_loop(0, nt // 4, quad, 0)

        @pl.when(nt % 4 >= 2)
        def _():
            rows_step(pl.multiple_of((nt // 4) * (4 * t_rows), 2 * t_rows), 2)

        @pl.when(nt % 2 == 1)
        def _():
            rows_step(pl.multiple_of((nt - 1) * t_rows, t_rows), 1)

        @pl.when(last)
        def _():
            pend_ref[0] = nt

    @pl.when((i == n_items - 1) & (c == n_chunks - 1))
    def _():
        wait_pending()


def _expert_ffn(item_tabs, xs, w1, b1, w2, b2, layer, max_items):
    p = xs.shape[0] // SUBLANES
    half = w1.shape[2] // 2
    _, e, d, n = w1.shape
    hc = HIDDEN_CHUNK
    n_chunks = (n // 2) // hc
    rows = EXPERT_TILES * MOE_TILE

    def chunk(c, nt, i):
        return jnp.where(nt[i] > 0, c, n_chunks - 1)

    grid_spec = pltpu.PrefetchScalarGridSpec(
        num_scalar_prefetch=4,
        grid=(max_items, n_chunks),
        in_specs=[pl.BlockSpec(memory_space=pl.ANY),
                  pl.BlockSpec((None, None, d, 2 * hc), lambda i, c, ie, r0, nt, zt: (layer, ie[i], 0, chunk(c, nt, i))),
                  pl.BlockSpec((None, None, 1, 2 * hc), lambda i, c, ie, r0, nt, zt: (layer, ie[i], 0, chunk(c, nt, i))),
                  pl.BlockSpec((None, None, hc, d), lambda i, c, ie, r0, nt, zt: (layer, ie[i], chunk(c, nt, i), 0)),
                  pl.BlockSpec((None, None, 1, d), lambda i, c, ie, r0, nt, zt: (layer, ie[i], 0, 0))],
        out_specs=pl.BlockSpec(memory_space=pl.ANY),
        scratch_shapes=[pltpu.VMEM((rows, d), BF16),
                        pltpu.VMEM((2, MOE_TILE * SUBLANES, LANES), jnp.uint32),
                        pltpu.VMEM((rows, d), F32),
                        pltpu.VMEM((d, 2 * hc), BF16),
                        pltpu.VMEM((hc, d), BF16),
                        pltpu.SMEM((1,), I32),
                        pltpu.SemaphoreType.DMA((2,)),
                        pltpu.SemaphoreType.DMA(())])
    vmem = (rows * d * 6 + 2 * MOE_TILE * half * 4 + MOE_TILE * d * 4
            + 3 * d * 2 * hc * 4 + d * 2 * hc * 2 + 3 * hc * d * 4 + hc * d * 2
            + 8 * MOE_TILE * 2 * hc * 4 + 4 * MOE_TILE * d * 4)
    depth = w1.shape[0]
    return pl.pallas_call(
        functools.partial(_expert_ffn_kernel, n_chunks=n_chunks, n_items=max_items), grid_spec=grid_spec,
        out_shape=jax.ShapeDtypeStruct((p, d), F32),
        compiler_params=_params(("arbitrary", "arbitrary"), vmem),
        name="expert_ffn",
    )(*item_tabs, xs, w1, b1.reshape(depth, e, 1, n), w2, b2.reshape(depth, e, 1, d))


COMBINE_BLOCK = 128


def _combine_kernel(slot_ref, slot_next_ref, y_ref, w_ref, x_ref, g_ref, o_ref, buf, sem, *, nblk):
    i = pl.program_id(0)
    tb = x_ref.shape[0]

    def row_copy(s, b, k, j):
        return pltpu.make_async_copy(y_ref.at[pl.ds(s, 1), :], buf.at[b, k, pl.ds(j, 1), :], sem.at[b])

    def issue(slots, b):
        def body(j, carry):
            for k in range(TOP_K):
                row_copy(slots[k, j], b, k, j).start()
            return carry
        lax.fori_loop(0, tb, body, 0)

    @pl.when(i == 0)
    def _():
        issue(slot_ref, 0)

    @pl.when(i + 1 < nblk)
    def _():
        issue(slot_next_ref, (i + 1) % 2)

    cur = i % 2

    def drain(j, carry):
        for k in range(TOP_K):
            row_copy(0, cur, 0, 0).wait()
        return carry

    lax.fori_loop(0, tb, drain, 0)

    w = w_ref[...]
    acc = w[:, 0:1] * buf[cur, 0]
    for k in range(1, TOP_K):
        acc = acc + w[:, k:k + 1] * buf[cur, k]
    o_ref[...] = x_ref[...] + g_ref[...] * acc


def _combine(slots_t, y, gates, x2d, gate_mod, *, rows_per_mod):
    n, d = x2d.shape
    tb = COMBINE_BLOCK
    nblk = n // tb
    per = rows_per_mod // tb
    vmem = 2 * TOP_K * tb * d * 4 + 6 * tb * d * 4
    return pl.pallas_call(
        functools.partial(_combine_kernel, nblk=nblk),
        grid=(nblk,),
        in_specs=[pl.BlockSpec((TOP_K, tb), lambda i: (0, i), memory_space=pltpu.SMEM),
                  pl.BlockSpec((TOP_K, tb), lambda i: (0, jnp.minimum(i + 1, nblk - 1)),
                               memory_space=pltpu.SMEM),
                  pl.BlockSpec(memory_space=pl.ANY),
                  pl.BlockSpec((tb, LANES), lambda i: (i, 0)),
                  pl.BlockSpec((tb, d), lambda i: (i, 0)),
                  pl.BlockSpec((None, 1, d), lambda i: (i // per, 0, 0))],
        out_specs=pl.BlockSpec((tb, d), lambda i: (i, 0)),
        out_shape=jax.ShapeDtypeStruct((n, d), F32),
        scratch_shapes=[pltpu.VMEM((2, TOP_K, tb, d), F32), pltpu.SemaphoreType.DMA((2,))],
        compiler_params=_params(("arbitrary",), vmem),
        name="combine",
    )(slots_t, slots_t, y, gates, x2d, gate_mod)


def _owner(ends, idx):
    return jnp.minimum(jnp.sum((ends[None, :] <= idx[:, None]).astype(I32), axis=1), ends.shape[0] - 1)


def _schedule(counts, max_tiles, max_items):
    t_rows = MOE_TILE
    tiles_e = (counts + t_rows - 1) // t_rows
    tile_end = jnp.cumsum(tiles_e)
    tile_start = tile_end - tiles_e
    num_tiles = tile_end[-1]
    t = jnp.arange(max_tiles, dtype=I32)
    te = _owner(tile_end, jnp.minimum(t, num_tiles - 1))
    fill = jnp.where((t < num_tiles) & (t != tile_end[te] - 1), 0, 1)

    items_e = (tiles_e + EXPERT_TILES - 1) // EXPERT_TILES
    item_end = jnp.cumsum(items_e)
    item_start = item_end - items_e
    n_items = item_end[-1]
    i = jnp.arange(max_items, dtype=I32)
    live = i < n_items
    ie = _owner(item_end, jnp.minimum(i, n_items - 1))
    local = jnp.minimum(i, n_items - 1) - item_start[ie]
    nt = jnp.where(live, jnp.clip(tiles_e[ie] - local * EXPERT_TILES, 0, EXPERT_TILES), 0)
    dead_tile0 = num_tiles + (i - n_items) * EXPERT_TILES
    zt = jnp.where(live, 0, jnp.clip(max_tiles - dead_tile0, 0, EXPERT_TILES))
    row0 = jnp.where(live, (tile_start[ie] + local * EXPERT_TILES) * t_rows,
                     jnp.clip(dead_tile0, 0, max_tiles - 1) * t_rows)
    return ((tile_start * t_rows).astype(I32), fill.astype(I32),
            (ie.astype(I32), row0.astype(I32), nt.astype(I32), zt.astype(I32)))


def _moe(parts, shift2, scale2, gate2, norm_g, router_w, router_b, w1, b1, w2, b2, layer):
    d = router_w.shape[0]
    rw_f32 = jnp.zeros((d, LANES), F32).at[:, :N_EXPERTS].set(router_w)
    rw_hi = lax.bitcast_convert_type(
        lax.bitcast_convert_type(rw_f32, jnp.uint32) & jnp.uint32(0xFFFF0000), F32)
    rw_pad = jnp.stack([rw_hi.astype(BF16), (rw_f32 - rw_hi).astype(BF16)])
    rb_pad = jnp.full((1, LANES), NEG_INF, F32).at[0, :N_EXPERTS].set(router_b)
    counts = jnp.zeros((1, LANES), F32)
    routed = []
    for x2d, mod_rows, per in parts:
        hp, idx, wgt, rank, counts = _router(
            x2d, shift2[mod_rows][:, None, :], scale2[mod_rows][:, None, :], norm_g[None, :],
            rw_pad, rb_pad, counts, rows_per_mod=per)
        routed.append((hp, idx[:, :TOP_K], wgt, rank[:, :TOP_K]))
    n_total = sum(x2d.shape[0] for x2d, _, _ in parts)
    max_tiles = (n_total * TOP_K) // MOE_TILE + N_EXPERTS
    n_slots = max_tiles * MOE_TILE
    max_items = N_EXPERTS + -(-max_tiles // EXPERT_TILES)
    group_start, fill_tiles, tabs = _schedule(counts[0, :N_EXPERTS].astype(I32), max_tiles, max_items)

    slots = [(group_start[idx] + rank).T for _, idx, _, rank in routed]
    hp_all = jnp.concatenate([r[0] for r in routed], axis=0) if len(routed) > 1 else routed[0][0]
    slots_all = jnp.concatenate(slots, axis=1) if len(slots) > 1 else slots[0]
    xs = _dispatch(fill_tiles, slots_all, hp_all, n_slots)
    y = _expert_ffn(tabs, xs, w1, b1, w2, b2, layer, max_items)
    outs = []
    for (x2d, mod_rows, per), sl, (_, _, wgt, _) in zip(parts, slots, routed):
        outs.append(_combine(sl, y, wgt, x2d, gate2[mod_rows][:, None, :], rows_per_mod=per))
    return outs


def _rope_tables(seq):
    t = jnp.arange(seq, dtype=I32)
    m = HEAD_DIM // 4
    inv_freq = ROPE_THETA ** (-jnp.arange(m, dtype=F32) / m)
    ang_r = (t // GRID_W).astype(F32)[:, None] * inv_freq[None, :]
    ang_c = (t % GRID_W).astype(F32)[:, None] * inv_freq[None, :]
    cos = jnp.concatenate([jnp.cos(ang_r)] * 2 + [jnp.cos(ang_c)] * 2, axis=-1)
    sin = jnp.concatenate([-jnp.sin(ang_r), jnp.sin(ang_r), -jnp.sin(ang_c), jnp.sin(ang_c)], axis=-1)
    return cos, sin


def kernel(x, c, ctx, c_ctx, ada_w, ada_b, norm_mix_g, norm_ffn_g, a_w_qkv, a_w_o, a_q_gain, a_k_gain,
           b_w_qkv, b_w_o, b_q_gain, b_k_gain, b_rel_bias, router_w, router_b, exp_w1, exp_b1, exp_w2, exp_b2):
    bsz, seq, d = x.shape
    l_ctx = ctx.shape[1]
    depth = ada_w.shape[0]
    n_heads = d // HEAD_DIM
    q_scale = HEAD_DIM ** -0.5 * LOG2E

    c_rows = jnp.zeros((SUBLANES, d), F32).at[:bsz].set(c).at[bsz].set(c_ctx)
    mod = _ada_modulation(c_rows, ada_w, ada_b)
    cos, sin = _rope_tables(seq)
    ones_tab = jnp.ones((l_ctx, HEAD_DIM), F32)
    x_rows = jnp.arange(bsz)
    c_rows_idx = jnp.full((bsz,), bsz)

    for i in range(depth):
        last = i == depth - 1
        sh1, sc1, g1, sh2, sc2, g2 = [mod[i, :, k * d:(k + 1) * d] for k in range(N_MOD)]
        j = i // 2
        if i % 2 == 0:
            w_qkv, w_o, qg, kg = a_w_qkv[j], a_w_o[j], a_q_gain[j], a_k_gain[j]
            n_q, n_k, n_v = n_heads, A_KV_HEADS, A_KV_HEADS
        else:
            w_qkv, w_o, qg, kg = b_w_qkv[j], b_w_o[j], b_q_gain[j], b_k_gain[j]
            n_q, n_k, n_v = n_heads, n_heads, n_heads
        heads_per_tile = 512 // HEAD_DIM
        gain_cols = jnp.concatenate([jnp.tile(qg * q_scale, n_q), jnp.tile(kg, n_k),
                                     jnp.ones((n_v * HEAD_DIM,), F32)])[None, :]
        flags = jnp.concatenate([jnp.ones(((n_q + n_k) // heads_per_tile,), I32),
                                 jnp.zeros((n_v // heads_per_tile,), I32)])
        rope = i % 2 == 0
        qkv_x = _qkv_project(x, sh1[x_rows][:, None, :], sc1[x_rows][:, None, :], norm_mix_g[i][None, :],
                             w_qkv, gain_cols, flags, cos, sin, rope=rope, tm=1024)
        qkv_c = _qkv_project(ctx, sh1[c_rows_idx][:, None, :], sc1[c_rows_idx][:, None, :],
                             norm_mix_g[i][None, :], w_qkv, gain_cols, flags, ones_tab, ones_tab,
                             rope=False, tm=l_ctx)
        if i % 2 == 0:
            ox = _attention(qkv_x, qkv_c, qkv_x, n_q_heads=n_q, n_kv_heads=n_k,
                            k_col0=n_q, v_col0=n_q + n_k, tq=256)
        else:
            ox = _neighborhood_attention(qkv_x, qkv_c, b_rel_bias[j], n_heads=n_heads)
        x = _out_project(ox, w_o, x, g1[x_rows][:, None, :], tm=1024)
        parts = [(x.reshape(bsz * seq, d), x_rows, seq)]
        if not last:
            oc = _attention(qkv_c, qkv_c, None, n_q_heads=n_q, n_kv_heads=n_k,
                            k_col0=n_q, v_col0=n_q + n_k, tq=l_ctx)
            ctx = _out_project(oc, w_o, ctx, g1[c_rows_idx][:, None, :], tm=l_ctx)
            parts.append((ctx.reshape(bsz * l_ctx, d), jnp.full((1,), bsz), bsz * l_ctx))
        outs = _moe(parts, sh2, sc2, g2, norm_ffn_g[i], router_w[i], router_b[i],
                    exp_w1, exp_b1, exp_w2, exp_b2, i)
        x = outs[0].reshape(bsz, seq, d)
        if not last:
            ctx = outs[1].reshape(bsz, l_ctx, d)
    return x
```

```python
import functools
import math

import jax
import jax.numpy as jnp
import numpy as np
from jax import lax
from jax.experimental import pallas as pl
from jax.experimental.pallas import tpu as pltpu

F32 = jnp.float32
BF16 = jnp.bfloat16
I32 = jnp.int32

LANES = 128
SUBLANES = 8
VMEM_BYTES_V7X = 64 * 1024 * 1024

HEAD_DIM = 128
GRID_W = 64
A_KV_HEADS = 4
NA_WIN_ROWS = 8
NA_WIN_COLS = 16
N_EXPERTS = 32
TOP_K = 4
ROPE_THETA = 10000.0
SWIGLU_ALPHA = 1.702
SWIGLU_LIMIT = 7.0
NORM_EPS = 1e-6
N_MOD = 6
LOG2E = math.log2(math.e)
NEG_INF = float("-inf")

MOE_TILE = 128


def _vmem_limit(nbytes):
    return int(min(nbytes + 8 * 1024 * 1024, VMEM_BYTES_V7X - 8 * 1024 * 1024))


def _params(semantics, vmem_bytes):
    return pltpu.CompilerParams(dimension_semantics=semantics,
                                vmem_limit_bytes=_vmem_limit(vmem_bytes))


def _ada_kernel(c_ref, w_ref, b_ref, o_ref):
    c = c_ref[...]
    a = (c * jax.nn.sigmoid(c)).astype(BF16)
    o_ref[0] = jnp.dot(a, w_ref[0].astype(BF16), preferred_element_type=F32) + b_ref[0]


def _ada_modulation(c_rows, ada_w, ada_b):
    depth, d, n = ada_w.shape
    rows = c_rows.shape[0]
    tn = 1024
    return pl.pallas_call(
        _ada_kernel,
        grid=(depth, n // tn),
        in_specs=[pl.BlockSpec((rows, d), lambda l, j: (0, 0)),
                  pl.BlockSpec((1, d, tn), lambda l, j: (l, 0, j)),
                  pl.BlockSpec((1, 1, tn), lambda l, j: (l, 0, j))],
        out_specs=pl.BlockSpec((1, rows, tn), lambda l, j: (l, 0, j)),
        out_shape=jax.ShapeDtypeStruct((depth, rows, n), F32),
        compiler_params=_params(("arbitrary", "arbitrary"), 2 * d * tn * 4 + d * tn * 2),
        name="ada_modulation",
    )(c_rows, ada_w, ada_b.reshape(depth, 1, n))


def _norm_modulate(x, g, shift, scale):
    ms = jnp.mean(x * x, axis=-1, keepdims=True)
    y = x * lax.rsqrt(ms + NORM_EPS) * g
    return y * (1.0 + scale) + shift


def _norm_mod_kernel(x_ref, sh_ref, sc_ref, g_ref, h_ref):
    h_ref[...] = _norm_modulate(x_ref[...], g_ref[...], sh_ref[...], sc_ref[...]).astype(BF16)


def _norm_mod(x, shift, scale, g, *, tm):
    b, t, d = x.shape
    return pl.pallas_call(
        _norm_mod_kernel,
        grid=(b, t // tm),
        in_specs=[pl.BlockSpec((None, tm, d), lambda bi, m: (bi, m, 0)),
                  pl.BlockSpec((None, 1, d), lambda bi, m: (bi, 0, 0)),
                  pl.BlockSpec((None, 1, d), lambda bi, m: (bi, 0, 0)),
                  pl.BlockSpec((1, d), lambda bi, m: (0, 0))],
        out_specs=pl.BlockSpec((None, tm, d), lambda bi, m: (bi, m, 0)),
        out_shape=jax.ShapeDtypeStruct((b, t, d), BF16),
        compiler_params=_params(("arbitrary", "arbitrary"), 2 * tm * d * 6 + 4 * tm * d * 4),
        name="norm_mod",
    )(x, shift, scale, g)


def _qkv_kernel(flag_ref, h_ref, w_ref, gain_ref, cos_ref, sin_ref, o_ref, wb_ref, *, rope):
    n = pl.program_id(0)

    @pl.when((pl.program_id(1) == 0) & (pl.program_id(2) == 0))
    def _():
        wb_ref[...] = w_ref[...].astype(BF16)

    tm, tn = o_ref.shape
    piece = min(tm, 256)
    pieces = [slice(r, r + piece) for r in range(0, tm, piece)]

    @pl.when(flag_ref[n] == 0)
    def _():
        for rs in pieces:
            o_ref[rs, :] = jnp.dot(h_ref[rs, :], wb_ref[...], preferred_element_type=F32).astype(BF16)

    @pl.when(flag_ref[n] == 1)
    def _():
        lane = lax.broadcasted_iota(I32, (1, HEAD_DIM), 1)
        first_half = (lane % (HEAD_DIM // 2)) < (HEAD_DIM // 4)
        for rs in pieces:
            a = jnp.dot(h_ref[rs, :], wb_ref[...], preferred_element_type=F32)
            for hh in range(tn // HEAD_DIM):
                sl = slice(hh * HEAD_DIM, (hh + 1) * HEAD_DIM)
                ah = a[:, sl]
                ms = jnp.mean(ah * ah, axis=-1, keepdims=True)
                yh = ah * lax.rsqrt(ms + NORM_EPS) * gain_ref[:, sl]
                if rope:
                    partner = jnp.where(first_half,
                                        pltpu.roll(yh, HEAD_DIM - HEAD_DIM // 4, 1),
                                        pltpu.roll(yh, HEAD_DIM // 4, 1))
                    yh = yh * cos_ref[rs, :] + partner * sin_ref[rs, :]
                o_ref[rs, sl] = yh.astype(BF16)


def _qkv_project(x, shift, scale, g, w, gain_cols, norm_flags, cos, sin, *, rope, tm):
    b, t, d = x.shape
    n = w.shape[1]
    tn = 512
    h = _norm_mod(x, shift, scale, g, tm=min(tm, 512))
    kernel = functools.partial(_qkv_kernel, rope=rope)
    grid_spec = pltpu.PrefetchScalarGridSpec(
        num_scalar_prefetch=1,
        grid=(n // tn, b, t // tm),
        in_specs=[pl.BlockSpec((None, tm, d), lambda j, bi, m, f: (bi, m, 0)),
                  pl.BlockSpec((d, tn), lambda j, bi, m, f: (0, j)),
                  pl.BlockSpec((1, tn), lambda j, bi, m, f: (0, j)),
                  pl.BlockSpec((tm, HEAD_DIM), lambda j, bi, m, f: (m, 0)),
                  pl.BlockSpec((tm, HEAD_DIM), lambda j, bi, m, f: (m, 0))],
        out_specs=pl.BlockSpec((None, tm, tn), lambda j, bi, m, f: (bi, m, j)),
        scratch_shapes=[pltpu.VMEM((d, tn), BF16)])
    vmem = 2 * tm * d * 2 + 2 * d * tn * 4 + d * tn * 2 + 6 * tm * tn * 4
    return pl.pallas_call(
        kernel, grid_spec=grid_spec,
        out_shape=jax.ShapeDtypeStruct((b, t, n), BF16),
        compiler_params=_params(("arbitrary", "arbitrary", "arbitrary"), vmem),
        name="qkv_project",
    )(norm_flags, h, w, gain_cols, cos, sin)


def _nt_dot(a, b):
    return lax.dot_general(a, b, (((1,), (1,)), ((), ())), preferred_element_type=F32)


def _values_with_ones(v_ref, dst_ref):
    dst_ref[:, :HEAD_DIM] = v_ref[...]
    dst_ref[:, HEAD_DIM:] = jnp.ones((v_ref.shape[0], HEAD_DIM), BF16)


def _attn_kernel(q_ref, kc_ref, vc_ref, *rest, group, has_x):
    if has_x:
        kx_ref, vx_ref, o_ref = rest
    else:
        (o_ref,) = rest
    kc = kc_ref[...]
    vc = vc_ref[...]
    for g in range(group):
        sl = slice(g * HEAD_DIM, (g + 1) * HEAD_DIM)
        q = q_ref[:, sl]
        sc = _nt_dot(q, kc)
        m = jnp.max(sc, axis=-1, keepdims=True)
        if has_x:
            sx = _nt_dot(q, kx_ref[...])
            m = jnp.maximum(m, jnp.max(sx, axis=-1, keepdims=True))
        pc = jnp.exp2(sc - m)
        l = jnp.sum(pc, axis=-1, keepdims=True)
        acc = jnp.dot(pc.astype(BF16), vc, preferred_element_type=F32)
        if has_x:
            px = jnp.exp2(sx - m)
            l = l + jnp.sum(px, axis=-1, keepdims=True)
            acc = acc + jnp.dot(px.astype(BF16), vx_ref[...], preferred_element_type=F32)
        o_ref[:, sl] = (acc / l).astype(BF16)


def _attention(q_src, ctx_src, x_src, *, n_q_heads, n_kv_heads, k_col0, v_col0, tq):
    b, t_q, _ = q_src.shape
    l = ctx_src.shape[1]
    group = n_q_heads // n_kv_heads
    has_x = x_src is not None
    gw = group * HEAD_DIM
    in_specs = [pl.BlockSpec((None, tq, gw), lambda bi, kv, i: (bi, i, kv)),
                pl.BlockSpec((None, l, HEAD_DIM), lambda bi, kv, i: (bi, 0, k_col0 + kv)),
                pl.BlockSpec((None, l, HEAD_DIM), lambda bi, kv, i: (bi, 0, v_col0 + kv))]
    args = [q_src, ctx_src, ctx_src]
    s = 0
    if has_x:
        s = x_src.shape[1]
        in_specs += [pl.BlockSpec((None, s, HEAD_DIM), lambda bi, kv, i: (bi, 0, k_col0 + kv)),
                     pl.BlockSpec((None, s, HEAD_DIM), lambda bi, kv, i: (bi, 0, v_col0 + kv))]
        args += [x_src, x_src]
    vmem = 4 * (l + s) * HEAD_DIM * 2 + 4 * tq * gw * 2 + 4 * tq * (l + s) * 4
    return pl.pallas_call(
        functools.partial(_attn_kernel, group=group, has_x=has_x),
        grid=(b, n_kv_heads, t_q // tq),
        in_specs=in_specs,
        out_specs=pl.BlockSpec((None, tq, gw), lambda bi, kv, i: (bi, i, kv)),
        out_shape=jax.ShapeDtypeStruct((b, t_q, n_q_heads * HEAD_DIM), BF16),
        compiler_params=_params(("arbitrary", "arbitrary", "arbitrary"), vmem),
        name="attention_x" if has_x else "attention_ctx",
    )(*args)


NA_Q_ROWS = 8
NA_BAND_ROWS = 2 * NA_WIN_ROWS
NA_DR = 2 * NA_WIN_ROWS - 1
NA_DC = 2 * NA_WIN_COLS - 1
NA_PAIR_TILES = 30


def _na_kernel(tab_ref, q_ref, k_ref, v_ref, kc_ref, vc_ref, o_ref, tile_ref, pair_ref, s_ref,
               v1_ref, vc1_ref, *, rows):
    h = pl.program_id(1)
    a = pl.program_id(2)
    nb = NA_BAND_ROWS * GRID_W

    @pl.when(a == 0)
    def _():
        _values_with_ones(v_ref, v1_ref)
        _values_with_ones(vc_ref, vc1_ref)
        qc = lax.broadcasted_iota(I32, (GRID_W, LANES), 0)
        kc = lax.broadcasted_iota(I32, (GRID_W, LANES), 1) % GRID_W
        dc = jnp.clip(kc - qc, -(NA_WIN_COLS - 1), NA_WIN_COLS - 1) + NA_WIN_COLS - 1
        c0 = jnp.clip(qc - NA_WIN_COLS // 2, 0, GRID_W - NA_WIN_COLS)
        col_ok = (kc >= c0) & (kc < c0 + NA_WIN_COLS)
        for dr in range(NA_DR):
            acc = jnp.zeros((GRID_W, LANES), F32)
            for d in range(NA_DC):
                acc = jnp.where(dc == d, tab_ref[h * (NA_DR * NA_DC) + dr * NA_DC + d] * LOG2E, acc)
            tile_ref[dr] = jnp.where(col_ok, acc, NEG_INF)
        left = lax.broadcasted_iota(I32, (GRID_W, LANES), 1) < GRID_W
        zero = jnp.zeros((GRID_W, LANES), F32)
        for p in range(NA_PAIR_TILES):
            lo = tile_ref[p - NA_WIN_ROWS] if 0 <= p - NA_WIN_ROWS < NA_DR else zero
            hi = tile_ref[p - NA_WIN_ROWS + 1] if 0 <= p - NA_WIN_ROWS + 1 < NA_DR else zero
            pair_ref[p] = jnp.where(left, lo, hi)

    wr = NA_WIN_ROWS
    kb0 = jnp.clip(a * NA_Q_ROWS - wr // 2, 0, rows - NA_BAND_ROWS)
    k_band = k_ref[pl.ds(pl.multiple_of(kb0 * GRID_W, 4 * GRID_W), nb), :]
    v_band = v1_ref[pl.ds(pl.multiple_of(kb0 * GRID_W, 4 * GRID_W), nb), :]
    q = q_ref[...]
    s_loc = _nt_dot(q, k_band)
    s_ref[:, nb:] = _nt_dot(q, kc_ref[...])

    band_row = lax.broadcasted_iota(I32, (1, nb), 1) // GRID_W
    for rq in range(NA_Q_ROWS):
        r = a * NA_Q_ROWS + rq
        r0 = jnp.clip(r - wr // 2, 0, rows - wr)
        lo = r0 - kb0
        row_mask = jnp.where((band_row >= lo) & (band_row < lo + wr), 0.0, NEG_INF)
        j0 = kb0 - r + 2 * wr - 1
        bias = jnp.concatenate([pair_ref[j0 + 2 * i] for i in range(NA_BAND_ROWS // 2)], axis=1)
        rs = slice(rq * GRID_W, (rq + 1) * GRID_W)
        s_ref[rs, :nb] = s_loc[rs, :] + bias + row_mask

    s = s_ref[...]
    m = jnp.max(s, axis=-1, keepdims=True)
    pb = jnp.exp2(s - m).astype(BF16)
    acc = (jnp.dot(pb[:, :nb], v_band, preferred_element_type=F32)
           + jnp.dot(pb[:, nb:], vc1_ref[...], preferred_element_type=F32))
    o_ref[...] = (acc[:, :HEAD_DIM] / acc[:, HEAD_DIM:]).astype(BF16)


def _neighborhood_attention(qkv_x, qkv_c, rel_bias, *, n_heads):
    b, s, _ = qkv_x.shape
    l = qkv_c.shape[1]
    rows = s // GRID_W
    nq = NA_Q_ROWS * GRID_W
    nb = NA_BAND_ROWS * GRID_W
    grid_spec = pltpu.PrefetchScalarGridSpec(
        num_scalar_prefetch=0,
        grid=(b, n_heads, rows // NA_Q_ROWS),
        in_specs=[pl.BlockSpec(memory_space=pltpu.SMEM),
                  pl.BlockSpec((None, nq, HEAD_DIM), lambda bi, h, a: (bi, a, h)),
                  pl.BlockSpec((None, s, HEAD_DIM), lambda bi, h, a: (bi, 0, n_heads + h)),
                  pl.BlockSpec((None, s, HEAD_DIM), lambda bi, h, a: (bi, 0, 2 * n_heads + h)),
                  pl.BlockSpec((None, l, HEAD_DIM), lambda bi, h, a: (bi, 0, n_heads + h)),
                  pl.BlockSpec((None, l, HEAD_DIM), lambda bi, h, a: (bi, 0, 2 * n_heads + h))],
        out_specs=pl.BlockSpec((None, nq, HEAD_DIM), lambda bi, h, a: (bi, a, h)),
        scratch_shapes=[pltpu.VMEM((NA_DR, GRID_W, LANES), F32),
                        pltpu.VMEM((NA_PAIR_TILES, GRID_W, LANES), F32),
                        pltpu.VMEM((nq, nb + l), F32),
                        pltpu.VMEM((s, 2 * HEAD_DIM), BF16),
                        pltpu.VMEM((l, 2 * HEAD_DIM), BF16)])
    vmem = 6 * (s + l) * HEAD_DIM * 2 + 4 * nq * (nb + l) * 4 + 2 * 1024 * 1024
    return pl.pallas_call(
        functools.partial(_na_kernel, rows=rows), grid_spec=grid_spec,
        out_shape=jax.ShapeDtypeStruct((b, s, n_heads * HEAD_DIM), BF16),
        compiler_params=_params(("arbitrary", "arbitrary", "arbitrary"), vmem),
        name="neighborhood_attention",
    )(rel_bias.reshape(-1), qkv_x, qkv_x, qkv_x, qkv_c, qkv_c)


def _oproj_kernel(o_ref, w_ref, x_ref, g_ref, out_ref, wb_ref):
    @pl.when((pl.program_id(1) == 0) & (pl.program_id(2) == 0))
    def _():
        wb_ref[...] = w_ref[...].astype(BF16)

    y = jnp.dot(o_ref[...], wb_ref[...], preferred_element_type=F32)
    out_ref[...] = x_ref[...] + g_ref[...] * y


def _out_project(o, w, x, gate, *, tm):
    b, t, k = o.shape
    d = w.shape[1]
    tn = 512
    vmem = 2 * tm * k * 2 + 2 * k * tn * 4 + k * tn * 2 + 6 * tm * tn * 4
    return pl.pallas_call(
        _oproj_kernel,
        grid=(d // tn, b, t // tm),
        in_specs=[pl.BlockSpec((None, tm, k), lambda j, bi, m: (bi, m, 0)),
                  pl.BlockSpec((k, tn), lambda j, bi, m: (0, j)),
                  pl.BlockSpec((None, tm, tn), lambda j, bi, m: (bi, m, j)),
                  pl.BlockSpec((None, 1, tn), lambda j, bi, m: (bi, 0, j))],
        out_specs=pl.BlockSpec((None, tm, tn), lambda j, bi, m: (bi, m, j)),
        scratch_shapes=[pltpu.VMEM((k, tn), BF16)],
        out_shape=jax.ShapeDtypeStruct((b, t, d), F32),
        compiler_params=_params(("arbitrary", "arbitrary", "arbitrary"), vmem),
        name="out_project",
    )(o, w, x, gate)


ROUTER_BLOCK = 256


def _router_kernel(x_ref, sh_ref, sc_ref, g_ref, rw_ref, rb_ref, cnt_in_ref,
                   hp_ref, idx_ref, wgt_ref, rank_ref, cnt_ref, run_ref):
    i = pl.program_id(0)

    @pl.when(i == 0)
    def _():
        run_ref[...] = cnt_in_ref[...]

    h = _norm_modulate(x_ref[...], g_ref[...], sh_ref[...], sc_ref[...])
    tb, d = h.shape
    packed = pltpu.pack_elementwise([h[:, :d // 2], h[:, d // 2:]], packed_dtype=BF16)
    for k in range(SUBLANES):
        hp_ref[pl.ds(k, tb, stride=SUBLANES), :] = packed[:, k * LANES:(k + 1) * LANES]

    h_hi = h.astype(BF16)
    h_lo = (h - h_hi.astype(F32)).astype(BF16)
    logits = (jnp.dot(h_hi, rw_ref[0], preferred_element_type=F32)
              + (jnp.dot(h_hi, rw_ref[1], preferred_element_type=F32)
                 + jnp.dot(h_lo, rw_ref[0], preferred_element_type=F32))
              + rb_ref[...])
    lane = lax.broadcasted_iota(I32, (tb, LANES), 1)
    lane_f = lane.astype(F32)
    work = logits
    idx_acc = jnp.zeros((tb, LANES), F32)
    val_acc = jnp.full((tb, LANES), NEG_INF, F32)
    hots = []
    for k in range(TOP_K):
        m = jnp.max(work, axis=-1, keepdims=True)
        idx = jnp.min(jnp.where(work == m, lane_f, float(LANES)), axis=-1, keepdims=True)
        hot = lane_f == idx
        hots.append(hot)
        idx_acc = jnp.where(lane == k, idx, idx_acc)
        val_acc = jnp.where(lane == k, m, val_acc)
        work = jnp.where(hot, NEG_INF, work)

    e = jnp.exp(val_acc - jnp.max(val_acc, axis=-1, keepdims=True))
    wgt_ref[...] = e / jnp.sum(e, axis=-1, keepdims=True)
    idx_ref[...] = idx_acc.astype(I32)

    chosen = jnp.zeros((tb, LANES), F32)
    for hot in hots:
        chosen = chosen + jnp.where(hot, 1.0, 0.0)
    row = lax.broadcasted_iota(I32, (tb, tb), 0)
    col = lax.broadcasted_iota(I32, (tb, tb), 1)
    earlier = jnp.where(col < row, 1.0, 0.0).astype(BF16)
    before = jnp.dot(earlier, chosen.astype(BF16), preferred_element_type=F32) + run_ref[...]
    rank_acc = jnp.zeros((tb, LANES), F32)
    for k, hot in enumerate(hots):
        rk = jnp.sum(jnp.where(hot, before, 0.0), axis=-1, keepdims=True)
        rank_acc = jnp.where(lane == k, rk, rank_acc)
    rank_ref[...] = rank_acc.astype(I32)
    run_ref[...] = run_ref[...] + jnp.sum(chosen, axis=0, keepdims=True)
    cnt_ref[...] = run_ref[...]


def _router(x2d, shift, scale, g, rw_pad, rb_pad, counts_in, *, rows_per_mod):
    n, d = x2d.shape
    tb = ROUTER_BLOCK
    per = rows_per_mod // tb
    vmem = 2 * tb * d * 4 + 6 * tb * d * 4 + 2 * d * LANES * 4
    outs = pl.pallas_call(
        _router_kernel,
        grid=(n // tb,),
        in_specs=[pl.BlockSpec((tb, d), lambda i: (i, 0)),
                  pl.BlockSpec((None, 1, d), lambda i: (i // per, 0, 0)),
                  pl.BlockSpec((None, 1, d), lambda i: (i // per, 0, 0)),
                  pl.BlockSpec((1, d), lambda i: (0, 0)),
                  pl.BlockSpec((2, d, LANES), lambda i: (0, 0, 0)),
                  pl.BlockSpec((1, LANES), lambda i: (0, 0)),
                  pl.BlockSpec((1, LANES), lambda i: (0, 0))],
        out_specs=[pl.BlockSpec((tb * SUBLANES, LANES), lambda i: (i, 0)),
                   pl.BlockSpec((tb, LANES), lambda i: (i, 0)),
                   pl.BlockSpec((tb, LANES), lambda i: (i, 0)),
                   pl.BlockSpec((tb, LANES), lambda i: (i, 0)),
                   pl.BlockSpec((1, LANES), lambda i: (0, 0))],
        out_shape=[jax.ShapeDtypeStruct((n * SUBLANES, LANES), jnp.uint32),
                   jax.ShapeDtypeStruct((n, LANES), I32),
                   jax.ShapeDtypeStruct((n, LANES), F32),
                   jax.ShapeDtypeStruct((n, LANES), I32),
                   jax.ShapeDtypeStruct((1, LANES), F32)],
        scratch_shapes=[pltpu.VMEM((1, LANES), F32)],
        compiler_params=_params(("arbitrary",), vmem),
        name="router",
    )(x2d, shift, scale, g, rw_pad, rb_pad, counts_in)
    return outs


def _dispatch_kernel(fill_ref, slot_ref, hp_ref, xs_ref, zero_ref, sem, fill_sem):
    tb = hp_ref.shape[0] // SUBLANES
    t_rows = zero_ref.shape[0]

    @pl.when(pl.program_id(0) == 0)
    def _():
        zero_ref[...] = jnp.zeros_like(zero_ref)

        def tile_copy(t):
            return pltpu.make_async_copy(zero_ref, xs_ref.at[pl.ds(t * t_rows, t_rows), :], fill_sem)

        def fill(t, carry):
            @pl.when(fill_ref[t] == 1)
            def _():
                tile_copy(t).start()
            return carry

        def fill_wait(t, carry):
            @pl.when(fill_ref[t] == 1)
            def _():
                tile_copy(t).wait()
            return carry

        lax.fori_loop(0, fill_ref.shape[0], fill, 0)
        lax.fori_loop(0, fill_ref.shape[0], fill_wait, 0)

    def row_copy(j, s):
        src = j * SUBLANES if isinstance(j, int) else pl.multiple_of(j * SUBLANES, SUBLANES)
        dst = s * SUBLANES if isinstance(s, int) else pl.multiple_of(s * SUBLANES, SUBLANES)
        return pltpu.make_async_copy(hp_ref.at[pl.ds(src, SUBLANES), :],
                                     xs_ref.at[pl.ds(dst, SUBLANES), :], sem)

    def issue(j, carry):
        for k in range(TOP_K):
            row_copy(j, slot_ref[k, j]).start()
        return carry

    lax.fori_loop(0, tb, issue, 0)

    for k in range(TOP_K):
        pltpu.make_async_copy(hp_ref, xs_ref.at[pl.ds(0, tb * SUBLANES), :], sem).wait()


def _dispatch(fill_tiles, slots_t, hp, n_slots):
    n = hp.shape[0] // SUBLANES
    tb = ROUTER_BLOCK
    grid_spec = pltpu.PrefetchScalarGridSpec(
        num_scalar_prefetch=1,
        grid=(n // tb,),
        in_specs=[pl.BlockSpec((TOP_K, tb), lambda i, f: (0, i), memory_space=pltpu.SMEM),
                  pl.BlockSpec((tb * SUBLANES, LANES), lambda i, f: (i, 0))],
        out_specs=pl.BlockSpec(memory_space=pl.ANY),
        scratch_shapes=[pltpu.VMEM((MOE_TILE * SUBLANES, LANES), jnp.uint32),
                        pltpu.SemaphoreType.DMA(()), pltpu.SemaphoreType.DMA(())])
    return pl.pallas_call(
        _dispatch_kernel, grid_spec=grid_spec,
        out_shape=jax.ShapeDtypeStruct((n_slots * SUBLANES, LANES), jnp.uint32),
        compiler_params=_params(("arbitrary",), (2 * tb + MOE_TILE) * SUBLANES * LANES * 4),
        name="dispatch",
    )(fill_tiles, slots_t, hp)


EXPERT_TILES = 10
HIDDEN_CHUNK = 512


def _swiglu_pairs(a):
    even = (lax.broadcasted_iota(I32, (1, LANES), 1) % 2) == 0
    outs = []
    for c in range(a.shape[1] // (2 * LANES)):
        ga = a[:, (2 * c) * LANES:(2 * c + 1) * LANES]
        gb = a[:, (2 * c + 1) * LANES:(2 * c + 2) * LANES]
        glu = jnp.where(even, ga, pltpu.roll(gb, 1, 1))
        lin = jnp.where(even, pltpu.roll(ga, LANES - 1, 1), gb)
        glu = jnp.minimum(glu, SWIGLU_LIMIT)
        lin = jnp.clip(lin, -SWIGLU_LIMIT, SWIGLU_LIMIT)
        outs.append((glu * jax.nn.sigmoid(SWIGLU_ALPHA * glu) * (lin + 1.0)).astype(BF16))
    return outs[0] if len(outs) == 1 else jnp.concatenate(outs, axis=1)


def _interleaved_rows_bf16(w_ref, dst_ref):
    half = LANES // 2
    for qd in range(w_ref.shape[0] // LANES):
        wa = w_ref[qd * LANES:qd * LANES + half, :]
        wc = w_ref[qd * LANES + half:(qd + 1) * LANES, :]
        packed = pltpu.pack_elementwise([wa, wc], packed_dtype=BF16)
        dst_ref[qd * LANES:(qd + 1) * LANES, :] = pltpu.bitcast(packed, BF16)


def _expert_ffn_kernel(ie_ref, row0_ref, nt_ref, zt_ref, xs_ref, w1_ref, b1_ref, w2_ref, b2_ref, y_ref,
                       xb_ref, stage_ref, ybuf_ref, w1b_ref, w2b_ref, pend_ref,
                       sem_in, sem_out, *, n_chunks, n_items):
    del ie_ref
    i = pl.program_id(0)
    c = pl.program_id(1)
    t_rows = MOE_TILE
    nt = nt_ref[i]
    row0 = row0_ref[i]

    def out_copy(src_ref, r_src, r_dst):
        dst = r_dst if isinstance(r_dst, int) else pl.multiple_of(r_dst, t_rows)
        return pltpu.make_async_copy(src_ref.at[pl.ds(r_src, t_rows), :],
                                     y_ref.at[pl.ds(dst, t_rows), :], sem_out)

    def wait_pending():
        def body(_, carry):
            out_copy(ybuf_ref, 0, 0).wait()
            return carry
        lax.fori_loop(0, pend_ref[0], body, 0)
        pend_ref[0] = 0

    @pl.when((i == 0) & (c == 0))
    def _():
        pend_ref[0] = 0

    @pl.when(c == 0)
    def _():
        wait_pending()

        def in_copy(t, slot):
            src = pl.multiple_of((row0 + t * t_rows) * SUBLANES, t_rows * SUBLANES)
            return pltpu.make_async_copy(xs_ref.at[pl.ds(src, t_rows * SUBLANES), :],
                                         stage_ref.at[slot], sem_in.at[slot])

        @pl.when(nt > 0)
        def _():
            in_copy(0, 0).start()

        def load(t, carry):
            slot = t % 2
            in_copy(t, slot).wait()

            @pl.when(t + 1 < nt)
            def _():
                in_copy(t + 1, 1 - slot).start()

            r = pl.multiple_of(t * t_rows, t_rows)
            half = xb_ref.shape[1] // 2
            for k in range(SUBLANES):
                xw = stage_ref[slot, pl.ds(k, t_rows, stride=SUBLANES), :]
                lo = pltpu.unpack_elementwise(xw, index=0, packed_dtype=BF16, unpacked_dtype=F32)
                hi = pltpu.unpack_elementwise(xw, index=1, packed_dtype=BF16, unpacked_dtype=F32)
                xb_ref[pl.ds(r, t_rows), k * LANES:(k + 1) * LANES] = lo.astype(BF16)
                xb_ref[pl.ds(r, t_rows), half + k * LANES:half + (k + 1) * LANES] = hi.astype(BF16)
            ybuf_ref[pl.ds(r, t_rows), :] = jnp.broadcast_to(b2_ref[...], (t_rows, ybuf_ref.shape[1]))
            return carry

        lax.fori_loop(0, nt, load, 0)

        def zfill(t, carry):
            out_copy(ybuf_ref, 0, row0 + t * t_rows).start()
            return carry

        @pl.when(zt_ref[i] > 0)
        def _():
            ybuf_ref[0:t_rows, :] = jnp.zeros((t_rows, ybuf_ref.shape[1]), F32)

        lax.fori_loop(0, zt_ref[i], zfill, 0)
        pend_ref[0] = zt_ref[i]

    @pl.when(nt > 0)
    def _():
        w1b_ref[...] = w1_ref[...].astype(BF16)
        _interleaved_rows_bf16(w2_ref, w2b_ref)
        last = c == n_chunks - 1

        def rows_step(r, n_tiles):
            n_rows = n_tiles * t_rows
            xb = xb_ref[pl.ds(r, n_rows), :]
            a = jnp.dot(xb, w1b_ref[...], preferred_element_type=F32) + b1_ref[...]
            act = _swiglu_pairs(a)
            ybuf_ref[pl.ds(r, n_rows), :] += jnp.dot(act, w2b_ref[...], preferred_element_type=F32)

            @pl.when(last)
            def _():
                for k in range(n_tiles):
                    out_copy(ybuf_ref, r + k * t_rows, row0 + r + k * t_rows).start()

        def quad(p, carry):
            rows_step(pl.multiple_of(p * (4 * t_rows), 4 * t_rows), 4)
            return carry

        lax.fori_loop(0, nt // 4, quad, 0)

        @pl.when(nt % 4 >= 2)
        def _():
            rows_step(pl.multiple_of((nt // 4) * (4 * t_rows), 2 * t_rows), 2)

        @pl.when(nt % 2 == 1)
        def _():
            rows_step(pl.multiple_of((nt - 1) * t_rows, t_rows), 1)

        @pl.when(last)
        def _():
            pend_ref[0] = nt

    @pl.when((i == n_items - 1) & (c == n_chunks - 1))
    def _():
        wait_pending()


def _expert_ffn(item_tabs, xs, w1, b1, w2, b2, layer, max_items):
    p = xs.shape[0] // SUBLANES
    half = w1.shape[2] // 2
    _, e, d, n = w1.shape
    hc = HIDDEN_CHUNK
    n_chunks = (n // 2) // hc
    rows = EXPERT_TILES * MOE_TILE

    def chunk(c, nt, i):
        return jnp.where(nt[i] > 0, c, n_chunks - 1)

    grid_spec = pltpu.PrefetchScalarGridSpec(
        num_scalar_prefetch=4,
        grid=(max_items, n_chunks),
        in_specs=[pl.BlockSpec(memory_space=pl.ANY),
                  pl.BlockSpec((None, None, d, 2 * hc), lambda i, c, ie, r0, nt, zt: (layer, ie[i], 0, chunk(c, nt, i))),
                  pl.BlockSpec((None, None, 1, 2 * hc), lambda i, c, ie, r0, nt, zt: (layer, ie[i], 0, chunk(c, nt, i))),
                  pl.BlockSpec((None, None, hc, d), lambda i, c, ie, r0, nt, zt: (layer, ie[i], chunk(c, nt, i), 0)),
                  pl.BlockSpec((None, None, 1, d), lambda i, c, ie, r0, nt, zt: (layer, ie[i], 0, 0))],
        out_specs=pl.BlockSpec(memory_space=pl.ANY),
        scratch_shapes=[pltpu.VMEM((rows, d), BF16),
                        pltpu.VMEM((2, MOE_TILE * SUBLANES, LANES), jnp.uint32),
                        pltpu.VMEM((rows, d), F32),
                        pltpu.VMEM((d, 2 * hc), BF16),
                        pltpu.VMEM((hc, d), BF16),
                        pltpu.SMEM((1,), I32),
                        pltpu.SemaphoreType.DMA((2,)),
                        pltpu.SemaphoreType.DMA(())])
    vmem = (rows * d * 6 + 2 * MOE_TILE * half * 4 + MOE_TILE * d * 4
            + 3 * d * 2 * hc * 4 + d * 2 * hc * 2 + 3 * hc * d * 4 + hc * d * 2
            + 8 * MOE_TILE * 2 * hc * 4 + 4 * MOE_TILE * d * 4)
    depth = w1.shape[0]
    return pl.pallas_call(
        functools.partial(_expert_ffn_kernel, n_chunks=n_chunks, n_items=max_items), grid_spec=grid_spec,
        out_shape=jax.ShapeDtypeStruct((p, d), F32),
        compiler_params=_params(("arbitrary", "arbitrary"), vmem),
        name="expert_ffn",
    )(*item_tabs, xs, w1, b1.reshape(depth, e, 1, n), w2, b2.reshape(depth, e, 1, d))


COMBINE_BLOCK = 128


def _combine_kernel(slot_ref, slot_next_ref, y_ref, w_ref, x_ref, g_ref, o_ref, buf, sem, *, nblk):
    i = pl.program_id(0)
    tb = x_ref.shape[0]

    def row_copy(s, b, k, j):
        return pltpu.make_async_copy(y_ref.at[pl.ds(s, 1), :], buf.at[b, k, pl.ds(j, 1), :], sem.at[b])

    def issue(slots, b):
        def body(j8, carry):
            for jj in range(SUBLANES):
                j = j8 * SUBLANES + jj
                for k in range(TOP_K):
                    row_copy(slots[k, j], b, k, j).start()
            return carry
        lax.fori_loop(0, tb // SUBLANES, body, 0)

    @pl.when(i == 0)
    def _():
        issue(slot_ref, 0)

    @pl.when(i + 1 < nblk)
    def _():
        issue(slot_next_ref, (i + 1) % 2)

    cur = i % 2
    for k in range(TOP_K):
        pltpu.make_async_copy(y_ref.at[pl.ds(0, tb), :], buf.at[cur, k], sem.at[cur]).wait()

    w = w_ref[...]
    acc = w[:, 0:1] * buf[cur, 0]
    for k in range(1, TOP_K):
        acc = acc + w[:, k:k + 1] * buf[cur, k]
    o_ref[...] = x_ref[...] + g_ref[...] * acc


def _combine(slots_t, y, gates, x2d, gate_mod, *, rows_per_mod):
    n, d = x2d.shape
    tb = COMBINE_BLOCK
    nblk = n // tb
    per = rows_per_mod // tb
    vmem = 2 * TOP_K * tb * d * 4 + 6 * tb * d * 4
    return pl.pallas_call(
        functools.partial(_combine_kernel, nblk=nblk),
        grid=(nblk,),
        in_specs=[pl.BlockSpec((TOP_K, tb), lambda i: (0, i), memory_space=pltpu.SMEM),
                  pl.BlockSpec((TOP_K, tb), lambda i: (0, jnp.minimum(i + 1, nblk - 1)),
                               memory_space=pltpu.SMEM),
                  pl.BlockSpec(memory_space=pl.ANY),
                  pl.BlockSpec((tb, LANES), lambda i: (i, 0)),
                  pl.BlockSpec((tb, d), lambda i: (i, 0)),
                  pl.BlockSpec((None, 1, d), lambda i: (i // per, 0, 0))],
        out_specs=pl.BlockSpec((tb, d), lambda i: (i, 0)),
        out_shape=jax.ShapeDtypeStruct((n, d), F32),
        scratch_shapes=[pltpu.VMEM((2, TOP_K, tb, d), F32), pltpu.SemaphoreType.DMA((2,))],
        compiler_params=_params(("arbitrary",), vmem),
        name="combine",
    )(slots_t, slots_t, y, gates, x2d, gate_mod)


def _owner(ends, idx):
    return jnp.minimum(jnp.sum((ends[None, :] <= idx[:, None]).astype(I32), axis=1), ends.shape[0] - 1)


def _schedule(counts, max_tiles, max_items):
    t_rows = MOE_TILE
    tiles_e = (counts + t_rows - 1) // t_rows
    tile_end = jnp.cumsum(tiles_e)
    tile_start = tile_end - tiles_e
    num_tiles = tile_end[-1]
    t = jnp.arange(max_tiles, dtype=I32)
    te = _owner(tile_end, jnp.minimum(t, num_tiles - 1))
    fill = jnp.where((t < num_tiles) & (t != tile_end[te] - 1), 0, 1)

    items_e = (tiles_e + EXPERT_TILES - 1) // EXPERT_TILES
    item_end = jnp.cumsum(items_e)
    item_start = item_end - items_e
    n_items = item_end[-1]
    i = jnp.arange(max_items, dtype=I32)
    live = i < n_items
    ie = _owner(item_end, jnp.minimum(i, n_items - 1))
    local = jnp.minimum(i, n_items - 1) - item_start[ie]
    nt = jnp.where(live, jnp.clip(tiles_e[ie] - local * EXPERT_TILES, 0, EXPERT_TILES), 0)
    dead_tile0 = num_tiles + (i - n_items) * EXPERT_TILES
    zt = jnp.where(live, 0, jnp.clip(max_tiles - dead_tile0, 0, EXPERT_TILES))
    row0 = jnp.where(live, (tile_start[ie] + local * EXPERT_TILES) * t_rows,
                     jnp.clip(dead_tile0, 0, max_tiles - 1) * t_rows)
    return ((tile_start * t_rows).astype(I32), fill.astype(I32),
            (ie.astype(I32), row0.astype(I32), nt.astype(I32), zt.astype(I32)))


def _moe(parts, shift2, scale2, gate2, norm_g, router_w, router_b, w1, b1, w2, b2, layer):
    d = router_w.shape[0]
    rw_f32 = jnp.zeros((d, LANES), F32).at[:, :N_EXPERTS].set(router_w)
    rw_hi = lax.bitcast_convert_type(
        lax.bitcast_convert_type(rw_f32, jnp.uint32) & jnp.uint32(0xFFFF0000), F32)
    rw_pad = jnp.stack([rw_hi.astype(BF16), (rw_f32 - rw_hi).astype(BF16)])
    rb_pad = jnp.full((1, LANES), NEG_INF, F32).at[0, :N_EXPERTS].set(router_b)
    counts = jnp.zeros((1, LANES), F32)
    routed = []
    for x2d, mod_rows, per in parts:
        hp, idx, wgt, rank, counts = _router(
            x2d, shift2[mod_rows][:, None, :], scale2[mod_rows][:, None, :], norm_g[None, :],
            rw_pad, rb_pad, counts, rows_per_mod=per)
        routed.append((hp, idx[:, :TOP_K], wgt, rank[:, :TOP_K]))
    n_total = sum(x2d.shape[0] for x2d, _, _ in parts)
    max_tiles = (n_total * TOP_K) // MOE_TILE + N_EXPERTS
    n_slots = max_tiles * MOE_TILE
    max_items = N_EXPERTS + -(-max_tiles // EXPERT_TILES)
    group_start, fill_tiles, tabs = _schedule(counts[0, :N_EXPERTS].astype(I32), max_tiles, max_items)

    slots = [(group_start[idx] + rank).T for _, idx, _, rank in routed]
    hp_all = jnp.concatenate([r[0] for r in routed], axis=0) if len(routed) > 1 else routed[0][0]
    slots_all = jnp.concatenate(slots, axis=1) if len(slots) > 1 else slots[0]
    xs = _dispatch(fill_tiles, slots_all, hp_all, n_slots)
    y = _expert_ffn(tabs, xs, w1, b1, w2, b2, layer, max_items)
    outs = []
    for (x2d, mod_rows, per), sl, (_, _, wgt, _) in zip(parts, slots, routed):
        outs.append(_combine(sl, y, wgt, x2d, gate2[mod_rows][:, None, :], rows_per_mod=per))
    return outs


def _rope_tables(seq):
    t = np.arange(seq)
    m = HEAD_DIM // 4
    inv_freq = ROPE_THETA ** (-np.arange(m, dtype=np.float64) / m)
    ang_r = (t // GRID_W)[:, None] * inv_freq[None, :]
    ang_c = (t % GRID_W)[:, None] * inv_freq[None, :]
    cos = np.concatenate([np.cos(ang_r)] * 2 + [np.cos(ang_c)] * 2, axis=-1)
    sin = np.concatenate([-np.sin(ang_r), np.sin(ang_r), -np.sin(ang_c), np.sin(ang_c)], axis=-1)
    return jnp.asarray(cos, F32), jnp.asarray(sin, F32)


def kernel(x, c, ctx, c_ctx, ada_w, ada_b, norm_mix_g, norm_ffn_g, a_w_qkv, a_w_o, a_q_gain, a_k_gain,
           b_w_qkv, b_w_o, b_q_gain, b_k_gain, b_rel_bias, router_w, router_b, exp_w1, exp_b1, exp_w2, exp_b2):
    bsz, seq, d = x.shape
    l_ctx = ctx.shape[1]
    depth = ada_w.shape[0]
    n_heads = d // HEAD_DIM
    q_scale = HEAD_DIM ** -0.5 * LOG2E

    c_rows = jnp.zeros((SUBLANES, d), F32).at[:bsz].set(c).at[bsz].set(c_ctx)
    mod = _ada_modulation(c_rows, ada_w, ada_b)
    cos, sin = _rope_tables(seq)
    ones_tab = jnp.ones((l_ctx, HEAD_DIM), F32)
    x_rows = jnp.arange(bsz)
    c_rows_idx = jnp.full((bsz,), bsz)

    for i in range(depth):
        last = i == depth - 1
        sh1, sc1, g1, sh2, sc2, g2 = [mod[i, :, k * d:(k + 1) * d] for k in range(N_MOD)]
        j = i // 2
        if i % 2 == 0:
            w_qkv, w_o, qg, kg = a_w_qkv[j], a_w_o[j], a_q_gain[j], a_k_gain[j]
            n_q, n_k, n_v = n_heads, A_KV_HEADS, A_KV_HEADS
        else:
            w_qkv, w_o, qg, kg = b_w_qkv[j], b_w_o[j], b_q_gain[j], b_k_gain[j]
            n_q, n_k, n_v = n_heads, n_heads, n_heads
        heads_per_tile = 512 // HEAD_DIM
        gain_cols = jnp.concatenate([jnp.tile(qg * q_scale, n_q), jnp.tile(kg, n_k),
                                     jnp.ones((n_v * HEAD_DIM,), F32)])[None, :]
        flags = jnp.concatenate([jnp.ones(((n_q + n_k) // heads_per_tile,), I32),
                                 jnp.zeros((n_v // heads_per_tile,), I32)])
        rope = i % 2 == 0
        qkv_x = _qkv_project(x, sh1[x_rows][:, None, :], sc1[x_rows][:, None, :], norm_mix_g[i][None, :],
                             w_qkv, gain_cols, flags, cos, sin, rope=rope, tm=1024)
        qkv_c = _qkv_project(ctx, sh1[c_rows_idx][:, None, :], sc1[c_rows_idx][:, None, :],
                             norm_mix_g[i][None, :], w_qkv, gain_cols, flags, ones_tab, ones_tab,
                             rope=False, tm=l_ctx)
        if i % 2 == 0:
            ox = _attention(qkv_x, qkv_c, qkv_x, n_q_heads=n_q, n_kv_heads=n_k,
                            k_col0=n_q, v_col0=n_q + n_k, tq=256)
        else:
            ox = _neighborhood_attention(qkv_x, qkv_c, b_rel_bias[j], n_heads=n_heads)
        x = _out_project(ox, w_o, x, g1[x_rows][:, None, :], tm=1024)
        parts = [(x.reshape(bsz * seq, d), x_rows, seq)]
        if not last:
            oc = _attention(qkv_c, qkv_c, None, n_q_heads=n_q, n_kv_heads=n_k,
                            k_col0=n_q, v_col0=n_q + n_k, tq=l_ctx)
            ctx = _out_project(oc, w_o, ctx, g1[c_rows_idx][:, None, :], tm=l_ctx)
            parts.append((ctx.reshape(bsz * l_ctx, d), jnp.full((1,), bsz), bsz * l_ctx))
        outs = _moe(parts, sh2, sc2, g2, norm_ffn_g[i], router_w[i], router_b[i],
                    exp_w1, exp_b1, exp_w2, exp_b2, i)
        x = outs[0].reshape(bsz, seq, d)
        if not last:
            ctx = outs[1].reshape(bsz, l_ctx, d)
    return x
```

```python
import functools
import math

import jax
import jax.numpy as jnp
import numpy as np
from jax import lax
from jax.experimental import pallas as pl
from jax.experimental.pallas import tpu as pltpu

F32 = jnp.float32
BF16 = jnp.bfloat16
I32 = jnp.int32

LANES = 128
SUBLANES = 8
VMEM_BYTES_V7X = 64 * 1024 * 1024

HEAD_DIM = 128
GRID_W = 64
A_KV_HEADS = 4
NA_WIN_ROWS = 8
NA_WIN_COLS = 16
N_EXPERTS = 32
TOP_K = 4
ROPE_THETA = 10000.0
SWIGLU_ALPHA = 1.702
SWIGLU_LIMIT = 7.0
NORM_EPS = 1e-6
N_MOD = 6
LOG2E = math.log2(math.e)
NEG_INF = float("-inf")

MOE_TILE = 128


def _vmem_limit(nbytes):
    return int(min(nbytes + 8 * 1024 * 1024, VMEM_BYTES_V7X - 8 * 1024 * 1024))


def _params(semantics, vmem_bytes):
    return pltpu.CompilerParams(dimension_semantics=semantics,
                                vmem_limit_bytes=_vmem_limit(vmem_bytes))


def _ada_kernel(c_ref, w_ref, b_ref, o_ref):
    c = c_ref[...]
    a = (c * jax.nn.sigmoid(c)).astype(BF16)
    o_ref[0] = jnp.dot(a, w_ref[0].astype(BF16), preferred_element_type=F32) + b_ref[0]


def _ada_modulation(c_rows, ada_w, ada_b):
    depth, d, n = ada_w.shape
    rows = c_rows.shape[0]
    tn = 1024
    return pl.pallas_call(
        _ada_kernel,
        grid=(depth, n // tn),
        in_specs=[pl.BlockSpec((rows, d), lambda l, j: (0, 0)),
                  pl.BlockSpec((1, d, tn), lambda l, j: (l, 0, j)),
                  pl.BlockSpec((1, 1, tn), lambda l, j: (l, 0, j))],
        out_specs=pl.BlockSpec((1, rows, tn), lambda l, j: (l, 0, j)),
        out_shape=jax.ShapeDtypeStruct((depth, rows, n), F32),
        compiler_params=_params(("arbitrary", "arbitrary"), 2 * d * tn * 4 + d * tn * 2),
        name="ada_modulation",
    )(c_rows, ada_w, ada_b.reshape(depth, 1, n))


def _norm_modulate(x, g, shift, scale):
    ms = jnp.mean(x * x, axis=-1, keepdims=True)
    y = x * lax.rsqrt(ms + NORM_EPS) * g
    return y * (1.0 + scale) + shift


def _norm_mod_kernel(x_ref, sh_ref, sc_ref, g_ref, h_ref):
    h_ref[...] = _norm_modulate(x_ref[...], g_ref[...], sh_ref[...], sc_ref[...]).astype(BF16)


def _norm_mod(x, shift, scale, g, *, tm):
    b, t, d = x.shape
    return pl.pallas_call(
        _norm_mod_kernel,
        grid=(b, t // tm),
        in_specs=[pl.BlockSpec((None, tm, d), lambda bi, m: (bi, m, 0)),
                  pl.BlockSpec((None, 1, d), lambda bi, m: (bi, 0, 0)),
                  pl.BlockSpec((None, 1, d), lambda bi, m: (bi, 0, 0)),
                  pl.BlockSpec((1, d), lambda bi, m: (0, 0))],
        out_specs=pl.BlockSpec((None, tm, d), lambda bi, m: (bi, m, 0)),
        out_shape=jax.ShapeDtypeStruct((b, t, d), BF16),
        compiler_params=_params(("arbitrary", "arbitrary"), 2 * tm * d * 6 + 4 * tm * d * 4),
        name="norm_mod",
    )(x, shift, scale, g)


def _qkv_kernel(flag_ref, h_ref, w_ref, gain_ref, cos_ref, sin_ref, o_ref, wb_ref, *, rope):
    n = pl.program_id(0)

    @pl.when((pl.program_id(1) == 0) & (pl.program_id(2) == 0))
    def _():
        wb_ref[...] = w_ref[...].astype(BF16)

    tm, tn = o_ref.shape
    piece = min(tm, 256)
    pieces = [slice(r, r + piece) for r in range(0, tm, piece)]

    @pl.when(flag_ref[n] == 0)
    def _():
        for rs in pieces:
            o_ref[rs, :] = jnp.dot(h_ref[rs, :], wb_ref[...], preferred_element_type=F32).astype(BF16)

    @pl.when(flag_ref[n] == 1)
    def _():
        lane = lax.broadcasted_iota(I32, (1, HEAD_DIM), 1)
        first_half = (lane % (HEAD_DIM // 2)) < (HEAD_DIM // 4)
        for rs in pieces:
            a = jnp.dot(h_ref[rs, :], wb_ref[...], preferred_element_type=F32)
            for hh in range(tn // HEAD_DIM):
                sl = slice(hh * HEAD_DIM, (hh + 1) * HEAD_DIM)
                ah = a[:, sl]
                ms = jnp.mean(ah * ah, axis=-1, keepdims=True)
                yh = ah * lax.rsqrt(ms + NORM_EPS) * gain_ref[:, sl]
                if rope:
                    partner = jnp.where(first_half,
                                        pltpu.roll(yh, HEAD_DIM - HEAD_DIM // 4, 1),
                                        pltpu.roll(yh, HEAD_DIM // 4, 1))
                    yh = yh * cos_ref[rs, :] + partner * sin_ref[rs, :]
                o_ref[rs, sl] = yh.astype(BF16)


def _qkv_project(x, shift, scale, g, w, gain_cols, norm_flags, cos, sin, *, rope, tm):
    b, t, d = x.shape
    n = w.shape[1]
    tn = 512
    h = _norm_mod(x, shift, scale, g, tm=min(tm, 512))
    kernel = functools.partial(_qkv_kernel, rope=rope)
    grid_spec = pltpu.PrefetchScalarGridSpec(
        num_scalar_prefetch=1,
        grid=(n // tn, b, t // tm),
        in_specs=[pl.BlockSpec((None, tm, d), lambda j, bi, m, f: (bi, m, 0)),
                  pl.BlockSpec((d, tn), lambda j, bi, m, f: (0, j)),
                  pl.BlockSpec((1, tn), lambda j, bi, m, f: (0, j)),
                  pl.BlockSpec((tm, HEAD_DIM), lambda j, bi, m, f: (m, 0)),
                  pl.BlockSpec((tm, HEAD_DIM), lambda j, bi, m, f: (m, 0))],
        out_specs=pl.BlockSpec((None, tm, tn), lambda j, bi, m, f: (bi, m, j)),
        scratch_shapes=[pltpu.VMEM((d, tn), BF16)])
    vmem = 2 * tm * d * 2 + 2 * d * tn * 4 + d * tn * 2 + 6 * tm * tn * 4
    return pl.pallas_call(
        kernel, grid_spec=grid_spec,
        out_shape=jax.ShapeDtypeStruct((b, t, n), BF16),
        compiler_params=_params(("arbitrary", "arbitrary", "arbitrary"), vmem),
        name="qkv_project",
    )(norm_flags, h, w, gain_cols, cos, sin)


def _nt_dot(a, b):
    return lax.dot_general(a, b, (((1,), (1,)), ((), ())), preferred_element_type=F32)


def _values_with_ones(v_ref, dst_ref):
    dst_ref[:, :HEAD_DIM] = v_ref[...]
    dst_ref[:, HEAD_DIM:] = jnp.ones((v_ref.shape[0], HEAD_DIM), BF16)


def _attn_kernel(q_ref, kc_ref, vc_ref, *rest, group, has_x):
    if has_x:
        kx_ref, vx_ref, o_ref = rest
    else:
        (o_ref,) = rest
    kc = kc_ref[...]
    vc = vc_ref[...]
    for g in range(group):
        sl = slice(g * HEAD_DIM, (g + 1) * HEAD_DIM)
        q = q_ref[:, sl]
        sc = _nt_dot(q, kc)
        m = jnp.max(sc, axis=-1, keepdims=True)
        if has_x:
            sx = _nt_dot(q, kx_ref[...])
            m = jnp.maximum(m, jnp.max(sx, axis=-1, keepdims=True))
        pc = jnp.exp2(sc - m)
        l = jnp.sum(pc, axis=-1, keepdims=True)
        acc = jnp.dot(pc.astype(BF16), vc, preferred_element_type=F32)
        if has_x:
            px = jnp.exp2(sx - m)
            l = l + jnp.sum(px, axis=-1, keepdims=True)
            acc = acc + jnp.dot(px.astype(BF16), vx_ref[...], preferred_element_type=F32)
        o_ref[:, sl] = (acc / l).astype(BF16)


def _attention(q_src, ctx_src, x_src, *, n_q_heads, n_kv_heads, k_col0, v_col0, tq):
    b, t_q, _ = q_src.shape
    l = ctx_src.shape[1]
    group = n_q_heads // n_kv_heads
    has_x = x_src is not None
    gw = group * HEAD_DIM
    in_specs = [pl.BlockSpec((None, tq, gw), lambda bi, kv, i: (bi, i, kv)),
                pl.BlockSpec((None, l, HEAD_DIM), lambda bi, kv, i: (bi, 0, k_col0 + kv)),
                pl.BlockSpec((None, l, HEAD_DIM), lambda bi, kv, i: (bi, 0, v_col0 + kv))]
    args = [q_src, ctx_src, ctx_src]
    s = 0
    if has_x:
        s = x_src.shape[1]
        in_specs += [pl.BlockSpec((None, s, HEAD_DIM), lambda bi, kv, i: (bi, 0, k_col0 + kv)),
                     pl.BlockSpec((None, s, HEAD_DIM), lambda bi, kv, i: (bi, 0, v_col0 + kv))]
        args += [x_src, x_src]
    vmem = 4 * (l + s) * HEAD_DIM * 2 + 4 * tq * gw * 2 + 4 * tq * (l + s) * 4
    return pl.pallas_call(
        functools.partial(_attn_kernel, group=group, has_x=has_x),
        grid=(b, n_kv_heads, t_q // tq),
        in_specs=in_specs,
        out_specs=pl.BlockSpec((None, tq, gw), lambda bi, kv, i: (bi, i, kv)),
        out_shape=jax.ShapeDtypeStruct((b, t_q, n_q_heads * HEAD_DIM), BF16),
        compiler_params=_params(("arbitrary", "arbitrary", "arbitrary"), vmem),
        name="attention_x" if has_x else "attention_ctx",
    )(*args)


NA_Q_ROWS = 8
NA_BAND_ROWS = 2 * NA_WIN_ROWS
NA_DR = 2 * NA_WIN_ROWS - 1
NA_DC = 2 * NA_WIN_COLS - 1
NA_PAIR_TILES = 30


def _na_kernel(tab_ref, q_ref, k_ref, v_ref, kc_ref, vc_ref, o_ref, tile_ref, pair_ref, s_ref,
               v1_ref, vc1_ref, *, rows):
    h = pl.program_id(1)
    a = pl.program_id(2)
    nb = NA_BAND_ROWS * GRID_W

    @pl.when(a == 0)
    def _():
        _values_with_ones(v_ref, v1_ref)
        _values_with_ones(vc_ref, vc1_ref)
        qc = lax.broadcasted_iota(I32, (GRID_W, LANES), 0)
        kc = lax.broadcasted_iota(I32, (GRID_W, LANES), 1) % GRID_W
        dc = jnp.clip(kc - qc, -(NA_WIN_COLS - 1), NA_WIN_COLS - 1) + NA_WIN_COLS - 1
        c0 = jnp.clip(qc - NA_WIN_COLS // 2, 0, GRID_W - NA_WIN_COLS)
        col_ok = (kc >= c0) & (kc < c0 + NA_WIN_COLS)
        for dr in range(NA_DR):
            acc = jnp.zeros((GRID_W, LANES), F32)
            for d in range(NA_DC):
                acc = jnp.where(dc == d, tab_ref[h * (NA_DR * NA_DC) + dr * NA_DC + d] * LOG2E, acc)
            tile_ref[dr] = jnp.where(col_ok, acc, NEG_INF)
        left = lax.broadcasted_iota(I32, (GRID_W, LANES), 1) < GRID_W
        zero = jnp.zeros((GRID_W, LANES), F32)
        for p in range(NA_PAIR_TILES):
            lo = tile_ref[p - NA_WIN_ROWS] if 0 <= p - NA_WIN_ROWS < NA_DR else zero
            hi = tile_ref[p - NA_WIN_ROWS + 1] if 0 <= p - NA_WIN_ROWS + 1 < NA_DR else zero
            pair_ref[p] = jnp.where(left, lo, hi)

    wr = NA_WIN_ROWS
    kb0 = jnp.clip(a * NA_Q_ROWS - wr // 2, 0, rows - NA_BAND_ROWS)
    k_band = k_ref[pl.ds(pl.multiple_of(kb0 * GRID_W, 4 * GRID_W), nb), :]
    v_band = v1_ref[pl.ds(pl.multiple_of(kb0 * GRID_W, 4 * GRID_W), nb), :]
    q = q_ref[...]
    s_loc = _nt_dot(q, k_band)
    s_ctx = _nt_dot(q, kc_ref[...])

    band_row = lax.broadcasted_iota(I32, (1, nb), 1) // GRID_W
    for rq in range(NA_Q_ROWS):
        r = a * NA_Q_ROWS + rq
        r0 = jnp.clip(r - wr // 2, 0, rows - wr)
        lo = r0 - kb0
        row_mask = jnp.where((band_row >= lo) & (band_row < lo + wr), 0.0, NEG_INF)
        j0 = kb0 - r + 2 * wr - 1
        bias = jnp.concatenate([pair_ref[j0 + 2 * i] for i in range(NA_BAND_ROWS // 2)], axis=1)
        rs = slice(rq * GRID_W, (rq + 1) * GRID_W)
        sl = s_loc[rs, :] + bias + row_mask
        sc = s_ctx[rs, :]
        m = jnp.maximum(jnp.max(sl, axis=-1, keepdims=True), jnp.max(sc, axis=-1, keepdims=True))
        s_ref[rs, :nb] = jnp.exp2(sl - m).astype(BF16)
        s_ref[rs, nb:] = jnp.exp2(sc - m).astype(BF16)

    pb = s_ref[...]
    acc = (jnp.dot(pb[:, :nb], v_band, preferred_element_type=F32)
           + jnp.dot(pb[:, nb:], vc1_ref[...], preferred_element_type=F32))
    o_ref[...] = (acc[:, :HEAD_DIM] / acc[:, HEAD_DIM:]).astype(BF16)


def _neighborhood_attention(qkv_x, qkv_c, rel_bias, *, n_heads):
    b, s, _ = qkv_x.shape
    l = qkv_c.shape[1]
    rows = s // GRID_W
    nq = NA_Q_ROWS * GRID_W
    nb = NA_BAND_ROWS * GRID_W
    grid_spec = pltpu.PrefetchScalarGridSpec(
        num_scalar_prefetch=0,
        grid=(b, n_heads, rows // NA_Q_ROWS),
        in_specs=[pl.BlockSpec(memory_space=pltpu.SMEM),
                  pl.BlockSpec((None, nq, HEAD_DIM), lambda bi, h, a: (bi, a, h)),
                  pl.BlockSpec((None, s, HEAD_DIM), lambda bi, h, a: (bi, 0, n_heads + h)),
                  pl.BlockSpec((None, s, HEAD_DIM), lambda bi, h, a: (bi, 0, 2 * n_heads + h)),
                  pl.BlockSpec((None, l, HEAD_DIM), lambda bi, h, a: (bi, 0, n_heads + h)),
                  pl.BlockSpec((None, l, HEAD_DIM), lambda bi, h, a: (bi, 0, 2 * n_heads + h))],
        out_specs=pl.BlockSpec((None, nq, HEAD_DIM), lambda bi, h, a: (bi, a, h)),
        scratch_shapes=[pltpu.VMEM((NA_DR, GRID_W, LANES), F32),
                        pltpu.VMEM((NA_PAIR_TILES, GRID_W, LANES), F32),
                        pltpu.VMEM((nq, nb + l), BF16),
                        pltpu.VMEM((s, 2 * HEAD_DIM), BF16),
                        pltpu.VMEM((l, 2 * HEAD_DIM), BF16)])
    vmem = 6 * (s + l) * HEAD_DIM * 2 + 4 * nq * (nb + l) * 4 + 2 * 1024 * 1024
    return pl.pallas_call(
        functools.partial(_na_kernel, rows=rows), grid_spec=grid_spec,
        out_shape=jax.ShapeDtypeStruct((b, s, n_heads * HEAD_DIM), BF16),
        compiler_params=_params(("arbitrary", "arbitrary", "arbitrary"), vmem),
        name="neighborhood_attention",
    )(rel_bias.reshape(-1), qkv_x, qkv_x, qkv_x, qkv_c, qkv_c)


def _oproj_kernel(o_ref, w_ref, x_ref, g_ref, out_ref, wb_ref):
    @pl.when((pl.program_id(1) == 0) & (pl.program_id(2) == 0))
    def _():
        wb_ref[...] = w_ref[...].astype(BF16)

    y = jnp.dot(o_ref[...], wb_ref[...], preferred_element_type=F32)
    out_ref[...] = x_ref[...] + g_ref[...] * y


def _out_project(o, w, x, gate, *, tm):
    b, t, k = o.shape
    d = w.shape[1]
    tn = 512
    vmem = 2 * tm * k * 2 + 2 * k * tn * 4 + k * tn * 2 + 6 * tm * tn * 4
    return pl.pallas_call(
        _oproj_kernel,
        grid=(d // tn, b, t // tm),
        in_specs=[pl.BlockSpec((None, tm, k), lambda j, bi, m: (bi, m, 0)),
                  pl.BlockSpec((k, tn), lambda j, bi, m: (0, j)),
                  pl.BlockSpec((None, tm, tn), lambda j, bi, m: (bi, m, j)),
                  pl.BlockSpec((None, 1, tn), lambda j, bi, m: (bi, 0, j))],
        out_specs=pl.BlockSpec((None, tm, tn), lambda j, bi, m: (bi, m, j)),
        scratch_shapes=[pltpu.VMEM((k, tn), BF16)],
        out_shape=jax.ShapeDtypeStruct((b, t, d), F32),
        compiler_params=_params(("arbitrary", "arbitrary", "arbitrary"), vmem),
        name="out_project",
    )(o, w, x, gate)


ROUTER_BLOCK = 256


def _router_kernel(x_ref, sh_ref, sc_ref, g_ref, rw_ref, rb_ref, cnt_in_ref,
                   hp_ref, idx_ref, wgt_ref, rank_ref, cnt_ref, run_ref):
    i = pl.program_id(0)

    @pl.when(i == 0)
    def _():
        run_ref[...] = cnt_in_ref[...]

    h = _norm_modulate(x_ref[...], g_ref[...], sh_ref[...], sc_ref[...])
    tb, d = h.shape
    packed = pltpu.pack_elementwise([h[:, :d // 2], h[:, d // 2:]], packed_dtype=BF16)
    for k in range(SUBLANES):
        hp_ref[pl.ds(k, tb, stride=SUBLANES), :] = packed[:, k * LANES:(k + 1) * LANES]

    h_hi = h.astype(BF16)
    h_lo = (h - h_hi.astype(F32)).astype(BF16)
    hi_terms = jnp.dot(h_hi, rw_ref[...], preferred_element_type=F32)
    logits = (hi_terms[:, :LANES]
              + (hi_terms[:, LANES:] + jnp.dot(h_lo, rw_ref[:, :LANES], preferred_element_type=F32))
              + rb_ref[...])
    lane = lax.broadcasted_iota(I32, (tb, LANES), 1)
    lane_f = lane.astype(F32)
    work = logits
    idx_acc = jnp.zeros((tb, LANES), F32)
    val_acc = jnp.full((tb, LANES), NEG_INF, F32)
    hots = []
    for k in range(TOP_K):
        m = jnp.max(work, axis=-1, keepdims=True)
        idx = jnp.min(jnp.where(work == m, lane_f, float(LANES)), axis=-1, keepdims=True)
        hot = lane_f == idx
        hots.append(hot)
        idx_acc = jnp.where(lane == k, idx, idx_acc)
        val_acc = jnp.where(lane == k, m, val_acc)
        work = jnp.where(hot, NEG_INF, work)

    e = jnp.exp(val_acc - jnp.max(val_acc, axis=-1, keepdims=True))
    wgt_ref[...] = e / jnp.sum(e, axis=-1, keepdims=True)
    idx_ref[...] = idx_acc.astype(I32)

    chosen = jnp.zeros((tb, LANES), F32)
    for hot in hots:
        chosen = chosen + jnp.where(hot, 1.0, 0.0)
    row = lax.broadcasted_iota(I32, (tb, tb), 0)
    col = lax.broadcasted_iota(I32, (tb, tb), 1)
    earlier = jnp.where(col < row, 1.0, 0.0).astype(BF16)
    before = jnp.dot(earlier, chosen.astype(BF16), preferred_element_type=F32) + run_ref[...]
    rank_acc = jnp.zeros((tb, LANES), F32)
    for k, hot in enumerate(hots):
        rk = jnp.sum(jnp.where(hot, before, 0.0), axis=-1, keepdims=True)
        rank_acc = jnp.where(lane == k, rk, rank_acc)
    rank_ref[...] = rank_acc.astype(I32)
    run_ref[...] = run_ref[...] + jnp.sum(chosen, axis=0, keepdims=True)
    cnt_ref[...] = run_ref[...]


def _router(x2d, shift, scale, g, rw_pad, rb_pad, counts_in, *, rows_per_mod):
    n, d = x2d.shape
    tb = ROUTER_BLOCK
    per = rows_per_mod // tb
    vmem = 2 * tb * d * 4 + 6 * tb * d * 4 + 2 * d * LANES * 4
    outs = pl.pallas_call(
        _router_kernel,
        grid=(n // tb,),
        in_specs=[pl.BlockSpec((tb, d), lambda i: (i, 0)),
                  pl.BlockSpec((None, 1, d), lambda i: (i // per, 0, 0)),
                  pl.BlockSpec((None, 1, d), lambda i: (i // per, 0, 0)),
                  pl.BlockSpec((1, d), lambda i: (0, 0)),
                  pl.BlockSpec((d, 2 * LANES), lambda i: (0, 0)),
                  pl.BlockSpec((1, LANES), lambda i: (0, 0)),
                  pl.BlockSpec((1, LANES), lambda i: (0, 0))],
        out_specs=[pl.BlockSpec((tb * SUBLANES, LANES), lambda i: (i, 0)),
                   pl.BlockSpec((tb, LANES), lambda i: (i, 0)),
                   pl.BlockSpec((tb, LANES), lambda i: (i, 0)),
                   pl.BlockSpec((tb, LANES), lambda i: (i, 0)),
                   pl.BlockSpec((1, LANES), lambda i: (0, 0))],
        out_shape=[jax.ShapeDtypeStruct((n * SUBLANES, LANES), jnp.uint32),
                   jax.ShapeDtypeStruct((n, LANES), I32),
                   jax.ShapeDtypeStruct((n, LANES), F32),
                   jax.ShapeDtypeStruct((n, LANES), I32),
                   jax.ShapeDtypeStruct((1, LANES), F32)],
        scratch_shapes=[pltpu.VMEM((1, LANES), F32)],
        compiler_params=_params(("arbitrary",), vmem),
        name="router",
    )(x2d, shift, scale, g, rw_pad, rb_pad, counts_in)
    return outs


def _dispatch_kernel(fill_ref, slot_ref, hp_ref, xs_ref, zero_ref, sem, fill_sem):
    tb = hp_ref.shape[0] // SUBLANES
    t_rows = zero_ref.shape[0]

    @pl.when(pl.program_id(0) == 0)
    def _():
        zero_ref[...] = jnp.zeros_like(zero_ref)

        def tile_copy(t):
            return pltpu.make_async_copy(zero_ref, xs_ref.at[pl.ds(t * t_rows, t_rows), :], fill_sem)

        def fill(t, carry):
            @pl.when(fill_ref[t] == 1)
            def _():
                tile_copy(t).start()
            return carry

        def fill_wait(t, carry):
            @pl.when(fill_ref[t] == 1)
            def _():
                tile_copy(t).wait()
            return carry

        lax.fori_loop(0, fill_ref.shape[0], fill, 0)
        lax.fori_loop(0, fill_ref.shape[0], fill_wait, 0)

    def row_copy(j, s):
        src = j * SUBLANES if isinstance(j, int) else pl.multiple_of(j * SUBLANES, SUBLANES)
        dst = s * SUBLANES if isinstance(s, int) else pl.multiple_of(s * SUBLANES, SUBLANES)
        return pltpu.make_async_copy(hp_ref.at[pl.ds(src, SUBLANES), :],
                                     xs_ref.at[pl.ds(dst, SUBLANES), :], sem)

    def issue(j, carry):
        for k in range(TOP_K):
            row_copy(j, slot_ref[k, j]).start()
        return carry

    lax.fori_loop(0, tb, issue, 0)

    for k in range(TOP_K):
        pltpu.make_async_copy(hp_ref, xs_ref.at[pl.ds(0, tb * SUBLANES), :], sem).wait()


def _dispatch(fill_tiles, slots_t, hp, n_slots):
    n = hp.shape[0] // SUBLANES
    tb = ROUTER_BLOCK
    grid_spec = pltpu.PrefetchScalarGridSpec(
        num_scalar_prefetch=1,
        grid=(n // tb,),
        in_specs=[pl.BlockSpec((TOP_K, tb), lambda i, f: (0, i), memory_space=pltpu.SMEM),
                  pl.BlockSpec((tb * SUBLANES, LANES), lambda i, f: (i, 0))],
        out_specs=pl.BlockSpec(memory_space=pl.ANY),
        scratch_shapes=[pltpu.VMEM((MOE_TILE * SUBLANES, LANES), jnp.uint32),
                        pltpu.SemaphoreType.DMA(()), pltpu.SemaphoreType.DMA(())])
    return pl.pallas_call(
        _dispatch_kernel, grid_spec=grid_spec,
        out_shape=jax.ShapeDtypeStruct((n_slots * SUBLANES, LANES), jnp.uint32),
        compiler_params=_params(("arbitrary",), (2 * tb + MOE_TILE) * SUBLANES * LANES * 4),
        name="dispatch",
    )(fill_tiles, slots_t, hp)


EXPERT_TILES = 10
HIDDEN_CHUNK = 512


def _swiglu_pairs(a):
    even = (lax.broadcasted_iota(I32, (1, LANES), 1) % 2) == 0
    outs = []
    for c in range(a.shape[1] // (2 * LANES)):
        ga = a[:, (2 * c) * LANES:(2 * c + 1) * LANES]
        gb = a[:, (2 * c + 1) * LANES:(2 * c + 2) * LANES]
        glu = jnp.where(even, ga, pltpu.roll(gb, 1, 1))
        lin = jnp.where(even, pltpu.roll(ga, LANES - 1, 1), gb)
        glu = jnp.minimum(glu, SWIGLU_LIMIT)
        lin = jnp.clip(lin, -SWIGLU_LIMIT, SWIGLU_LIMIT)
        outs.append((glu * jax.nn.sigmoid(SWIGLU_ALPHA * glu) * (lin + 1.0)).astype(BF16))
    return outs[0] if len(outs) == 1 else jnp.concatenate(outs, axis=1)


def _interleaved_rows_bf16(w_ref, dst_ref):
    half = LANES // 2
    for qd in range(w_ref.shape[0] // LANES):
        wa = w_ref[qd * LANES:qd * LANES + half, :]
        wc = w_ref[qd * LANES + half:(qd + 1) * LANES, :]
        packed = pltpu.pack_elementwise([wa, wc], packed_dtype=BF16)
        dst_ref[qd * LANES:(qd + 1) * LANES, :] = pltpu.bitcast(packed, BF16)


def _expert_ffn_kernel(ie_ref, row0_ref, nt_ref, zt_ref, xs_ref, w1_ref, b1_ref, w2_ref, b2_ref, y_ref,
                       xb_ref, stage_ref, ybuf_ref, w1b_ref, w2b_ref, pend_ref,
                       sem_in, sem_out, *, n_chunks, n_items):
    del ie_ref
    i = pl.program_id(0)
    c = pl.program_id(1)
    t_rows = MOE_TILE
    nt = nt_ref[i]
    row0 = row0_ref[i]

    def out_copy(src_ref, r_src, r_dst):
        dst = r_dst if isinstance(r_dst, int) else pl.multiple_of(r_dst, t_rows)
        return pltpu.make_async_copy(src_ref.at[pl.ds(r_src, t_rows), :],
                                     y_ref.at[pl.ds(dst, t_rows), :], sem_out)

    def wait_pending():
        def body(_, carry):
            out_copy(ybuf_ref, 0, 0).wait()
            return carry
        lax.fori_loop(0, pend_ref[0], body, 0)
        pend_ref[0] = 0

    @pl.when((i == 0) & (c == 0))
    def _():
        pend_ref[0] = 0

    @pl.when(c == 0)
    def _():
        wait_pending()

        def in_copy(t, slot):
            src = pl.multiple_of((row0 + t * t_rows) * SUBLANES, t_rows * SUBLANES)
            return pltpu.make_async_copy(xs_ref.at[pl.ds(src, t_rows * SUBLANES), :],
                                         stage_ref.at[slot], sem_in.at[slot])

        @pl.when(nt > 0)
        def _():
            in_copy(0, 0).start()

        def load(t, carry):
            slot = t % 2
            in_copy(t, slot).wait()

            @pl.when(t + 1 < nt)
            def _():
                in_copy(t + 1, 1 - slot).start()

            r = pl.multiple_of(t * t_rows, t_rows)
            half = xb_ref.shape[1] // 2
            for k in range(SUBLANES):
                xw = stage_ref[slot, pl.ds(k, t_rows, stride=SUBLANES), :]
                lo = pltpu.unpack_elementwise(xw, index=0, packed_dtype=BF16, unpacked_dtype=F32)
                hi = pltpu.unpack_elementwise(xw, index=1, packed_dtype=BF16, unpacked_dtype=F32)
                xb_ref[pl.ds(r, t_rows), k * LANES:(k + 1) * LANES] = lo.astype(BF16)
                xb_ref[pl.ds(r, t_rows), half + k * LANES:half + (k + 1) * LANES] = hi.astype(BF16)
            return carry

        lax.fori_loop(0, nt, load, 0)

        def zfill(t, carry):
            out_copy(ybuf_ref, 0, row0 + t * t_rows).start()
            return carry

        @pl.when(zt_ref[i] > 0)
        def _():
            ybuf_ref[0:t_rows, :] = jnp.zeros((t_rows, ybuf_ref.shape[1]), F32)

        lax.fori_loop(0, zt_ref[i], zfill, 0)
        pend_ref[0] = zt_ref[i]

    @pl.when(nt > 0)
    def _():
        w1b_ref[...] = w1_ref[...].astype(BF16)
        _interleaved_rows_bf16(w2_ref, w2b_ref)
        last = c == n_chunks - 1

        def run(first):
            def rows_step(r, n_tiles):
                n_rows = n_tiles * t_rows
                xb = xb_ref[pl.ds(r, n_rows), :]
                a = jnp.dot(xb, w1b_ref[...], preferred_element_type=F32) + b1_ref[...]
                act = _swiglu_pairs(a)
                part = jnp.dot(act, w2b_ref[...], preferred_element_type=F32)
                if first:
                    ybuf_ref[pl.ds(r, n_rows), :] = part + b2_ref[...]
                else:
                    ybuf_ref[pl.ds(r, n_rows), :] += part

                if not first or n_chunks == 1:
                    @pl.when(last)
                    def _():
                        for k in range(n_tiles):
                            out_copy(ybuf_ref, r + k * t_rows, row0 + r + k * t_rows).start()

            def quad(p, carry):
                rows_step(pl.multiple_of(p * (4 * t_rows), 4 * t_rows), 4)
                return carry

            lax.fori_loop(0, nt // 4, quad, 0)

            @pl.when(nt % 4 >= 2)
            def _():
                rows_step(pl.multiple_of((nt // 4) * (4 * t_rows), 2 * t_rows), 2)

            @pl.when(nt % 2 == 1)
            def _():
                rows_step(pl.multiple_of((nt - 1) * t_rows, t_rows), 1)

        @pl.when(c == 0)
        def _():
            run(True)

        @pl.when(c != 0)
        def _():
            run(False)

        @pl.when(last)
        def _():
            pend_ref[0] = nt

    @pl.when((i == n_items - 1) & (c == n_chunks - 1))
    def _():
        wait_pending()


def _expert_ffn(item_tabs, xs, w1, b1, w2, b2, layer, max_items):
    p = xs.shape[0] // SUBLANES
    half = w1.shape[2] // 2
    _, e, d, n = w1.shape
    hc = HIDDEN_CHUNK
    n_chunks = (n // 2) // hc
    rows = EXPERT_TILES * MOE_TILE

    def chunk(c, nt, i):
        return jnp.where(nt[i] > 0, c, n_chunks - 1)

    grid_spec = pltpu.PrefetchScalarGridSpec(
        num_scalar_prefetch=4,
        grid=(max_items, n_chunks),
        in_specs=[pl.BlockSpec(memory_space=pl.ANY),
                  pl.BlockSpec((None, None, d, 2 * hc), lambda i, c, ie, r0, nt, zt: (layer, ie[i], 0, chunk(c, nt, i))),
                  pl.BlockSpec((None, None, 1, 2 * hc), lambda i, c, ie, r0, nt, zt: (layer, ie[i], 0, chunk(c, nt, i))),
                  pl.BlockSpec((None, None, hc, d), lambda i, c, ie, r0, nt, zt: (layer, ie[i], chunk(c, nt, i), 0)),
                  pl.BlockSpec((None, None, 1, d), lambda i, c, ie, r0, nt, zt: (layer, ie[i], 0, 0))],
        out_specs=pl.BlockSpec(memory_space=pl.ANY),
        scratch_shapes=[pltpu.VMEM((rows, d), BF16),
                        pltpu.VMEM((2, MOE_TILE * SUBLANES, LANES), jnp.uint32),
                        pltpu.VMEM((rows, d), F32),
                        pltpu.VMEM((d, 2 * hc), BF16),
                        pltpu.VMEM((hc, d), BF16),
                        pltpu.SMEM((1,), I32),
                        pltpu.SemaphoreType.DMA((2,)),
                        pltpu.SemaphoreType.DMA(())])
    vmem = (rows * d * 6 + 2 * MOE_TILE * half * 4 + MOE_TILE * d * 4
            + 3 * d * 2 * hc * 4 + d * 2 * hc * 2 + 3 * hc * d * 4 + hc * d * 2
            + 8 * MOE_TILE * 2 * hc * 4 + 4 * MOE_TILE * d * 4)
    depth = w1.shape[0]
    return pl.pallas_call(
        functools.partial(_expert_ffn_kernel, n_chunks=n_chunks, n_items=max_items), grid_spec=grid_spec,
        out_shape=jax.ShapeDtypeStruct((p, d), F32),
        compiler_params=_params(("arbitrary", "arbitrary"), vmem),
        name="expert_ffn",
    )(*item_tabs, xs, w1, b1.reshape(depth, e, 1, n), w2, b2.reshape(depth, e, 1, d))


COMBINE_BLOCK = 128


def _combine_kernel(slot_ref, slot_next_ref, y_ref, w_ref, x_ref, g_ref, o_ref, buf, sem, *, nblk):
    i = pl.program_id(0)
    tb = x_ref.shape[0]

    def row_copy(s, b, k, j):
        return pltpu.make_async_copy(y_ref.at[pl.ds(s, 1), :], buf.at[b, k, pl.ds(j, 1), :], sem.at[b])

    def issue(slots, b):
        def body(j8, carry):
            for jj in range(SUBLANES):
                j = j8 * SUBLANES + jj
                for k in range(TOP_K):
                    row_copy(slots[k, j], b, k, j).start()
            return carry
        lax.fori_loop(0, tb // SUBLANES, body, 0)

    @pl.when(i == 0)
    def _():
        issue(slot_ref, 0)

    @pl.when(i + 1 < nblk)
    def _():
        issue(slot_next_ref, (i + 1) % 2)

    cur = i % 2
    for k in range(TOP_K):
        pltpu.make_async_copy(y_ref.at[pl.ds(0, tb), :], buf.at[cur, k], sem.at[cur]).wait()

    w = w_ref[...]
    acc = w[:, 0:1] * buf[cur, 0]
    for k in range(1, TOP_K):
        acc = acc + w[:, k:k + 1] * buf[cur, k]
    o_ref[...] = x_ref[...] + g_ref[...] * acc


def _combine(slots_t, y, gates, x2d, gate_mod, *, rows_per_mod):
    n, d = x2d.shape
    tb = COMBINE_BLOCK
    nblk = n // tb
    per = rows_per_mod // tb
    vmem = 2 * TOP_K * tb * d * 4 + 6 * tb * d * 4
    return pl.pallas_call(
        functools.partial(_combine_kernel, nblk=nblk),
        grid=(nblk,),
        in_specs=[pl.BlockSpec((TOP_K, tb), lambda i: (0, i), memory_space=pltpu.SMEM),
                  pl.BlockSpec((TOP_K, tb), lambda i: (0, jnp.minimum(i + 1, nblk - 1)),
                               memory_space=pltpu.SMEM),
                  pl.BlockSpec(memory_space=pl.ANY),
                  pl.BlockSpec((tb, LANES), lambda i: (i, 0)),
                  pl.BlockSpec((tb, d), lambda i: (i, 0)),
                  pl.BlockSpec((None, 1, d), lambda i: (i // per, 0, 0))],
        out_specs=pl.BlockSpec((tb, d), lambda i: (i, 0)),
        out_shape=jax.ShapeDtypeStruct((n, d), F32),
        scratch_shapes=[pltpu.VMEM((2, TOP_K, tb, d), F32), pltpu.SemaphoreType.DMA((2,))],
        compiler_params=_params(("arbitrary",), vmem),
        name="combine",
    )(slots_t, slots_t, y, gates, x2d, gate_mod)


def _owner(ends, idx):
    return jnp.minimum(jnp.sum((ends[None, :] <= idx[:, None]).astype(I32), axis=1), ends.shape[0] - 1)


def _schedule(counts, max_tiles, max_items):
    t_rows = MOE_TILE
    tiles_e = (counts + t_rows - 1) // t_rows
    tile_end = jnp.cumsum(tiles_e)
    tile_start = tile_end - tiles_e
    num_tiles = tile_end[-1]
    t = jnp.arange(max_tiles, dtype=I32)
    te = _owner(tile_end, jnp.minimum(t, num_tiles - 1))
    fill = jnp.where((t < num_tiles) & (t != tile_end[te] - 1), 0, 1)

    items_e = (tiles_e + EXPERT_TILES - 1) // EXPERT_TILES
    item_end = jnp.cumsum(items_e)
    item_start = item_end - items_e
    n_items = item_end[-1]
    i = jnp.arange(max_items, dtype=I32)
    live = i < n_items
    ie = _owner(item_end, jnp.minimum(i, n_items - 1))
    local = jnp.minimum(i, n_items - 1) - item_start[ie]
    nt = jnp.where(live, jnp.clip(tiles_e[ie] - local * EXPERT_TILES, 0, EXPERT_TILES), 0)
    dead_tile0 = num_tiles + (i - n_items) * EXPERT_TILES
    zt = jnp.where(live, 0, jnp.clip(max_tiles - dead_tile0, 0, EXPERT_TILES))
    row0 = jnp.where(live, (tile_start[ie] + local * EXPERT_TILES) * t_rows,
                     jnp.clip(dead_tile0, 0, max_tiles - 1) * t_rows)
    return ((tile_start * t_rows).astype(I32), fill.astype(I32),
            (ie.astype(I32), row0.astype(I32), nt.astype(I32), zt.astype(I32)))


def _moe(parts, shift2, scale2, gate2, norm_g, router_w, router_b, w1, b1, w2, b2, layer):
    d = router_w.shape[0]
    rw_f32 = jnp.zeros((d, LANES), F32).at[:, :N_EXPERTS].set(router_w)
    rw_hi = lax.bitcast_convert_type(
        lax.bitcast_convert_type(rw_f32, jnp.uint32) & jnp.uint32(0xFFFF0000), F32)
    rw_pad = jnp.concatenate([rw_hi.astype(BF16), (rw_f32 - rw_hi).astype(BF16)], axis=1)
    rb_pad = jnp.full((1, LANES), NEG_INF, F32).at[0, :N_EXPERTS].set(router_b)
    counts = jnp.zeros((1, LANES), F32)
    routed = []
    for x2d, mod_rows, per in parts:
        hp, idx, wgt, rank, counts = _router(
            x2d, shift2[mod_rows][:, None, :], scale2[mod_rows][:, None, :], norm_g[None, :],
            rw_pad, rb_pad, counts, rows_per_mod=per)
        routed.append((hp, idx[:, :TOP_K], wgt, rank[:, :TOP_K]))
    n_total = sum(x2d.shape[0] for x2d, _, _ in parts)
    max_tiles = (n_total * TOP_K) // MOE_TILE + N_EXPERTS
    n_slots = max_tiles * MOE_TILE
    max_items = N_EXPERTS + -(-max_tiles // EXPERT_TILES)
    group_start, fill_tiles, tabs = _schedule(counts[0, :N_EXPERTS].astype(I32), max_tiles, max_items)

    slots = [(group_start[idx] + rank).T for _, idx, _, rank in routed]
    hp_all = jnp.concatenate([r[0] for r in routed], axis=0) if len(routed) > 1 else routed[0][0]
    slots_all = jnp.concatenate(slots, axis=1) if len(slots) > 1 else slots[0]
    xs = _dispatch(fill_tiles, slots_all, hp_all, n_slots)
    y = _expert_ffn(tabs, xs, w1, b1, w2, b2, layer, max_items)
    outs = []
    for (x2d, mod_rows, per), sl, (_, _, wgt, _) in zip(parts, slots, routed):
        outs.append(_combine(sl, y, wgt, x2d, gate2[mod_rows][:, None, :], rows_per_mod=per))
    return outs


def _rope_tables(seq):
    t = np.arange(seq)
    m = HEAD_DIM // 4
    inv_freq = ROPE_THETA ** (-np.arange(m, dtype=np.float64) / m)
    ang_r = (t // GRID_W)[:, None] * inv_freq[None, :]
    ang_c = (t % GRID_W)[:, None] * inv_freq[None, :]
    cos = np.concatenate([np.cos(ang_r)] * 2 + [np.cos(ang_c)] * 2, axis=-1)
    sin = np.concatenate([-np.sin(ang_r), np.sin(ang_r), -np.sin(ang_c), np.sin(ang_c)], axis=-1)
    return jnp.asarray(cos, F32), jnp.asarray(sin, F32)


def kernel(x, c, ctx, c_ctx, ada_w, ada_b, norm_mix_g, norm_ffn_g, a_w_qkv, a_w_o, a_q_gain, a_k_gain,
           b_w_qkv, b_w_o, b_q_gain, b_k_gain, b_rel_bias, router_w, router_b, exp_w1, exp_b1, exp_w2, exp_b2):
    bsz, seq, d = x.shape
    l_ctx = ctx.shape[1]
    depth = ada_w.shape[0]
    n_heads = d // HEAD_DIM
    q_scale = HEAD_DIM ** -0.5 * LOG2E

    c_rows = jnp.zeros((SUBLANES, d), F32).at[:bsz].set(c).at[bsz].set(c_ctx)
    mod = _ada_modulation(c_rows, ada_w, ada_b)
    cos, sin = _rope_tables(seq)
    ones_tab = jnp.ones((l_ctx, HEAD_DIM), F32)
    x_rows = jnp.arange(bsz)
    c_rows_idx = jnp.full((bsz,), bsz)

    for i in range(depth):
        last = i == depth - 1
        sh1, sc1, g1, sh2, sc2, g2 = [mod[i, :, k * d:(k + 1) * d] for k in range(N_MOD)]
        j = i // 2
        if i % 2 == 0:
            w_qkv, w_o, qg, kg = a_w_qkv[j], a_w_o[j], a_q_gain[j], a_k_gain[j]
            n_q, n_k, n_v = n_heads, A_KV_HEADS, A_KV_HEADS
        else:
            w_qkv, w_o, qg, kg = b_w_qkv[j], b_w_o[j], b_q_gain[j], b_k_gain[j]
            n_q, n_k, n_v = n_heads, n_heads, n_heads
        heads_per_tile = 512 // HEAD_DIM
        gain_cols = jnp.concatenate([jnp.tile(qg * q_scale, n_q), jnp.tile(kg, n_k),
                                     jnp.ones((n_v * HEAD_DIM,), F32)])[None, :]
        flags = jnp.concatenate([jnp.ones(((n_q + n_k) // heads_per_tile,), I32),
                                 jnp.zeros((n_v // heads_per_tile,), I32)])
        rope = i % 2 == 0
        qkv_x = _qkv_project(x, sh1[x_rows][:, None, :], sc1[x_rows][:, None, :], norm_mix_g[i][None, :],
                             w_qkv, gain_cols, flags, cos, sin, rope=rope, tm=1024)
        qkv_c = _qkv_project(ctx, sh1[c_rows_idx][:, None, :], sc1[c_rows_idx][:, None, :],
                             norm_mix_g[i][None, :], w_qkv, gain_cols, flags, ones_tab, ones_tab,
                             rope=False, tm=l_ctx)
        if i % 2 == 0:
            ox = _attention(qkv_x, qkv_c, qkv_x, n_q_heads=n_q, n_kv_heads=n_k,
                            k_col0=n_q, v_col0=n_q + n_k, tq=256)
        else:
            ox = _neighborhood_attention(qkv_x, qkv_c, b_rel_bias[j], n_heads=n_heads)
        x = _out_project(ox, w_o, x, g1[x_rows][:, None, :], tm=1024)
        parts = [(x.reshape(bsz * seq, d), x_rows, seq)]
        if not last:
            oc = _attention(qkv_c, qkv_c, None, n_q_heads=n_q, n_kv_heads=n_k,
                            k_col0=n_q, v_col0=n_q + n_k, tq=l_ctx)
            ctx = _out_project(oc, w_o, ctx, g1[c_rows_idx][:, None, :], tm=l_ctx)
            parts.append((ctx.reshape(bsz * l_ctx, d), jnp.full((1,), bsz), bsz * l_ctx))
        outs = _moe(parts, sh2, sc2, g2, norm_ffn_g[i], router_w[i], router_b[i],
                    exp_w1, exp_b1, exp_w2, exp_b2, i)
        x = outs[0].reshape(bsz, seq, d)
        if not last:
            ctx = outs[1].reshape(bsz, l_ctx, d)
    return x
```

```python
import functools
import math

import jax
import jax.numpy as jnp
import numpy as np
from jax import lax
from jax.experimental import pallas as pl
from jax.experimental.pallas import tpu as pltpu

F32 = jnp.float32
BF16 = jnp.bfloat16
I32 = jnp.int32

LANES = 128
SUBLANES = 8
VMEM_BYTES_V7X = 64 * 1024 * 1024

HEAD_DIM = 128
GRID_W = 64
A_KV_HEADS = 4
NA_WIN_ROWS = 8
NA_WIN_COLS = 16
N_EXPERTS = 32
TOP_K = 4
ROPE_THETA = 10000.0
SWIGLU_ALPHA = 1.702
SWIGLU_LIMIT = 7.0
NORM_EPS = 1e-6
N_MOD = 6
LOG2E = math.log2(math.e)
NEG_INF = float("-inf")

MOE_TILE = 128


def _vmem_limit(nbytes):
    return int(min(nbytes + 8 * 1024 * 1024, VMEM_BYTES_V7X - 8 * 1024 * 1024))


def _params(semantics, vmem_bytes):
    return pltpu.CompilerParams(dimension_semantics=semantics,
                                vmem_limit_bytes=_vmem_limit(vmem_bytes))


def _ada_kernel(c_ref, w_ref, b_ref, o_ref):
    c = c_ref[...]
    a = (c * jax.nn.sigmoid(c)).astype(BF16)
    o_ref[0] = jnp.dot(a, w_ref[0].astype(BF16), preferred_element_type=F32) + b_ref[0]


def _ada_modulation(c_rows, ada_w, ada_b):
    depth, d, n = ada_w.shape
    rows = c_rows.shape[0]
    tn = 1024
    return pl.pallas_call(
        _ada_kernel,
        grid=(depth, n // tn),
        in_specs=[pl.BlockSpec((rows, d), lambda l, j: (0, 0)),
                  pl.BlockSpec((1, d, tn), lambda l, j: (l, 0, j)),
                  pl.BlockSpec((1, 1, tn), lambda l, j: (l, 0, j))],
        out_specs=pl.BlockSpec((1, rows, tn), lambda l, j: (l, 0, j)),
        out_shape=jax.ShapeDtypeStruct((depth, rows, n), F32),
        compiler_params=_params(("arbitrary", "arbitrary"), 2 * d * tn * 4 + d * tn * 2),
        name="ada_modulation",
    )(c_rows, ada_w, ada_b.reshape(depth, 1, n))


def _norm_modulate(x, g, shift, scale):
    ms = jnp.mean(x * x, axis=-1, keepdims=True)
    y = x * lax.rsqrt(ms + NORM_EPS) * g
    return y * (1.0 + scale) + shift


def _norm_mod_kernel(x_ref, sh_ref, sc_ref, g_ref, h_ref):
    h_ref[...] = _norm_modulate(x_ref[...], g_ref[...], sh_ref[...], sc_ref[...]).astype(BF16)


def _norm_mod(x, shift, scale, g, *, tm):
    b, t, d = x.shape
    return pl.pallas_call(
        _norm_mod_kernel,
        grid=(b, t // tm),
        in_specs=[pl.BlockSpec((None, tm, d), lambda bi, m: (bi, m, 0)),
                  pl.BlockSpec((None, 1, d), lambda bi, m: (bi, 0, 0)),
                  pl.BlockSpec((None, 1, d), lambda bi, m: (bi, 0, 0)),
                  pl.BlockSpec((1, d), lambda bi, m: (0, 0))],
        out_specs=pl.BlockSpec((None, tm, d), lambda bi, m: (bi, m, 0)),
        out_shape=jax.ShapeDtypeStruct((b, t, d), BF16),
        compiler_params=_params(("arbitrary", "arbitrary"), 2 * tm * d * 6 + 4 * tm * d * 4),
        name="norm_mod",
    )(x, shift, scale, g)


def _qkv_kernel(flag_ref, h_ref, w_ref, gain_ref, cos_ref, sin_ref, o_ref, wb_ref, *, rope):
    n = pl.program_id(0)

    @pl.when((pl.program_id(1) == 0) & (pl.program_id(2) == 0))
    def _():
        wb_ref[...] = w_ref[...].astype(BF16)

    tm, tn = o_ref.shape
    piece = min(tm, 256)
    pieces = [slice(r, r + piece) for r in range(0, tm, piece)]

    @pl.when(flag_ref[n] == 0)
    def _():
        for rs in pieces:
            o_ref[rs, :] = jnp.dot(h_ref[rs, :], wb_ref[...], preferred_element_type=F32).astype(BF16)

    @pl.when(flag_ref[n] == 1)
    def _():
        lane = lax.broadcasted_iota(I32, (1, HEAD_DIM), 1)
        first_half = (lane % (HEAD_DIM // 2)) < (HEAD_DIM // 4)
        for rs in pieces:
            a = jnp.dot(h_ref[rs, :], wb_ref[...], preferred_element_type=F32)
            for hh in range(tn // HEAD_DIM):
                sl = slice(hh * HEAD_DIM, (hh + 1) * HEAD_DIM)
                ah = a[:, sl]
                ms = jnp.mean(ah * ah, axis=-1, keepdims=True)
                yh = ah * lax.rsqrt(ms + NORM_EPS) * gain_ref[:, sl]
                if rope:
                    partner = jnp.where(first_half,
                                        pltpu.roll(yh, HEAD_DIM - HEAD_DIM // 4, 1),
                                        pltpu.roll(yh, HEAD_DIM // 4, 1))
                    yh = yh * cos_ref[rs, :] + partner * sin_ref[rs, :]
                o_ref[rs, sl] = yh.astype(BF16)


def _qkv_project(x, shift, scale, g, w, gain_cols, norm_flags, cos, sin, *, rope, tm):
    b, t, d = x.shape
    n = w.shape[1]
    tn = 512
    h = _norm_mod(x, shift, scale, g, tm=min(tm, 512))
    kernel = functools.partial(_qkv_kernel, rope=rope)
    grid_spec = pltpu.PrefetchScalarGridSpec(
        num_scalar_prefetch=1,
        grid=(n // tn, b, t // tm),
        in_specs=[pl.BlockSpec((None, tm, d), lambda j, bi, m, f: (bi, m, 0)),
                  pl.BlockSpec((d, tn), lambda j, bi, m, f: (0, j)),
                  pl.BlockSpec((1, tn), lambda j, bi, m, f: (0, j)),
                  pl.BlockSpec((tm, HEAD_DIM), lambda j, bi, m, f: (m, 0)),
                  pl.BlockSpec((tm, HEAD_DIM), lambda j, bi, m, f: (m, 0))],
        out_specs=pl.BlockSpec((None, tm, tn), lambda j, bi, m, f: (bi, m, j)),
        scratch_shapes=[pltpu.VMEM((d, tn), BF16)])
    vmem = 2 * tm * d * 2 + 2 * d * tn * 4 + d * tn * 2 + 6 * tm * tn * 4
    return pl.pallas_call(
        kernel, grid_spec=grid_spec,
        out_shape=jax.ShapeDtypeStruct((b, t, n), BF16),
        compiler_params=_params(("arbitrary", "arbitrary", "arbitrary"), vmem),
        name="qkv_project",
    )(norm_flags, h, w, gain_cols, cos, sin)


def _nt_dot(a, b):
    return lax.dot_general(a, b, (((1,), (1,)), ((), ())), preferred_element_type=F32)


def _values_with_ones(v_ref, dst_ref):
    dst_ref[:, :HEAD_DIM] = v_ref[...]
    dst_ref[:, HEAD_DIM:] = jnp.ones((v_ref.shape[0], HEAD_DIM), BF16)


def _attn_kernel(q_ref, kc_ref, vc_ref, *rest, group, has_x):
    if has_x:
        kx_ref, vx_ref, o_ref = rest
    else:
        (o_ref,) = rest
    kc = kc_ref[...]
    vc = vc_ref[...]
    for g in range(group):
        sl = slice(g * HEAD_DIM, (g + 1) * HEAD_DIM)
        q = q_ref[:, sl]
        sc = _nt_dot(q, kc)
        m = jnp.max(sc, axis=-1, keepdims=True)
        if has_x:
            sx = _nt_dot(q, kx_ref[...])
            m = jnp.maximum(m, jnp.max(sx, axis=-1, keepdims=True))
        pc = jnp.exp2(sc - m)
        l = jnp.sum(pc, axis=-1, keepdims=True)
        acc = jnp.dot(pc.astype(BF16), vc, preferred_element_type=F32)
        if has_x:
            px = jnp.exp2(sx - m)
            l = l + jnp.sum(px, axis=-1, keepdims=True)
            acc = acc + jnp.dot(px.astype(BF16), vx_ref[...], preferred_element_type=F32)
        o_ref[:, sl] = (acc / l).astype(BF16)


def _attention(q_src, ctx_src, x_src, *, n_q_heads, n_kv_heads, k_col0, v_col0, tq):
    b, t_q, _ = q_src.shape
    l = ctx_src.shape[1]
    group = n_q_heads // n_kv_heads
    has_x = x_src is not None
    gw = group * HEAD_DIM
    in_specs = [pl.BlockSpec((None, tq, gw), lambda bi, kv, i: (bi, i, kv)),
                pl.BlockSpec((None, l, HEAD_DIM), lambda bi, kv, i: (bi, 0, k_col0 + kv)),
                pl.BlockSpec((None, l, HEAD_DIM), lambda bi, kv, i: (bi, 0, v_col0 + kv))]
    args = [q_src, ctx_src, ctx_src]
    s = 0
    if has_x:
        s = x_src.shape[1]
        in_specs += [pl.BlockSpec((None, s, HEAD_DIM), lambda bi, kv, i: (bi, 0, k_col0 + kv)),
                     pl.BlockSpec((None, s, HEAD_DIM), lambda bi, kv, i: (bi, 0, v_col0 + kv))]
        args += [x_src, x_src]
    vmem = 4 * (l + s) * HEAD_DIM * 2 + 4 * tq * gw * 2 + 4 * tq * (l + s) * 4
    return pl.pallas_call(
        functools.partial(_attn_kernel, group=group, has_x=has_x),
        grid=(b, n_kv_heads, t_q // tq),
        in_specs=in_specs,
        out_specs=pl.BlockSpec((None, tq, gw), lambda bi, kv, i: (bi, i, kv)),
        out_shape=jax.ShapeDtypeStruct((b, t_q, n_q_heads * HEAD_DIM), BF16),
        compiler_params=_params(("arbitrary", "arbitrary", "arbitrary"), vmem),
        name="attention_x" if has_x else "attention_ctx",
    )(*args)


NA_Q_ROWS = 8
NA_BAND_ROWS = 2 * NA_WIN_ROWS
NA_DR = 2 * NA_WIN_ROWS - 1
NA_DC = 2 * NA_WIN_COLS - 1
NA_PAIR_TILES = 30


def _na_kernel(tab_ref, q_ref, k_ref, v_ref, kc_ref, vc_ref, o_ref, tile_ref, pair_ref, s_ref,
               v1_ref, vc1_ref, *, rows):
    h = pl.program_id(1)
    a = pl.program_id(2)
    nb = NA_BAND_ROWS * GRID_W

    @pl.when(a == 0)
    def _():
        _values_with_ones(v_ref, v1_ref)
        _values_with_ones(vc_ref, vc1_ref)
        qc = lax.broadcasted_iota(I32, (GRID_W, LANES), 0)
        kc = lax.broadcasted_iota(I32, (GRID_W, LANES), 1) % GRID_W
        dc = jnp.clip(kc - qc, -(NA_WIN_COLS - 1), NA_WIN_COLS - 1) + NA_WIN_COLS - 1
        c0 = jnp.clip(qc - NA_WIN_COLS // 2, 0, GRID_W - NA_WIN_COLS)
        col_ok = (kc >= c0) & (kc < c0 + NA_WIN_COLS)
        for dr in range(NA_DR):
            acc = jnp.zeros((GRID_W, LANES), F32)
            for d in range(NA_DC):
                acc = jnp.where(dc == d, tab_ref[h * (NA_DR * NA_DC) + dr * NA_DC + d] * LOG2E, acc)
            tile_ref[dr] = jnp.where(col_ok, acc, NEG_INF)
        left = lax.broadcasted_iota(I32, (GRID_W, LANES), 1) < GRID_W
        zero = jnp.zeros((GRID_W, LANES), F32)
        for p in range(NA_PAIR_TILES):
            lo = tile_ref[p - NA_WIN_ROWS] if 0 <= p - NA_WIN_ROWS < NA_DR else zero
            hi = tile_ref[p - NA_WIN_ROWS + 1] if 0 <= p - NA_WIN_ROWS + 1 < NA_DR else zero
            pair_ref[p] = jnp.where(left, lo, hi)

    wr = NA_WIN_ROWS
    kb0 = jnp.clip(a * NA_Q_ROWS - wr // 2, 0, rows - NA_BAND_ROWS)
    k_band = k_ref[pl.ds(pl.multiple_of(kb0 * GRID_W, 4 * GRID_W), nb), :]
    v_band = v1_ref[pl.ds(pl.multiple_of(kb0 * GRID_W, 4 * GRID_W), nb), :]
    q = q_ref[...]
    s_loc = _nt_dot(q, k_band)
    s_ctx = _nt_dot(q, kc_ref[...])

    band_row = lax.broadcasted_iota(I32, (1, nb), 1) // GRID_W
    for rq in range(NA_Q_ROWS):
        r = a * NA_Q_ROWS + rq
        r0 = jnp.clip(r - wr // 2, 0, rows - wr)
        lo = r0 - kb0
        row_mask = jnp.where((band_row >= lo) & (band_row < lo + wr), 0.0, NEG_INF)
        j0 = kb0 - r + 2 * wr - 1
        bias = jnp.concatenate([pair_ref[j0 + 2 * i] for i in range(NA_BAND_ROWS // 2)], axis=1)
        rs = slice(rq * GRID_W, (rq + 1) * GRID_W)
        sl = s_loc[rs, :] + bias + row_mask
        sc = s_ctx[rs, :]
        m = jnp.maximum(jnp.max(sl, axis=-1, keepdims=True), jnp.max(sc, axis=-1, keepdims=True))
        s_ref[rs, :nb] = jnp.exp2(sl - m).astype(BF16)
        s_ref[rs, nb:] = jnp.exp2(sc - m).astype(BF16)

    pb = s_ref[...]
    acc = (jnp.dot(pb[:, :nb], v_band, preferred_element_type=F32)
           + jnp.dot(pb[:, nb:], vc1_ref[...], preferred_element_type=F32))
    o_ref[...] = (acc[:, :HEAD_DIM] / acc[:, HEAD_DIM:]).astype(BF16)


def _neighborhood_attention(qkv_x, qkv_c, rel_bias, *, n_heads):
    b, s, _ = qkv_x.shape
    l = qkv_c.shape[1]
    rows = s // GRID_W
    nq = NA_Q_ROWS * GRID_W
    nb = NA_BAND_ROWS * GRID_W
    grid_spec = pltpu.PrefetchScalarGridSpec(
        num_scalar_prefetch=0,
        grid=(b, n_heads, rows // NA_Q_ROWS),
        in_specs=[pl.BlockSpec(memory_space=pltpu.SMEM),
                  pl.BlockSpec((None, nq, HEAD_DIM), lambda bi, h, a: (bi, a, h)),
                  pl.BlockSpec((None, s, HEAD_DIM), lambda bi, h, a: (bi, 0, n_heads + h)),
                  pl.BlockSpec((None, s, HEAD_DIM), lambda bi, h, a: (bi, 0, 2 * n_heads + h)),
                  pl.BlockSpec((None, l, HEAD_DIM), lambda bi, h, a: (bi, 0, n_heads + h)),
                  pl.BlockSpec((None, l, HEAD_DIM), lambda bi, h, a: (bi, 0, 2 * n_heads + h))],
        out_specs=pl.BlockSpec((None, nq, HEAD_DIM), lambda bi, h, a: (bi, a, h)),
        scratch_shapes=[pltpu.VMEM((NA_DR, GRID_W, LANES), F32),
                        pltpu.VMEM((NA_PAIR_TILES, GRID_W, LANES), F32),
                        pltpu.VMEM((nq, nb + l), BF16),
                        pltpu.VMEM((s, 2 * HEAD_DIM), BF16),
                        pltpu.VMEM((l, 2 * HEAD_DIM), BF16)])
    vmem = 6 * (s + l) * HEAD_DIM * 2 + 4 * nq * (nb + l) * 4 + 2 * 1024 * 1024
    return pl.pallas_call(
        functools.partial(_na_kernel, rows=rows), grid_spec=grid_spec,
        out_shape=jax.ShapeDtypeStruct((b, s, n_heads * HEAD_DIM), BF16),
        compiler_params=_params(("arbitrary", "arbitrary", "arbitrary"), vmem),
        name="neighborhood_attention",
    )(rel_bias.reshape(-1), qkv_x, qkv_x, qkv_x, qkv_c, qkv_c)


def _oproj_kernel(o_ref, w_ref, x_ref, g_ref, out_ref, wb_ref):
    @pl.when((pl.program_id(1) == 0) & (pl.program_id(2) == 0))
    def _():
        wb_ref[...] = w_ref[...].astype(BF16)

    y = jnp.dot(o_ref[...], wb_ref[...], preferred_element_type=F32)
    out_ref[...] = x_ref[...] + g_ref[...] * y


def _out_project(o, w, x, gate, *, tm):
    b, t, k = o.shape
    d = w.shape[1]
    tn = 512
    vmem = 2 * tm * k * 2 + 2 * k * tn * 4 + k * tn * 2 + 6 * tm * tn * 4
    return pl.pallas_call(
        _oproj_kernel,
        grid=(d // tn, b, t // tm),
        in_specs=[pl.BlockSpec((None, tm, k), lambda j, bi, m: (bi, m, 0)),
                  pl.BlockSpec((k, tn), lambda j, bi, m: (0, j)),
                  pl.BlockSpec((None, tm, tn), lambda j, bi, m: (bi, m, j)),
                  pl.BlockSpec((None, 1, tn), lambda j, bi, m: (bi, 0, j))],
        out_specs=pl.BlockSpec((None, tm, tn), lambda j, bi, m: (bi, m, j)),
        scratch_shapes=[pltpu.VMEM((k, tn), BF16)],
        out_shape=jax.ShapeDtypeStruct((b, t, d), F32),
        compiler_params=_params(("arbitrary", "arbitrary", "arbitrary"), vmem),
        name="out_project",
    )(o, w, x, gate)


ROUTER_BLOCK = 256


def _router_kernel(x_ref, sh_ref, sc_ref, g_ref, rw_ref, rb_ref, cnt_in_ref,
                   hp_ref, idx_ref, wgt_ref, rank_ref, cnt_ref, run_ref):
    i = pl.program_id(0)

    @pl.when(i == 0)
    def _():
        run_ref[...] = cnt_in_ref[...]

    h = _norm_modulate(x_ref[...], g_ref[...], sh_ref[...], sc_ref[...])
    tb, d = h.shape
    packed = pltpu.pack_elementwise([h[:, :d // 2], h[:, d // 2:]], packed_dtype=BF16)
    for k in range(SUBLANES):
        hp_ref[pl.ds(k, tb, stride=SUBLANES), :] = packed[:, k * LANES:(k + 1) * LANES]

    h_hi = h.astype(BF16)
    h_lo = (h - h_hi.astype(F32)).astype(BF16)
    hi_terms = jnp.dot(h_hi, rw_ref[...], preferred_element_type=F32)
    logits = (hi_terms[:, :LANES]
              + (hi_terms[:, LANES:] + jnp.dot(h_lo, rw_ref[:, :LANES], preferred_element_type=F32))
              + rb_ref[...])
    lane = lax.broadcasted_iota(I32, (tb, LANES), 1)
    lane_f = lane.astype(F32)
    work = logits
    idx_acc = jnp.zeros((tb, LANES), F32)
    val_acc = jnp.full((tb, LANES), NEG_INF, F32)
    hots = []
    for k in range(TOP_K):
        m = jnp.max(work, axis=-1, keepdims=True)
        idx = jnp.min(jnp.where(work == m, lane_f, float(LANES)), axis=-1, keepdims=True)
        hot = lane_f == idx
        hots.append(hot)
        idx_acc = jnp.where(lane == k, idx, idx_acc)
        val_acc = jnp.where(lane == k, m, val_acc)
        work = jnp.where(hot, NEG_INF, work)

    e = jnp.exp(val_acc - jnp.max(val_acc, axis=-1, keepdims=True))
    wgt_ref[...] = e / jnp.sum(e, axis=-1, keepdims=True)
    idx_ref[...] = jnp.transpose(idx_acc)[:SUBLANES, :].astype(I32)

    chosen = jnp.zeros((tb, LANES), F32)
    for hot in hots:
        chosen = chosen + jnp.where(hot, 1.0, 0.0)
    row = lax.broadcasted_iota(I32, (tb, tb), 0)
    col = lax.broadcasted_iota(I32, (tb, tb), 1)
    earlier = jnp.where(col < row, 1.0, 0.0).astype(BF16)
    before = jnp.dot(earlier, chosen.astype(BF16), preferred_element_type=F32) + run_ref[...]
    rank_acc = jnp.zeros((tb, LANES), F32)
    for k, hot in enumerate(hots):
        rk = jnp.sum(jnp.where(hot, before, 0.0), axis=-1, keepdims=True)
        rank_acc = jnp.where(lane == k, rk, rank_acc)
    rank_ref[...] = jnp.transpose(rank_acc)[:SUBLANES, :].astype(I32)
    run_ref[...] = run_ref[...] + jnp.sum(chosen, axis=0, keepdims=True)
    cnt_ref[...] = run_ref[...]


def _router(x2d, shift, scale, g, rw_pad, rb_pad, counts_in, *, rows_per_mod):
    n, d = x2d.shape
    tb = ROUTER_BLOCK
    per = rows_per_mod // tb
    vmem = 2 * tb * d * 4 + 6 * tb * d * 4 + 2 * d * LANES * 4
    outs = pl.pallas_call(
        _router_kernel,
        grid=(n // tb,),
        in_specs=[pl.BlockSpec((tb, d), lambda i: (i, 0)),
                  pl.BlockSpec((None, 1, d), lambda i: (i // per, 0, 0)),
                  pl.BlockSpec((None, 1, d), lambda i: (i // per, 0, 0)),
                  pl.BlockSpec((1, d), lambda i: (0, 0)),
                  pl.BlockSpec((d, 2 * LANES), lambda i: (0, 0)),
                  pl.BlockSpec((1, LANES), lambda i: (0, 0)),
                  pl.BlockSpec((1, LANES), lambda i: (0, 0))],
        out_specs=[pl.BlockSpec((tb * SUBLANES, LANES), lambda i: (i, 0)),
                   pl.BlockSpec((SUBLANES, tb), lambda i: (0, i)),
                   pl.BlockSpec((tb, LANES), lambda i: (i, 0)),
                   pl.BlockSpec((SUBLANES, tb), lambda i: (0, i)),
                   pl.BlockSpec((1, LANES), lambda i: (0, 0))],
        out_shape=[jax.ShapeDtypeStruct((n * SUBLANES, LANES), jnp.uint32),
                   jax.ShapeDtypeStruct((SUBLANES, n), I32),
                   jax.ShapeDtypeStruct((n, LANES), F32),
                   jax.ShapeDtypeStruct((SUBLANES, n), I32),
                   jax.ShapeDtypeStruct((1, LANES), F32)],
        scratch_shapes=[pltpu.VMEM((1, LANES), F32)],
        compiler_params=_params(("arbitrary",), vmem),
        name="router",
    )(x2d, shift, scale, g, rw_pad, rb_pad, counts_in)
    return outs


def _dispatch_kernel(fill_ref, slot_ref, hp_ref, xs_ref, zero_ref, sem, fill_sem):
    tb = hp_ref.shape[0] // SUBLANES
    t_rows = zero_ref.shape[0]

    @pl.when(pl.program_id(0) == 0)
    def _():
        zero_ref[...] = jnp.zeros_like(zero_ref)

        def tile_copy(t):
            return pltpu.make_async_copy(zero_ref, xs_ref.at[pl.ds(t * t_rows, t_rows), :], fill_sem)

        def fill(t, carry):
            @pl.when(fill_ref[t] == 1)
            def _():
                tile_copy(t).start()
            return carry

        def fill_wait(t, carry):
            @pl.when(fill_ref[t] == 1)
            def _():
                tile_copy(t).wait()
            return carry

        lax.fori_loop(0, fill_ref.shape[0], fill, 0)
        lax.fori_loop(0, fill_ref.shape[0], fill_wait, 0)

    def row_copy(j, s):
        src = j * SUBLANES if isinstance(j, int) else pl.multiple_of(j * SUBLANES, SUBLANES)
        dst = s * SUBLANES if isinstance(s, int) else pl.multiple_of(s * SUBLANES, SUBLANES)
        return pltpu.make_async_copy(hp_ref.at[pl.ds(src, SUBLANES), :],
                                     xs_ref.at[pl.ds(dst, SUBLANES), :], sem)

    def issue(j, carry):
        for k in range(TOP_K):
            row_copy(j, slot_ref[k, j]).start()
        return carry

    lax.fori_loop(0, tb, issue, 0)

    for k in range(TOP_K):
        pltpu.make_async_copy(hp_ref, xs_ref.at[pl.ds(0, tb * SUBLANES), :], sem).wait()


def _dispatch(fill_tiles, slots_t, hp, n_slots):
    n = hp.shape[0] // SUBLANES
    tb = ROUTER_BLOCK
    grid_spec = pltpu.PrefetchScalarGridSpec(
        num_scalar_prefetch=1,
        grid=(n // tb,),
        in_specs=[pl.BlockSpec((TOP_K, tb), lambda i, f: (0, i), memory_space=pltpu.SMEM),
                  pl.BlockSpec((tb * SUBLANES, LANES), lambda i, f: (i, 0))],
        out_specs=pl.BlockSpec(memory_space=pl.ANY),
        scratch_shapes=[pltpu.VMEM((MOE_TILE * SUBLANES, LANES), jnp.uint32),
                        pltpu.SemaphoreType.DMA(()), pltpu.SemaphoreType.DMA(())])
    return pl.pallas_call(
        _dispatch_kernel, grid_spec=grid_spec,
        out_shape=jax.ShapeDtypeStruct((n_slots * SUBLANES, LANES), jnp.uint32),
        compiler_params=_params(("arbitrary",), (2 * tb + MOE_TILE) * SUBLANES * LANES * 4),
        name="dispatch",
    )(fill_tiles, slots_t, hp)


EXPERT_TILES = 10
HIDDEN_CHUNK = 512


def _swiglu_pairs(a):
    even = (lax.broadcasted_iota(I32, (1, LANES), 1) % 2) == 0
    outs = []
    for c in range(a.shape[1] // (2 * LANES)):
        ga = a[:, (2 * c) * LANES:(2 * c + 1) * LANES]
        gb = a[:, (2 * c + 1) * LANES:(2 * c + 2) * LANES]
        glu = jnp.where(even, ga, pltpu.roll(gb, 1, 1))
        lin = jnp.where(even, pltpu.roll(ga, LANES - 1, 1), gb)
        glu = jnp.minimum(glu, SWIGLU_LIMIT)
        lin = jnp.clip(lin, -SWIGLU_LIMIT, SWIGLU_LIMIT)
        outs.append((glu * jax.nn.sigmoid(SWIGLU_ALPHA * glu) * (lin + 1.0)).astype(BF16))
    return outs[0] if len(outs) == 1 else jnp.concatenate(outs, axis=1)


def _interleaved_rows_bf16(w_ref, dst_ref, row0=0):
    half = LANES // 2
    for qd in range(w_ref.shape[0] // LANES):
        wa = w_ref[qd * LANES:qd * LANES + half, :]
        wc = w_ref[qd * LANES + half:(qd + 1) * LANES, :]
        packed = pltpu.pack_elementwise([wa, wc], packed_dtype=BF16)
        dst_ref[row0 + qd * LANES:row0 + (qd + 1) * LANES, :] = pltpu.bitcast(packed, BF16)


def _expert_ffn_kernel(ie_ref, row0_ref, nt_ref, zt_ref, xs_ref, w1a_ref, w1c_ref, b1_ref, w2a_ref, w2c_ref,
                       b2_ref, y_ref,
                       xb_ref, stage_ref, ybuf_ref, w1b_ref, w2b_ref, pend_ref,
                       sem_in, sem_out, *, n_chunks, n_items):
    del ie_ref
    i = pl.program_id(0)
    c = pl.program_id(1)
    t_rows = MOE_TILE
    nt = nt_ref[i]
    row0 = row0_ref[i]

    def out_copy(src_ref, r_src, r_dst):
        dst = r_dst if isinstance(r_dst, int) else pl.multiple_of(r_dst, t_rows)
        return pltpu.make_async_copy(src_ref.at[pl.ds(r_src, t_rows), :],
                                     y_ref.at[pl.ds(dst, t_rows), :], sem_out)

    def wait_pending():
        def body(_, carry):
            out_copy(ybuf_ref, 0, 0).wait()
            return carry
        lax.fori_loop(0, pend_ref[0], body, 0)
        pend_ref[0] = 0

    @pl.when((i == 0) & (c == 0))
    def _():
        pend_ref[0] = 0

    @pl.when(c == 0)
    def _():
        wait_pending()

        def in_copy(t, slot):
            src = pl.multiple_of((row0 + t * t_rows) * SUBLANES, t_rows * SUBLANES)
            return pltpu.make_async_copy(xs_ref.at[pl.ds(src, t_rows * SUBLANES), :],
                                         stage_ref.at[slot], sem_in.at[slot])

        @pl.when(nt > 0)
        def _():
            in_copy(0, 0).start()

        def load(t, carry):
            slot = t % 2
            in_copy(t, slot).wait()

            @pl.when(t + 1 < nt)
            def _():
                in_copy(t + 1, 1 - slot).start()

            r = pl.multiple_of(t * t_rows, t_rows)
            half = xb_ref.shape[1] // 2
            for k in range(SUBLANES):
                xw = stage_ref[slot, pl.ds(k, t_rows, stride=SUBLANES), :]
                lo = pltpu.unpack_elementwise(xw, index=0, packed_dtype=BF16, unpacked_dtype=F32)
                hi = pltpu.unpack_elementwise(xw, index=1, packed_dtype=BF16, unpacked_dtype=F32)
                xb_ref[pl.ds(r, t_rows), k * LANES:(k + 1) * LANES] = lo.astype(BF16)
                xb_ref[pl.ds(r, t_rows), half + k * LANES:half + (k + 1) * LANES] = hi.astype(BF16)
            return carry

        lax.fori_loop(0, nt, load, 0)

        def zfill(t, carry):
            out_copy(ybuf_ref, 0, row0 + t * t_rows).start()
            return carry

        @pl.when(zt_ref[i] > 0)
        def _():
            ybuf_ref[0:t_rows, :] = jnp.zeros((t_rows, ybuf_ref.shape[1]), F32)

        lax.fori_loop(0, zt_ref[i], zfill, 0)
        pend_ref[0] = zt_ref[i]

    @pl.when(nt > 0)
    def _():
        k_half = w1a_ref.shape[0]
        w1b_ref[:k_half, :] = w1a_ref[...].astype(BF16)
        w1b_ref[k_half:, :] = w1c_ref[...].astype(BF16)
        _interleaved_rows_bf16(w2a_ref, w2b_ref)
        _interleaved_rows_bf16(w2c_ref, w2b_ref, row0=w2a_ref.shape[0])
        last = c == n_chunks - 1

        def run(first):
            def rows_step(r, n_tiles):
                n_rows = n_tiles * t_rows
                xb = xb_ref[pl.ds(r, n_rows), :]
                a = jnp.dot(xb, w1b_ref[...], preferred_element_type=F32) + b1_ref[...]
                act = _swiglu_pairs(a)
                part = jnp.dot(act, w2b_ref[...], preferred_element_type=F32)
                if first:
                    ybuf_ref[pl.ds(r, n_rows), :] = part + b2_ref[...]
                else:
                    ybuf_ref[pl.ds(r, n_rows), :] += part

                if not first or n_chunks == 1:
                    @pl.when(last)
                    def _():
                        for k in range(n_tiles):
                            out_copy(ybuf_ref, r + k * t_rows, row0 + r + k * t_rows).start()

            def quad(p, carry):
                rows_step(pl.multiple_of(p * (4 * t_rows), 4 * t_rows), 4)
                return carry

            lax.fori_loop(0, nt // 4, quad, 0)

            @pl.when(nt % 4 >= 2)
            def _():
                rows_step(pl.multiple_of((nt // 4) * (4 * t_rows), 2 * t_rows), 2)

            @pl.when(nt % 2 == 1)
            def _():
                rows_step(pl.multiple_of((nt - 1) * t_rows, t_rows), 1)

        @pl.when(c == 0)
        def _():
            run(True)

        @pl.when(c != 0)
        def _():
            run(False)

        @pl.when(last)
        def _():
            pend_ref[0] = nt

    @pl.when((i == n_items - 1) & (c == n_chunks - 1))
    def _():
        wait_pending()


def _expert_ffn(item_tabs, xs, w1, b1, w2, b2, layer, max_items):
    p = xs.shape[0] // SUBLANES
    half = w1.shape[2] // 2
    _, e, d, n = w1.shape
    hc = HIDDEN_CHUNK
    n_chunks = (n // 2) // hc
    rows = EXPERT_TILES * MOE_TILE

    def chunk(c, nt, i):
        return jnp.where(nt[i] > 0, c, n_chunks - 1)

    grid_spec = pltpu.PrefetchScalarGridSpec(
        num_scalar_prefetch=4,
        grid=(max_items, n_chunks),
        in_specs=[pl.BlockSpec(memory_space=pl.ANY),
                  pl.BlockSpec((None, None, d // 2, 2 * hc), lambda i, c, ie, r0, nt, zt: (layer, ie[i], 0, chunk(c, nt, i))),
                  pl.BlockSpec((None, None, d // 2, 2 * hc), lambda i, c, ie, r0, nt, zt: (layer, ie[i], 1, chunk(c, nt, i))),
                  pl.BlockSpec((None, None, 1, 2 * hc), lambda i, c, ie, r0, nt, zt: (layer, ie[i], 0, chunk(c, nt, i))),
                  pl.BlockSpec((None, None, hc // 2, d), lambda i, c, ie, r0, nt, zt: (layer, ie[i], 2 * chunk(c, nt, i), 0)),
                  pl.BlockSpec((None, None, hc // 2, d), lambda i, c, ie, r0, nt, zt: (layer, ie[i], 2 * chunk(c, nt, i) + 1, 0)),
                  pl.BlockSpec((None, None, 1, d), lambda i, c, ie, r0, nt, zt: (layer, ie[i], 0, 0))],
        out_specs=pl.BlockSpec(memory_space=pl.ANY),
        scratch_shapes=[pltpu.VMEM((rows, d), BF16),
                        pltpu.VMEM((2, MOE_TILE * SUBLANES, LANES), jnp.uint32),
                        pltpu.VMEM((rows, d), F32),
                        pltpu.VMEM((d, 2 * hc), BF16),
                        pltpu.VMEM((hc, d), BF16),
                        pltpu.SMEM((1,), I32),
                        pltpu.SemaphoreType.DMA((2,)),
                        pltpu.SemaphoreType.DMA(())])
    vmem = (rows * d * 6 + 2 * MOE_TILE * half * 4 + MOE_TILE * d * 4
            + 3 * d * 2 * hc * 4 + d * 2 * hc * 2 + 3 * hc * d * 4 + hc * d * 2
            + 8 * MOE_TILE * 2 * hc * 4 + 4 * MOE_TILE * d * 4)
    depth = w1.shape[0]
    return pl.pallas_call(
        functools.partial(_expert_ffn_kernel, n_chunks=n_chunks, n_items=max_items), grid_spec=grid_spec,
        out_shape=jax.ShapeDtypeStruct((p, d), F32),
        compiler_params=_params(("arbitrary", "arbitrary"), vmem),
        name="expert_ffn",
    )(*item_tabs, xs, w1, w1, b1.reshape(depth, e, 1, n), w2, w2, b2.reshape(depth, e, 1, d))


COMBINE_BLOCK = 128


def _combine_kernel(slot_ref, slot_next_ref, y_ref, w_ref, x_ref, g_ref, o_ref, buf, sem, *, nblk):
    i = pl.program_id(0)
    tb = x_ref.shape[0]

    def row_copy(s, b, k, j):
        return pltpu.make_async_copy(y_ref.at[pl.ds(s, 1), :], buf.at[b, k, pl.ds(j, 1), :], sem.at[b])

    def issue(slots, b):
        def body(j8, carry):
            for jj in range(SUBLANES):
                j = j8 * SUBLANES + jj
                for k in range(TOP_K):
                    row_copy(slots[k, j], b, k, j).start()
            return carry
        lax.fori_loop(0, tb // SUBLANES, body, 0)

    @pl.when(i == 0)
    def _():
        issue(slot_ref, 0)

    @pl.when(i + 1 < nblk)
    def _():
        issue(slot_next_ref, (i + 1) % 2)

    cur = i % 2
    for k in range(TOP_K):
        pltpu.make_async_copy(y_ref.at[pl.ds(0, tb), :], buf.at[cur, k], sem.at[cur]).wait()

    w = w_ref[...]
    acc = w[:, 0:1] * buf[cur, 0]
    for k in range(1, TOP_K):
        acc = acc + w[:, k:k + 1] * buf[cur, k]
    o_ref[...] = x_ref[...] + g_ref[...] * acc


def _combine(slots_t, y, gates, x2d, gate_mod, *, rows_per_mod):
    n, d = x2d.shape
    tb = COMBINE_BLOCK
    nblk = n // tb
    per = rows_per_mod // tb
    vmem = 2 * TOP_K * tb * d * 4 + 6 * tb * d * 4
    return pl.pallas_call(
        functools.partial(_combine_kernel, nblk=nblk),
        grid=(nblk,),
        in_specs=[pl.BlockSpec((TOP_K, tb), lambda i: (0, i), memory_space=pltpu.SMEM),
                  pl.BlockSpec((TOP_K, tb), lambda i: (0, jnp.minimum(i + 1, nblk - 1)),
                               memory_space=pltpu.SMEM),
                  pl.BlockSpec(memory_space=pl.ANY),
                  pl.BlockSpec((tb, LANES), lambda i: (i, 0)),
                  pl.BlockSpec((tb, d), lambda i: (i, 0)),
                  pl.BlockSpec((None, 1, d), lambda i: (i // per, 0, 0))],
        out_specs=pl.BlockSpec((tb, d), lambda i: (i, 0)),
        out_shape=jax.ShapeDtypeStruct((n, d), F32),
        scratch_shapes=[pltpu.VMEM((2, TOP_K, tb, d), F32), pltpu.SemaphoreType.DMA((2,))],
        compiler_params=_params(("arbitrary",), vmem),
        name="combine",
    )(slots_t, slots_t, y, gates, x2d, gate_mod)


def _owner(ends, idx):
    return jnp.minimum(jnp.sum((ends[None, :] <= idx[:, None]).astype(I32), axis=1), ends.shape[0] - 1)


def _schedule(counts, max_tiles, max_items):
    t_rows = MOE_TILE
    tiles_e = (counts + t_rows - 1) // t_rows
    tile_end = jnp.cumsum(tiles_e)
    tile_start = tile_end - tiles_e
    num_tiles = tile_end[-1]
    t = jnp.arange(max_tiles, dtype=I32)
    te = _owner(tile_end, jnp.minimum(t, num_tiles - 1))
    fill = jnp.where((t < num_tiles) & (t != tile_end[te] - 1), 0, 1)

    items_e = (tiles_e + EXPERT_TILES - 1) // EXPERT_TILES
    item_end = jnp.cumsum(items_e)
    item_start = item_end - items_e
    n_items = item_end[-1]
    i = jnp.arange(max_items, dtype=I32)
    live = i < n_items
    ie = _owner(item_end, jnp.minimum(i, n_items - 1))
    local = jnp.minimum(i, n_items - 1) - item_start[ie]
    nt = jnp.where(live, jnp.clip(tiles_e[ie] - local * EXPERT_TILES, 0, EXPERT_TILES), 0)
    dead_tile0 = num_tiles + (i - n_items) * EXPERT_TILES
    zt = jnp.where(live, 0, jnp.clip(max_tiles - dead_tile0, 0, EXPERT_TILES))
    row0 = jnp.where(live, (tile_start[ie] + local * EXPERT_TILES) * t_rows,
                     jnp.clip(dead_tile0, 0, max_tiles - 1) * t_rows)
    return ((tile_start * t_rows).astype(I32), fill.astype(I32),
            (ie.astype(I32), row0.astype(I32), nt.astype(I32), zt.astype(I32)))


def _moe(parts, shift2, scale2, gate2, norm_g, router_w, router_b, w1, b1, w2, b2, layer):
    d = router_w.shape[0]
    rw_f32 = jnp.zeros((d, LANES), F32).at[:, :N_EXPERTS].set(router_w)
    rw_hi = lax.bitcast_convert_type(
        lax.bitcast_convert_type(rw_f32, jnp.uint32) & jnp.uint32(0xFFFF0000), F32)
    rw_pad = jnp.concatenate([rw_hi.astype(BF16), (rw_f32 - rw_hi).astype(BF16)], axis=1)
    rb_pad = jnp.full((1, LANES), NEG_INF, F32).at[0, :N_EXPERTS].set(router_b)
    counts = jnp.zeros((1, LANES), F32)
    routed = []
    for x2d, mod_rows, per in parts:
        hp, idx, wgt, rank, counts = _router(
            x2d, shift2[mod_rows][:, None, :], scale2[mod_rows][:, None, :], norm_g[None, :],
            rw_pad, rb_pad, counts, rows_per_mod=per)
        routed.append((hp, idx[:TOP_K], wgt, rank[:TOP_K]))
    n_total = sum(x2d.shape[0] for x2d, _, _ in parts)
    max_tiles = (n_total * TOP_K) // MOE_TILE + N_EXPERTS
    n_slots = max_tiles * MOE_TILE
    max_items = N_EXPERTS + -(-max_tiles // EXPERT_TILES)
    group_start, fill_tiles, tabs = _schedule(counts[0, :N_EXPERTS].astype(I32), max_tiles, max_items)

    slots = [group_start[idx] + rank for _, idx, _, rank in routed]
    hp_all = jnp.concatenate([r[0] for r in routed], axis=0) if len(routed) > 1 else routed[0][0]
    slots_all = jnp.concatenate(slots, axis=1) if len(slots) > 1 else slots[0]
    xs = _dispatch(fill_tiles, slots_all, hp_all, n_slots)
    y = _expert_ffn(tabs, xs, w1, b1, w2, b2, layer, max_items)
    outs = []
    for (x2d, mod_rows, per), sl, (_, _, wgt, _) in zip(parts, slots, routed):
        outs.append(_combine(sl, y, wgt, x2d, gate2[mod_rows][:, None, :], rows_per_mod=per))
    return outs


def _rope_tables(seq):
    t = np.arange(seq)
    m = HEAD_DIM // 4
    inv_freq = ROPE_THETA ** (-np.arange(m, dtype=np.float64) / m)
    ang_r = (t // GRID_W)[:, None] * inv_freq[None, :]
    ang_c = (t % GRID_W)[:, None] * inv_freq[None, :]
    cos = np.concatenate([np.cos(ang_r)] * 2 + [np.cos(ang_c)] * 2, axis=-1)
    sin = np.concatenate([-np.sin(ang_r), np.sin(ang_r), -np.sin(ang_c), np.sin(ang_c)], axis=-1)
    return jnp.asarray(cos, F32), jnp.asarray(sin, F32)


def kernel(x, c, ctx, c_ctx, ada_w, ada_b, norm_mix_g, norm_ffn_g, a_w_qkv, a_w_o, a_q_gain, a_k_gain,
           b_w_qkv, b_w_o, b_q_gain, b_k_gain, b_rel_bias, router_w, router_b, exp_w1, exp_b1, exp_w2, exp_b2):
    bsz, seq, d = x.shape
    l_ctx = ctx.shape[1]
    depth = ada_w.shape[0]
    n_heads = d // HEAD_DIM
    q_scale = HEAD_DIM ** -0.5 * LOG2E

    c_rows = jnp.zeros((SUBLANES, d), F32).at[:bsz].set(c).at[bsz].set(c_ctx)
    mod = _ada_modulation(c_rows, ada_w, ada_b)
    cos, sin = _rope_tables(seq)
    ones_tab = jnp.ones((l_ctx, HEAD_DIM), F32)
    x_rows = jnp.arange(bsz)
    c_rows_idx = jnp.full((bsz,), bsz)

    for i in range(depth):
        last = i == depth - 1
        sh1, sc1, g1, sh2, sc2, g2 = [mod[i, :, k * d:(k + 1) * d] for k in range(N_MOD)]
        j = i // 2
        if i % 2 == 0:
            w_qkv, w_o, qg, kg = a_w_qkv[j], a_w_o[j], a_q_gain[j], a_k_gain[j]
            n_q, n_k, n_v = n_heads, A_KV_HEADS, A_KV_HEADS
        else:
            w_qkv, w_o, qg, kg = b_w_qkv[j], b_w_o[j], b_q_gain[j], b_k_gain[j]
            n_q, n_k, n_v = n_heads, n_heads, n_heads
        heads_per_tile = 512 // HEAD_DIM
        gain_cols = jnp.concatenate([jnp.tile(qg * q_scale, n_q), jnp.tile(kg, n_k),
                                     jnp.ones((n_v * HEAD_DIM,), F32)])[None, :]
        flags = jnp.concatenate([jnp.ones(((n_q + n_k) // heads_per_tile,), I32),
                                 jnp.zeros((n_v // heads_per_tile,), I32)])
        rope = i % 2 == 0
        qkv_x = _qkv_project(x, sh1[x_rows][:, None, :], sc1[x_rows][:, None, :], norm_mix_g[i][None, :],
                             w_qkv, gain_cols, flags, cos, sin, rope=rope, tm=1024)
        qkv_c = _qkv_project(ctx, sh1[c_rows_idx][:, None, :], sc1[c_rows_idx][:, None, :],
                             norm_mix_g[i][None, :], w_qkv, gain_cols, flags, ones_tab, ones_tab,
                             rope=False, tm=l_ctx)
        if i % 2 == 0:
            ox = _attention(qkv_x, qkv_c, qkv_x, n_q_heads=n_q, n_kv_heads=n_k,
                            k_col0=n_q, v_col0=n_q + n_k, tq=256)
        else:
            ox = _neighborhood_attention(qkv_x, qkv_c, b_rel_bias[j], n_heads=n_heads)
        x = _out_project(ox, w_o, x, g1[x_rows][:, None, :], tm=1024)
        parts = [(x.reshape(bsz * seq, d), x_rows, seq)]
        if not last:
            oc = _attention(qkv_c, qkv_c, None, n_q_heads=n_q, n_kv_heads=n_k,
                            k_col0=n_q, v_col0=n_q + n_k, tq=l_ctx)
            ctx = _out_project(oc, w_o, ctx, g1[c_rows_idx][:, None, :], tm=l_ctx)
            parts.append((ctx.reshape(bsz * l_ctx, d), jnp.full((1,), bsz), bsz * l_ctx))
        outs = _moe(parts, sh2, sc2, g2, norm_ffn_g[i], router_w[i], router_b[i],
                    exp_w1, exp_b1, exp_w2, exp_b2, i)
        x = outs[0].reshape(bsz, seq, d)
        if not last:
            ctx = outs[1].reshape(bsz, l_ctx, d)
    return x
```

```python
import functools
import math

import jax
import jax.numpy as jnp
import numpy as np
from jax import lax
from jax.experimental import pallas as pl
from jax.experimental.pallas import tpu as pltpu

F32 = jnp.float32
BF16 = jnp.bfloat16
I32 = jnp.int32

LANES = 128
SUBLANES = 8
VMEM_BYTES_V7X = 64 * 1024 * 1024

HEAD_DIM = 128
GRID_W = 64
A_KV_HEADS = 4
NA_WIN_ROWS = 8
NA_WIN_COLS = 16
N_EXPERTS = 32
TOP_K = 4
ROPE_THETA = 10000.0
SWIGLU_ALPHA = 1.702
SWIGLU_LIMIT = 7.0
NORM_EPS = 1e-6
N_MOD = 6
LOG2E = math.log2(math.e)
NEG_INF = float("-inf")

MOE_TILE = 128


def _vmem_limit(nbytes):
    return int(min(nbytes + 8 * 1024 * 1024, VMEM_BYTES_V7X - 8 * 1024 * 1024))


def _params(semantics, vmem_bytes):
    return pltpu.CompilerParams(dimension_semantics=semantics,
                                vmem_limit_bytes=_vmem_limit(vmem_bytes))


def _ada_kernel(c_ref, w_ref, b_ref, o_ref):
    c = c_ref[...]
    a = (c * jax.nn.sigmoid(c)).astype(BF16)
    o_ref[0] = jnp.dot(a, w_ref[0].astype(BF16), preferred_element_type=F32) + b_ref[0]


def _ada_modulation(c_rows, ada_w, ada_b):
    depth, d, n = ada_w.shape
    rows = c_rows.shape[0]
    tn = 1024
    return pl.pallas_call(
        _ada_kernel,
        grid=(depth, n // tn),
        in_specs=[pl.BlockSpec((rows, d), lambda l, j: (0, 0)),
                  pl.BlockSpec((1, d, tn), lambda l, j: (l, 0, j)),
                  pl.BlockSpec((1, 1, tn), lambda l, j: (l, 0, j))],
        out_specs=pl.BlockSpec((1, rows, tn), lambda l, j: (l, 0, j)),
        out_shape=jax.ShapeDtypeStruct((depth, rows, n), F32),
        compiler_params=_params(("arbitrary", "arbitrary"), 2 * d * tn * 4 + d * tn * 2),
        name="ada_modulation",
    )(c_rows, ada_w, ada_b.reshape(depth, 1, n))


def _norm_modulate(x, g, shift, scale):
    ms = jnp.mean(x * x, axis=-1, keepdims=True)
    y = x * lax.rsqrt(ms + NORM_EPS) * g
    return y * (1.0 + scale) + shift


def _norm_mod_kernel(x_ref, sh_ref, sc_ref, g_ref, h_ref):
    h_ref[...] = _norm_modulate(x_ref[...], g_ref[...], sh_ref[...], sc_ref[...]).astype(BF16)


def _norm_mod(x, shift, scale, g, *, tm):
    b, t, d = x.shape
    return pl.pallas_call(
        _norm_mod_kernel,
        grid=(b, t // tm),
        in_specs=[pl.BlockSpec((None, tm, d), lambda bi, m: (bi, m, 0)),
                  pl.BlockSpec((None, 1, d), lambda bi, m: (bi, 0, 0)),
                  pl.BlockSpec((None, 1, d), lambda bi, m: (bi, 0, 0)),
                  pl.BlockSpec((1, d), lambda bi, m: (0, 0))],
        out_specs=pl.BlockSpec((None, tm, d), lambda bi, m: (bi, m, 0)),
        out_shape=jax.ShapeDtypeStruct((b, t, d), BF16),
        compiler_params=_params(("arbitrary", "arbitrary"), 2 * tm * d * 6 + 4 * tm * d * 4),
        name="norm_mod",
    )(x, shift, scale, g)


def _qkv_kernel(flag_ref, h_ref, w_ref, gain_ref, cos_ref, sin_ref, o_ref, wb_ref, *, rope):
    n = pl.program_id(0)

    @pl.when((pl.program_id(1) == 0) & (pl.program_id(2) == 0))
    def _():
        wb_ref[...] = w_ref[...].astype(BF16)

    tm, tn = o_ref.shape
    piece = min(tm, 256)
    pieces = [slice(r, r + piece) for r in range(0, tm, piece)]

    @pl.when(flag_ref[n] == 0)
    def _():
        for rs in pieces:
            o_ref[rs, :] = jnp.dot(h_ref[rs, :], wb_ref[...], preferred_element_type=F32).astype(BF16)

    @pl.when(flag_ref[n] == 1)
    def _():
        lane = lax.broadcasted_iota(I32, (1, HEAD_DIM), 1)
        first_half = (lane % (HEAD_DIM // 2)) < (HEAD_DIM // 4)
        for rs in pieces:
            a = jnp.dot(h_ref[rs, :], wb_ref[...], preferred_element_type=F32)
            for hh in range(tn // HEAD_DIM):
                sl = slice(hh * HEAD_DIM, (hh + 1) * HEAD_DIM)
                ah = a[:, sl]
                ms = jnp.mean(ah * ah, axis=-1, keepdims=True)
                yh = ah * lax.rsqrt(ms + NORM_EPS) * gain_ref[:, sl]
                if rope:
                    partner = jnp.where(first_half,
                                        pltpu.roll(yh, HEAD_DIM - HEAD_DIM // 4, 1),
                                        pltpu.roll(yh, HEAD_DIM // 4, 1))
                    yh = yh * cos_ref[rs, :] + partner * sin_ref[rs, :]
                o_ref[rs, sl] = yh.astype(BF16)


def _qkv_project(x, shift, scale, g, w, gain_cols, norm_flags, cos, sin, *, rope, tm):
    b, t, d = x.shape
    n = w.shape[1]
    tn = 512
    h = _norm_mod(x, shift, scale, g, tm=min(tm, 512))
    kernel = functools.partial(_qkv_kernel, rope=rope)
    grid_spec = pltpu.PrefetchScalarGridSpec(
        num_scalar_prefetch=1,
        grid=(n // tn, b, t // tm),
        in_specs=[pl.BlockSpec((None, tm, d), lambda j, bi, m, f: (bi, m, 0)),
                  pl.BlockSpec((d, tn), lambda j, bi, m, f: (0, j)),
                  pl.BlockSpec((1, tn), lambda j, bi, m, f: (0, j)),
                  pl.BlockSpec((tm, HEAD_DIM), lambda j, bi, m, f: (m, 0)),
                  pl.BlockSpec((tm, HEAD_DIM), lambda j, bi, m, f: (m, 0))],
        out_specs=pl.BlockSpec((None, tm, tn), lambda j, bi, m, f: (bi, m, j)),
        scratch_shapes=[pltpu.VMEM((d, tn), BF16)])
    vmem = 2 * tm * d * 2 + 2 * d * tn * 4 + d * tn * 2 + 6 * tm * tn * 4
    return pl.pallas_call(
        kernel, grid_spec=grid_spec,
        out_shape=jax.ShapeDtypeStruct((b, t, n), BF16),
        compiler_params=_params(("arbitrary", "arbitrary", "arbitrary"), vmem),
        name="qkv_project",
    )(norm_flags, h, w, gain_cols, cos, sin)


def _nt_dot(a, b):
    return lax.dot_general(a, b, (((1,), (1,)), ((), ())), preferred_element_type=F32)


def _values_with_ones(v_ref, dst_ref):
    dst_ref[:, :HEAD_DIM] = v_ref[...]
    dst_ref[:, HEAD_DIM:] = jnp.ones((v_ref.shape[0], HEAD_DIM), BF16)


def _attn_kernel(q_ref, kc_ref, vc_ref, *rest, group, has_x):
    if has_x:
        kx_ref, vx_ref, o_ref = rest
    else:
        (o_ref,) = rest
    kc = kc_ref[...]
    vc = vc_ref[...]
    for g in range(group):
        sl = slice(g * HEAD_DIM, (g + 1) * HEAD_DIM)
        q = q_ref[:, sl]
        sc = _nt_dot(q, kc)
        m = jnp.max(sc, axis=-1, keepdims=True)
        if has_x:
            sx = _nt_dot(q, kx_ref[...])
            m = jnp.maximum(m, jnp.max(sx, axis=-1, keepdims=True))
        pc = jnp.exp2(sc - m)
        l = jnp.sum(pc, axis=-1, keepdims=True)
        acc = jnp.dot(pc.astype(BF16), vc, preferred_element_type=F32)
        if has_x:
            px = jnp.exp2(sx - m)
            l = l + jnp.sum(px, axis=-1, keepdims=True)
            acc = acc + jnp.dot(px.astype(BF16), vx_ref[...], preferred_element_type=F32)
        o_ref[:, sl] = (acc / l).astype(BF16)


def _attention(q_src, ctx_src, x_src, *, n_q_heads, n_kv_heads, k_col0, v_col0, tq):
    b, t_q, _ = q_src.shape
    l = ctx_src.shape[1]
    group = n_q_heads // n_kv_heads
    has_x = x_src is not None
    gw = group * HEAD_DIM
    in_specs = [pl.BlockSpec((None, tq, gw), lambda bi, kv, i: (bi, i, kv)),
                pl.BlockSpec((None, l, HEAD_DIM), lambda bi, kv, i: (bi, 0, k_col0 + kv)),
                pl.BlockSpec((None, l, HEAD_DIM), lambda bi, kv, i: (bi, 0, v_col0 + kv))]
    args = [q_src, ctx_src, ctx_src]
    s = 0
    if has_x:
        s = x_src.shape[1]
        in_specs += [pl.BlockSpec((None, s, HEAD_DIM), lambda bi, kv, i: (bi, 0, k_col0 + kv)),
                     pl.BlockSpec((None, s, HEAD_DIM), lambda bi, kv, i: (bi, 0, v_col0 + kv))]
        args += [x_src, x_src]
    vmem = 4 * (l + s) * HEAD_DIM * 2 + 4 * tq * gw * 2 + 4 * tq * (l + s) * 4
    return pl.pallas_call(
        functools.partial(_attn_kernel, group=group, has_x=has_x),
        grid=(b, n_kv_heads, t_q // tq),
        in_specs=in_specs,
        out_specs=pl.BlockSpec((None, tq, gw), lambda bi, kv, i: (bi, i, kv)),
        out_shape=jax.ShapeDtypeStruct((b, t_q, n_q_heads * HEAD_DIM), BF16),
        compiler_params=_params(("arbitrary", "arbitrary", "arbitrary"), vmem),
        name="attention_x" if has_x else "attention_ctx",
    )(*args)


NA_Q_ROWS = 8
NA_BAND_ROWS = 2 * NA_WIN_ROWS
NA_DR = 2 * NA_WIN_ROWS - 1
NA_DC = 2 * NA_WIN_COLS - 1
NA_PAIR_TILES = 30


def _na_kernel(tab_ref, q_ref, k_ref, v_ref, kc_ref, vc_ref, o_ref, tile_ref, pair_ref, s_ref,
               v1_ref, vc1_ref, *, rows):
    h = pl.program_id(1)
    a = pl.program_id(2)
    nb = NA_BAND_ROWS * GRID_W

    @pl.when(a == 0)
    def _():
        _values_with_ones(v_ref, v1_ref)
        _values_with_ones(vc_ref, vc1_ref)
        qc = lax.broadcasted_iota(I32, (GRID_W, LANES), 0)
        kc = lax.broadcasted_iota(I32, (GRID_W, LANES), 1) % GRID_W
        dc = jnp.clip(kc - qc, -(NA_WIN_COLS - 1), NA_WIN_COLS - 1) + NA_WIN_COLS - 1
        c0 = jnp.clip(qc - NA_WIN_COLS // 2, 0, GRID_W - NA_WIN_COLS)
        col_ok = (kc >= c0) & (kc < c0 + NA_WIN_COLS)
        for dr in range(NA_DR):
            acc = jnp.zeros((GRID_W, LANES), F32)
            for d in range(NA_DC):
                acc = jnp.where(dc == d, tab_ref[h * (NA_DR * NA_DC) + dr * NA_DC + d] * LOG2E, acc)
            tile_ref[dr] = jnp.where(col_ok, acc, NEG_INF)
        left = lax.broadcasted_iota(I32, (GRID_W, LANES), 1) < GRID_W
        zero = jnp.zeros((GRID_W, LANES), F32)
        for p in range(NA_PAIR_TILES):
            lo = tile_ref[p - NA_WIN_ROWS] if 0 <= p - NA_WIN_ROWS < NA_DR else zero
            hi = tile_ref[p - NA_WIN_ROWS + 1] if 0 <= p - NA_WIN_ROWS + 1 < NA_DR else zero
            pair_ref[p] = jnp.where(left, lo, hi)

    wr = NA_WIN_ROWS
    kb0 = jnp.clip(a * NA_Q_ROWS - wr // 2, 0, rows - NA_BAND_ROWS)
    k_band = k_ref[pl.ds(pl.multiple_of(kb0 * GRID_W, 4 * GRID_W), nb), :]
    v_band = v1_ref[pl.ds(pl.multiple_of(kb0 * GRID_W, 4 * GRID_W), nb), :]
    q = q_ref[...]
    s_loc = _nt_dot(q, k_band)
    s_ctx = _nt_dot(q, kc_ref[...])

    band_row = lax.broadcasted_iota(I32, (1, nb), 1) // GRID_W
    for rq in range(NA_Q_ROWS):
        r = a * NA_Q_ROWS + rq
        r0 = jnp.clip(r - wr // 2, 0, rows - wr)
        lo = r0 - kb0
        row_mask = jnp.where((band_row >= lo) & (band_row < lo + wr), 0.0, NEG_INF)
        j0 = kb0 - r + 2 * wr - 1
        bias = jnp.concatenate([pair_ref[j0 + 2 * i] for i in range(NA_BAND_ROWS // 2)], axis=1)
        rs = slice(rq * GRID_W, (rq + 1) * GRID_W)
        sl = s_loc[rs, :] + bias + row_mask
        sc = s_ctx[rs, :]
        m = jnp.maximum(jnp.max(sl, axis=-1, keepdims=True), jnp.max(sc, axis=-1, keepdims=True))
        s_ref[rs, :nb] = jnp.exp2(sl - m).astype(BF16)
        s_ref[rs, nb:] = jnp.exp2(sc - m).astype(BF16)

    pb = s_ref[...]
    acc = (jnp.dot(pb[:, :nb], v_band, preferred_element_type=F32)
           + jnp.dot(pb[:, nb:], vc1_ref[...], preferred_element_type=F32))
    o_ref[...] = (acc[:, :HEAD_DIM] / acc[:, HEAD_DIM:]).astype(BF16)


def _neighborhood_attention(qkv_x, qkv_c, rel_bias, *, n_heads):
    b, s, _ = qkv_x.shape
    l = qkv_c.shape[1]
    rows = s // GRID_W
    nq = NA_Q_ROWS * GRID_W
    nb = NA_BAND_ROWS * GRID_W
    grid_spec = pltpu.PrefetchScalarGridSpec(
        num_scalar_prefetch=0,
        grid=(b, n_heads, rows // NA_Q_ROWS),
        in_specs=[pl.BlockSpec(memory_space=pltpu.SMEM),
                  pl.BlockSpec((None, nq, HEAD_DIM), lambda bi, h, a: (bi, a, h)),
                  pl.BlockSpec((None, s, HEAD_DIM), lambda bi, h, a: (bi, 0, n_heads + h)),
                  pl.BlockSpec((None, s, HEAD_DIM), lambda bi, h, a: (bi, 0, 2 * n_heads + h)),
                  pl.BlockSpec((None, l, HEAD_DIM), lambda bi, h, a: (bi, 0, n_heads + h)),
                  pl.BlockSpec((None, l, HEAD_DIM), lambda bi, h, a: (bi, 0, 2 * n_heads + h))],
        out_specs=pl.BlockSpec((None, nq, HEAD_DIM), lambda bi, h, a: (bi, a, h)),
        scratch_shapes=[pltpu.VMEM((NA_DR, GRID_W, LANES), F32),
                        pltpu.VMEM((NA_PAIR_TILES, GRID_W, LANES), F32),
                        pltpu.VMEM((nq, nb + l), BF16),
                        pltpu.VMEM((s, 2 * HEAD_DIM), BF16),
                        pltpu.VMEM((l, 2 * HEAD_DIM), BF16)])
    vmem = 6 * (s + l) * HEAD_DIM * 2 + 4 * nq * (nb + l) * 4 + 2 * 1024 * 1024
    return pl.pallas_call(
        functools.partial(_na_kernel, rows=rows), grid_spec=grid_spec,
        out_shape=jax.ShapeDtypeStruct((b, s, n_heads * HEAD_DIM), BF16),
        compiler_params=_params(("arbitrary", "arbitrary", "arbitrary"), vmem),
        name="neighborhood_attention",
    )(rel_bias.reshape(-1), qkv_x, qkv_x, qkv_x, qkv_c, qkv_c)


def _oproj_kernel(o_ref, w_ref, x_ref, g_ref, out_ref, wb_ref):
    @pl.when((pl.program_id(1) == 0) & (pl.program_id(2) == 0))
    def _():
        wb_ref[...] = w_ref[...].astype(BF16)

    y = jnp.dot(o_ref[...], wb_ref[...], preferred_element_type=F32)
    out_ref[...] = x_ref[...] + g_ref[...] * y


def _out_project(o, w, x, gate, *, tm):
    b, t, k = o.shape
    d = w.shape[1]
    tn = 512
    vmem = 2 * tm * k * 2 + 2 * k * tn * 4 + k * tn * 2 + 6 * tm * tn * 4
    return pl.pallas_call(
        _oproj_kernel,
        grid=(d // tn, b, t // tm),
        in_specs=[pl.BlockSpec((None, tm, k), lambda j, bi, m: (bi, m, 0)),
                  pl.BlockSpec((k, tn), lambda j, bi, m: (0, j)),
                  pl.BlockSpec((None, tm, tn), lambda j, bi, m: (bi, m, j)),
                  pl.BlockSpec((None, 1, tn), lambda j, bi, m: (bi, 0, j))],
        out_specs=pl.BlockSpec((None, tm, tn), lambda j, bi, m: (bi, m, j)),
        scratch_shapes=[pltpu.VMEM((k, tn), BF16)],
        out_shape=jax.ShapeDtypeStruct((b, t, d), F32),
        compiler_params=_params(("arbitrary", "arbitrary", "arbitrary"), vmem),
        name="out_project",
    )(o, w, x, gate)


ROUTER_BLOCK = 256


def _router_kernel(x_ref, sh_ref, sc_ref, g_ref, rw_ref, rb_ref, cnt_in_ref,
                   hp_ref, idx_ref, wgt_ref, rank_ref, cnt_ref, run_ref):
    i = pl.program_id(0)

    @pl.when(i == 0)
    def _():
        run_ref[...] = cnt_in_ref[...]

    h = _norm_modulate(x_ref[...], g_ref[...], sh_ref[...], sc_ref[...])
    tb, d = h.shape
    packed = pltpu.pack_elementwise([h[:, :d // 2], h[:, d // 2:]], packed_dtype=BF16)
    for k in range(SUBLANES):
        hp_ref[pl.ds(k, tb, stride=SUBLANES), :] = packed[:, k * LANES:(k + 1) * LANES]

    h_hi = h.astype(BF16)
    h_lo = (h - h_hi.astype(F32)).astype(BF16)
    hi_terms = jnp.dot(h_hi, rw_ref[...], preferred_element_type=F32)
    logits = (hi_terms[:, :LANES]
              + (hi_terms[:, LANES:] + jnp.dot(h_lo, rw_ref[:, :LANES], preferred_element_type=F32))
              + rb_ref[...])
    lane = lax.broadcasted_iota(I32, (tb, LANES), 1)
    lane_f = lane.astype(F32)
    work = logits
    idx_acc = jnp.zeros((tb, LANES), F32)
    val_acc = jnp.full((tb, LANES), NEG_INF, F32)
    hots = []
    for k in range(TOP_K):
        m = jnp.max(work, axis=-1, keepdims=True)
        idx = jnp.min(jnp.where(work == m, lane_f, float(LANES)), axis=-1, keepdims=True)
        hot = lane_f == idx
        hots.append(hot)
        idx_acc = jnp.where(lane == k, idx, idx_acc)
        val_acc = jnp.where(lane == k, m, val_acc)
        work = jnp.where(hot, NEG_INF, work)

    e = jnp.exp(val_acc - jnp.max(val_acc, axis=-1, keepdims=True))
    wgt_ref[...] = e / jnp.sum(e, axis=-1, keepdims=True)
    idx_ref[...] = idx_acc.astype(I32)

    chosen = jnp.zeros((tb, LANES), F32)
    for hot in hots:
        chosen = chosen + jnp.where(hot, 1.0, 0.0)
    row = lax.broadcasted_iota(I32, (tb, tb), 0)
    col = lax.broadcasted_iota(I32, (tb, tb), 1)
    earlier = jnp.where(col < row, 1.0, 0.0).astype(BF16)
    before = jnp.dot(earlier, chosen.astype(BF16), preferred_element_type=F32) + run_ref[...]
    rank_acc = jnp.zeros((tb, LANES), F32)
    for k, hot in enumerate(hots):
        rk = jnp.sum(jnp.where(hot, before, 0.0), axis=-1, keepdims=True)
        rank_acc = jnp.where(lane == k, rk, rank_acc)
    rank_ref[...] = rank_acc.astype(I32)
    run_ref[...] = run_ref[...] + jnp.sum(chosen, axis=0, keepdims=True)
    cnt_ref[...] = run_ref[...]


def _router(x2d, shift, scale, g, rw_pad, rb_pad, counts_in, *, rows_per_mod):
    n, d = x2d.shape
    tb = ROUTER_BLOCK
    per = rows_per_mod // tb
    vmem = 2 * tb * d * 4 + 6 * tb * d * 4 + 2 * d * LANES * 4
    outs = pl.pallas_call(
        _router_kernel,
        grid=(n // tb,),
        in_specs=[pl.BlockSpec((tb, d), lambda i: (i, 0)),
                  pl.BlockSpec((None, 1, d), lambda i: (i // per, 0, 0)),
                  pl.BlockSpec((None, 1, d), lambda i: (i // per, 0, 0)),
                  pl.BlockSpec((1, d), lambda i: (0, 0)),
                  pl.BlockSpec((d, 2 * LANES), lambda i: (0, 0)),
                  pl.BlockSpec((1, LANES), lambda i: (0, 0)),
                  pl.BlockSpec((1, LANES), lambda i: (0, 0))],
        out_specs=[pl.BlockSpec((tb * SUBLANES, LANES), lambda i: (i, 0)),
                   pl.BlockSpec((tb, LANES), lambda i: (i, 0)),
                   pl.BlockSpec((tb, LANES), lambda i: (i, 0)),
                   pl.BlockSpec((tb, LANES), lambda i: (i, 0)),
                   pl.BlockSpec((1, LANES), lambda i: (0, 0))],
        out_shape=[jax.ShapeDtypeStruct((n * SUBLANES, LANES), jnp.uint32),
                   jax.ShapeDtypeStruct((n, LANES), I32),
                   jax.ShapeDtypeStruct((n, LANES), F32),
                   jax.ShapeDtypeStruct((n, LANES), I32),
                   jax.ShapeDtypeStruct((1, LANES), F32)],
        scratch_shapes=[pltpu.VMEM((1, LANES), F32)],
        compiler_params=_params(("arbitrary",), vmem),
        name="router",
    )(x2d, shift, scale, g, rw_pad, rb_pad, counts_in)
    return outs


def _dispatch_kernel(fill_ref, slot_ref, hp_ref, xs_ref, zero_ref, sem, fill_sem):
    tb = hp_ref.shape[0] // SUBLANES
    t_rows = zero_ref.shape[0]

    @pl.when(pl.program_id(0) == 0)
    def _():
        zero_ref[...] = jnp.zeros_like(zero_ref)

        def tile_copy(t):
            return pltpu.make_async_copy(zero_ref, xs_ref.at[pl.ds(t * t_rows, t_rows), :], fill_sem)

        def fill(t, carry):
            @pl.when(fill_ref[t] == 1)
            def _():
                tile_copy(t).start()
            return carry

        def fill_wait(t, carry):
            @pl.when(fill_ref[t] == 1)
            def _():
                tile_copy(t).wait()
            return carry

        lax.fori_loop(0, fill_ref.shape[0], fill, 0)
        lax.fori_loop(0, fill_ref.shape[0], fill_wait, 0)

    def row_copy(j, s):
        src = j * SUBLANES if isinstance(j, int) else pl.multiple_of(j * SUBLANES, SUBLANES)
        dst = s * SUBLANES if isinstance(s, int) else pl.multiple_of(s * SUBLANES, SUBLANES)
        return pltpu.make_async_copy(hp_ref.at[pl.ds(src, SUBLANES), :],
                                     xs_ref.at[pl.ds(dst, SUBLANES), :], sem)

    def issue(j, carry):
        for k in range(TOP_K):
            row_copy(j, slot_ref[k, j]).start()
        return carry

    lax.fori_loop(0, tb, issue, 0)

    for k in range(TOP_K):
        pltpu.make_async_copy(hp_ref, xs_ref.at[pl.ds(0, tb * SUBLANES), :], sem).wait()


def _dispatch(fill_tiles, slots_t, hp, n_slots):
    n = hp.shape[0] // SUBLANES
    tb = ROUTER_BLOCK
    grid_spec = pltpu.PrefetchScalarGridSpec(
        num_scalar_prefetch=1,
        grid=(n // tb,),
        in_specs=[pl.BlockSpec((TOP_K, tb), lambda i, f: (0, i), memory_space=pltpu.SMEM),
                  pl.BlockSpec((tb * SUBLANES, LANES), lambda i, f: (i, 0))],
        out_specs=pl.BlockSpec(memory_space=pl.ANY),
        scratch_shapes=[pltpu.VMEM((MOE_TILE * SUBLANES, LANES), jnp.uint32),
                        pltpu.SemaphoreType.DMA(()), pltpu.SemaphoreType.DMA(())])
    return pl.pallas_call(
        _dispatch_kernel, grid_spec=grid_spec,
        out_shape=jax.ShapeDtypeStruct((n_slots * SUBLANES, LANES), jnp.uint32),
        compiler_params=_params(("arbitrary",), (2 * tb + MOE_TILE) * SUBLANES * LANES * 4),
        name="dispatch",
    )(fill_tiles, slots_t, hp)


EXPERT_TILES = 10
HIDDEN_CHUNK = 512


def _swiglu_pairs(a):
    even = (lax.broadcasted_iota(I32, (1, LANES), 1) % 2) == 0
    outs = []
    for c in range(a.shape[1] // (2 * LANES)):
        ga = a[:, (2 * c) * LANES:(2 * c + 1) * LANES]
        gb = a[:, (2 * c + 1) * LANES:(2 * c + 2) * LANES]
        glu = jnp.where(even, ga, pltpu.roll(gb, 1, 1))
        lin = jnp.where(even, pltpu.roll(ga, LANES - 1, 1), gb)
        glu = jnp.minimum(glu, SWIGLU_LIMIT)
        lin = jnp.clip(lin, -SWIGLU_LIMIT, SWIGLU_LIMIT)
        outs.append((glu * jax.nn.sigmoid(SWIGLU_ALPHA * glu) * (lin + 1.0)).astype(BF16))
    return outs[0] if len(outs) == 1 else jnp.concatenate(outs, axis=1)


def _interleaved_rows_bf16(w_ref, dst_ref):
    half = LANES // 2
    for qd in range(w_ref.shape[0] // LANES):
        wa = w_ref[qd * LANES:qd * LANES + half, :]
        wc = w_ref[qd * LANES + half:(qd + 1) * LANES, :]
        packed = pltpu.pack_elementwise([wa, wc], packed_dtype=BF16)
        dst_ref[qd * LANES:(qd + 1) * LANES, :] = pltpu.bitcast(packed, BF16)


def _expert_ffn_kernel(ie_ref, row0_ref, nt_ref, zt_ref, xs_ref, w1_ref, b1_ref, w2_ref, b2_ref, y_ref,
                       xb_ref, stage_ref, ybuf_ref, w1b_ref, w2b_ref, pend_ref,
                       sem_in, sem_out, *, n_chunks, n_items):
    del ie_ref
    i = pl.program_id(0)
    c = pl.program_id(1)
    t_rows = MOE_TILE
    nt = nt_ref[i]
    row0 = row0_ref[i]

    def out_copy(src_ref, r_src, r_dst):
        dst = r_dst if isinstance(r_dst, int) else pl.multiple_of(r_dst, t_rows)
        return pltpu.make_async_copy(src_ref.at[pl.ds(r_src, t_rows), :],
                                     y_ref.at[pl.ds(dst, t_rows), :], sem_out)

    def wait_pending():
        def body(_, carry):
            out_copy(ybuf_ref, 0, 0).wait()
            return carry
        lax.fori_loop(0, pend_ref[0], body, 0)
        pend_ref[0] = 0

    @pl.when((i == 0) & (c == 0))
    def _():
        pend_ref[0] = 0

    def in_copy(first_row, t):
        src = pl.multiple_of((first_row + t * t_rows) * SUBLANES, t_rows * SUBLANES)
        return pltpu.make_async_copy(xs_ref.at[pl.ds(src, t_rows * SUBLANES), :],
                                     stage_ref.at[t], sem_in.at[t])

    def request_tiles(item):
        def body(t, carry):
            in_copy(row0_ref[item], t).start()
            return carry
        lax.fori_loop(0, nt_ref[item], body, 0)

    @pl.when(c == 0)
    def _():
        wait_pending()

        @pl.when(i == 0)
        def _():
            request_tiles(0)

        def load(t, carry):
            slot = t
            in_copy(row0, t).wait()

            r = pl.multiple_of(t * t_rows, t_rows)
            half = xb_ref.shape[1] // 2
            for k in range(SUBLANES):
                xw = stage_ref[slot, pl.ds(k, t_rows, stride=SUBLANES), :]
                lo = pltpu.unpack_elementwise(xw, index=0, packed_dtype=BF16, unpacked_dtype=F32)
                hi = pltpu.unpack_elementwise(xw, index=1, packed_dtype=BF16, unpacked_dtype=F32)
                xb_ref[pl.ds(r, t_rows), k * LANES:(k + 1) * LANES] = lo.astype(BF16)
                xb_ref[pl.ds(r, t_rows), half + k * LANES:half + (k + 1) * LANES] = hi.astype(BF16)
            return carry

        lax.fori_loop(0, nt, load, 0)

        def zfill(t, carry):
            out_copy(ybuf_ref, 0, row0 + t * t_rows).start()
            return carry

        @pl.when(zt_ref[i] > 0)
        def _():
            ybuf_ref[0:t_rows, :] = jnp.zeros((t_rows, ybuf_ref.shape[1]), F32)

        lax.fori_loop(0, zt_ref[i], zfill, 0)
        pend_ref[0] = zt_ref[i]

    @pl.when((c == n_chunks - 1) & (i + 1 < n_items))
    def _():
        request_tiles(i + 1)

    @pl.when(nt > 0)
    def _():
        w1b_ref[...] = w1_ref[...].astype(BF16)
        _interleaved_rows_bf16(w2_ref, w2b_ref)
        last = c == n_chunks - 1

        def run(first):
            def rows_step(r, n_tiles):
                n_rows = n_tiles * t_rows
                xb = xb_ref[pl.ds(r, n_rows), :]
                a = jnp.dot(xb, w1b_ref[...], preferred_element_type=F32) + b1_ref[...]
                act = _swiglu_pairs(a)
                part = jnp.dot(act, w2b_ref[...], preferred_element_type=F32)
                if first:
                    ybuf_ref[pl.ds(r, n_rows), :] = part + b2_ref[...]
                else:
                    ybuf_ref[pl.ds(r, n_rows), :] += part

                if not first or n_chunks == 1:
                    @pl.when(last)
                    def _():
                        for k in range(n_tiles):
                            out_copy(ybuf_ref, r + k * t_rows, row0 + r + k * t_rows).start()

            def quad(p, carry):
                rows_step(pl.multiple_of(p * (4 * t_rows), 4 * t_rows), 4)
                return carry

            lax.fori_loop(0, nt // 4, quad, 0)

            @pl.when(nt % 4 >= 2)
            def _():
                rows_step(pl.multiple_of((nt // 4) * (4 * t_rows), 2 * t_rows), 2)

            @pl.when(nt % 2 == 1)
            def _():
                rows_step(pl.multiple_of((nt - 1) * t_rows, t_rows), 1)

        @pl.when(c == 0)
        def _():
            run(True)

        @pl.when(c != 0)
        def _():
            run(False)

        @pl.when(last)
        def _():
            pend_ref[0] = nt

    @pl.when((i == n_items - 1) & (c == n_chunks - 1))
    def _():
        wait_pending()


def _expert_ffn(item_tabs, xs, w1, b1, w2, b2, layer, max_items):
    p = xs.shape[0] // SUBLANES
    half = w1.shape[2] // 2
    _, e, d, n = w1.shape
    hc = HIDDEN_CHUNK
    n_chunks = (n // 2) // hc
    rows = EXPERT_TILES * MOE_TILE

    def chunk(c, nt, i):
        return jnp.where(nt[i] > 0, c, n_chunks - 1)

    grid_spec = pltpu.PrefetchScalarGridSpec(
        num_scalar_prefetch=4,
        grid=(max_items, n_chunks),
        in_specs=[pl.BlockSpec(memory_space=pl.ANY),
                  pl.BlockSpec((None, None, d, 2 * hc), lambda i, c, ie, r0, nt, zt: (layer, ie[i], 0, chunk(c, nt, i))),
                  pl.BlockSpec((None, None, 1, 2 * hc), lambda i, c, ie, r0, nt, zt: (layer, ie[i], 0, chunk(c, nt, i))),
                  pl.BlockSpec((None, None, hc, d), lambda i, c, ie, r0, nt, zt: (layer, ie[i], chunk(c, nt, i), 0)),
                  pl.BlockSpec((None, None, 1, d), lambda i, c, ie, r0, nt, zt: (layer, ie[i], 0, 0))],
        out_specs=pl.BlockSpec(memory_space=pl.ANY),
        scratch_shapes=[pltpu.VMEM((rows, d), BF16),
                        pltpu.VMEM((EXPERT_TILES, MOE_TILE * SUBLANES, LANES), jnp.uint32),
                        pltpu.VMEM((rows, d), F32),
                        pltpu.VMEM((d, 2 * hc), BF16),
                        pltpu.VMEM((hc, d), BF16),
                        pltpu.SMEM((1,), I32),
                        pltpu.SemaphoreType.DMA((EXPERT_TILES,)),
                        pltpu.SemaphoreType.DMA(())])
    vmem = (rows * d * 6 + 2 * MOE_TILE * half * 4 + MOE_TILE * d * 4
            + 3 * d * 2 * hc * 4 + d * 2 * hc * 2 + 3 * hc * d * 4 + hc * d * 2
            + 8 * MOE_TILE * 2 * hc * 4 + 4 * MOE_TILE * d * 4)
    depth = w1.shape[0]
    return pl.pallas_call(
        functools.partial(_expert_ffn_kernel, n_chunks=n_chunks, n_items=max_items), grid_spec=grid_spec,
        out_shape=jax.ShapeDtypeStruct((p, d), F32),
        compiler_params=_params(("arbitrary", "arbitrary"), vmem),
        name="expert_ffn",
    )(*item_tabs, xs, w1, b1.reshape(depth, e, 1, n), w2, b2.reshape(depth, e, 1, d))


COMBINE_BLOCK = 128


def _combine_kernel(slot_ref, slot_next_ref, y_ref, w_ref, x_ref, g_ref, o_ref, buf, sem, *, nblk):
    i = pl.program_id(0)
    tb = x_ref.shape[0]

    def row_copy(s, b, k, j):
        return pltpu.make_async_copy(y_ref.at[pl.ds(s, 1), :], buf.at[b, k, pl.ds(j, 1), :], sem.at[b])

    def issue(slots, b):
        def body(j8, carry):
            for jj in range(SUBLANES):
                j = j8 * SUBLANES + jj
                for k in range(TOP_K):
                    row_copy(slots[k, j], b, k, j).start()
            return carry
        lax.fori_loop(0, tb // SUBLANES, body, 0)

    @pl.when(i == 0)
    def _():
        issue(slot_ref, 0)

    @pl.when(i + 1 < nblk)
    def _():
        issue(slot_next_ref, (i + 1) % 2)

    cur = i % 2
    for k in range(TOP_K):
        pltpu.make_async_copy(y_ref.at[pl.ds(0, tb), :], buf.at[cur, k], sem.at[cur]).wait()

    w = w_ref[...]
    acc = w[:, 0:1] * buf[cur, 0]
    for k in range(1, TOP_K):
        acc = acc + w[:, k:k + 1] * buf[cur, k]
    o_ref[...] = x_ref[...] + g_ref[...] * acc


def _combine(slots_t, y, gates, x2d, gate_mod, *, rows_per_mod):
    n, d = x2d.shape
    tb = COMBINE_BLOCK
    nblk = n // tb
    per = rows_per_mod // tb
    vmem = 2 * TOP_K * tb * d * 4 + 6 * tb * d * 4
    return pl.pallas_call(
        functools.partial(_combine_kernel, nblk=nblk),
        grid=(nblk,),
        in_specs=[pl.BlockSpec((TOP_K, tb), lambda i: (0, i), memory_space=pltpu.SMEM),
                  pl.BlockSpec((TOP_K, tb), lambda i: (0, jnp.minimum(i + 1, nblk - 1)),
                               memory_space=pltpu.SMEM),
                  pl.BlockSpec(memory_space=pl.ANY),
                  pl.BlockSpec((tb, LANES), lambda i: (i, 0)),
                  pl.BlockSpec((tb, d), lambda i: (i, 0)),
                  pl.BlockSpec((None, 1, d), lambda i: (i // per, 0, 0))],
        out_specs=pl.BlockSpec((tb, d), lambda i: (i, 0)),
        out_shape=jax.ShapeDtypeStruct((n, d), F32),
        scratch_shapes=[pltpu.VMEM((2, TOP_K, tb, d), F32), pltpu.SemaphoreType.DMA((2,))],
        compiler_params=_params(("arbitrary",), vmem),
        name="combine",
    )(slots_t, slots_t, y, gates, x2d, gate_mod)


def _owner(ends, idx):
    return jnp.minimum(jnp.sum((ends[None, :] <= idx[:, None]).astype(I32), axis=1), ends.shape[0] - 1)


def _schedule(counts, max_tiles, max_items):
    t_rows = MOE_TILE
    tiles_e = (counts + t_rows - 1) // t_rows
    tile_end = jnp.cumsum(tiles_e)
    tile_start = tile_end - tiles_e
    num_tiles = tile_end[-1]
    t = jnp.arange(max_tiles, dtype=I32)
    te = _owner(tile_end, jnp.minimum(t, num_tiles - 1))
    fill = jnp.where((t < num_tiles) & (t != tile_end[te] - 1), 0, 1)

    items_e = (tiles_e + EXPERT_TILES - 1) // EXPERT_TILES
    item_end = jnp.cumsum(items_e)
    item_start = item_end - items_e
    n_items = item_end[-1]
    i = jnp.arange(max_items, dtype=I32)
    live = i < n_items
    ie = _owner(item_end, jnp.minimum(i, n_items - 1))
    local = jnp.minimum(i, n_items - 1) - item_start[ie]
    nt = jnp.where(live, jnp.clip(tiles_e[ie] - local * EXPERT_TILES, 0, EXPERT_TILES), 0)
    dead_tile0 = num_tiles + (i - n_items) * EXPERT_TILES
    zt = jnp.where(live, 0, jnp.clip(max_tiles - dead_tile0, 0, EXPERT_TILES))
    row0 = jnp.where(live, (tile_start[ie] + local * EXPERT_TILES) * t_rows,
                     jnp.clip(dead_tile0, 0, max_tiles - 1) * t_rows)
    return ((tile_start * t_rows).astype(I32), fill.astype(I32),
            (ie.astype(I32), row0.astype(I32), nt.astype(I32), zt.astype(I32)))


def _moe(parts, shift2, scale2, gate2, norm_g, router_w, router_b, w1, b1, w2, b2, layer):
    d = router_w.shape[0]
    rw_f32 = jnp.zeros((d, LANES), F32).at[:, :N_EXPERTS].set(router_w)
    rw_hi = lax.bitcast_convert_type(
        lax.bitcast_convert_type(rw_f32, jnp.uint32) & jnp.uint32(0xFFFF0000), F32)
    rw_pad = jnp.concatenate([rw_hi.astype(BF16), (rw_f32 - rw_hi).astype(BF16)], axis=1)
    rb_pad = jnp.full((1, LANES), NEG_INF, F32).at[0, :N_EXPERTS].set(router_b)
    counts = jnp.zeros((1, LANES), F32)
    routed = []
    for x2d, mod_rows, per in parts:
        hp, idx, wgt, rank, counts = _router(
            x2d, shift2[mod_rows][:, None, :], scale2[mod_rows][:, None, :], norm_g[None, :],
            rw_pad, rb_pad, counts, rows_per_mod=per)
        routed.append((hp, idx[:, :TOP_K], wgt, rank[:, :TOP_K]))
    n_total = sum(x2d.shape[0] for x2d, _, _ in parts)
    max_tiles = (n_total * TOP_K) // MOE_TILE + N_EXPERTS
    n_slots = max_tiles * MOE_TILE
    max_items = N_EXPERTS + -(-max_tiles // EXPERT_TILES)
    group_start, fill_tiles, tabs = _schedule(counts[0, :N_EXPERTS].astype(I32), max_tiles, max_items)

    slots = [(group_start[idx] + rank).T for _, idx, _, rank in routed]
    hp_all = jnp.concatenate([r[0] for r in routed], axis=0) if len(routed) > 1 else routed[0][0]
    slots_all = jnp.concatenate(slots, axis=1) if len(slots) > 1 else slots[0]
    xs = _dispatch(fill_tiles, slots_all, hp_all, n_slots)
    y = _expert_ffn(tabs, xs, w1, b1, w2, b2, layer, max_items)
    outs = []
    for (x2d, mod_rows, per), sl, (_, _, wgt, _) in zip(parts, slots, routed):
        outs.append(_combine(sl, y, wgt, x2d, gate2[mod_rows][:, None, :], rows_per_mod=per))
    return outs


def _rope_tables(seq):
    t = np.arange(seq)
    m = HEAD_DIM // 4
    inv_freq = ROPE_THETA ** (-np.arange(m, dtype=np.float64) / m)
    ang_r = (t // GRID_W)[:, None] * inv_freq[None, :]
    ang_c = (t % GRID_W)[:, None] * inv_freq[None, :]
    cos = np.concatenate([np.cos(ang_r)] * 2 + [np.cos(ang_c)] * 2, axis=-1)
    sin = np.concatenate([-np.sin(ang_r), np.sin(ang_r), -np.sin(ang_c), np.sin(ang_c)], axis=-1)
    return jnp.asarray(cos, F32), jnp.asarray(sin, F32)


def kernel(x, c, ctx, c_ctx, ada_w, ada_b, norm_mix_g, norm_ffn_g, a_w_qkv, a_w_o, a_q_gain, a_k_gain,
           b_w_qkv, b_w_o, b_q_gain, b_k_gain, b_rel_bias, router_w, router_b, exp_w1, exp_b1, exp_w2, exp_b2):
    bsz, seq, d = x.shape
    l_ctx = ctx.shape[1]
    depth = ada_w.shape[0]
    n_heads = d // HEAD_DIM
    q_scale = HEAD_DIM ** -0.5 * LOG2E

    c_rows = jnp.zeros((SUBLANES, d), F32).at[:bsz].set(c).at[bsz].set(c_ctx)
    mod = _ada_modulation(c_rows, ada_w, ada_b)
    cos, sin = _rope_tables(seq)
    ones_tab = jnp.ones((l_ctx, HEAD_DIM), F32)
    x_rows = jnp.arange(bsz)
    c_rows_idx = jnp.full((bsz,), bsz)

    for i in range(depth):
        last = i == depth - 1
        sh1, sc1, g1, sh2, sc2, g2 = [mod[i, :, k * d:(k + 1) * d] for k in range(N_MOD)]
        j = i // 2
        if i % 2 == 0:
            w_qkv, w_o, qg, kg = a_w_qkv[j], a_w_o[j], a_q_gain[j], a_k_gain[j]
            n_q, n_k, n_v = n_heads, A_KV_HEADS, A_KV_HEADS
        else:
            w_qkv, w_o, qg, kg = b_w_qkv[j], b_w_o[j], b_q_gain[j], b_k_gain[j]
            n_q, n_k, n_v = n_heads, n_heads, n_heads
        heads_per_tile = 512 // HEAD_DIM
        gain_cols = jnp.concatenate([jnp.tile(qg * q_scale, n_q), jnp.tile(kg, n_k),
                                     jnp.ones((n_v * HEAD_DIM,), F32)])[None, :]
        flags = jnp.concatenate([jnp.ones(((n_q + n_k) // heads_per_tile,), I32),
                                 jnp.zeros((n_v // heads_per_tile,), I32)])
        rope = i % 2 == 0
        qkv_x = _qkv_project(x, sh1[x_rows][:, None, :], sc1[x_rows][:, None, :], norm_mix_g[i][None, :],
                             w_qkv, gain_cols, flags, cos, sin, rope=rope, tm=1024)
        qkv_c = _qkv_project(ctx, sh1[c_rows_idx][:, None, :], sc1[c_rows_idx][:, None, :],
                             norm_mix_g[i][None, :], w_qkv, gain_cols, flags, ones_tab, ones_tab,
                             rope=False, tm=l_ctx)
        if i % 2 == 0:
            ox = _attention(qkv_x, qkv_c, qkv_x, n_q_heads=n_q, n_kv_heads=n_k,
                            k_col0=n_q, v_col0=n_q + n_k, tq=256)
        else:
            ox = _neighborhood_attention(qkv_x, qkv_c, b_rel_bias[j], n_heads=n_heads)
        x = _out_project(ox, w_o, x, g1[x_rows][:, None, :], tm=1024)
        parts = [(x.reshape(bsz * seq, d), x_rows, seq)]
        if not last:
            oc = _attention(qkv_c, qkv_c, None, n_q_heads=n_q, n_kv_heads=n_k,
                            k_col0=n_q, v_col0=n_q + n_k, tq=l_ctx)
            ctx = _out_project(oc, w_o, ctx, g1[c_rows_idx][:, None, :], tm=l_ctx)
            parts.append((ctx.reshape(bsz * l_ctx, d), jnp.full((1,), bsz), bsz * l_ctx))
        outs = _moe(parts, sh2, sc2, g2, norm_ffn_g[i], router_w[i], router_b[i],
                    exp_w1, exp_b1, exp_w2, exp_b2, i)
        x = outs[0].reshape(bsz, seq, d)
        if not last:
            ctx = outs[1].reshape(bsz, l_ctx, d)
    return x
```

```python
import functools
import math

import jax
import jax.numpy as jnp
import numpy as np
from jax import lax
from jax.experimental import pallas as pl
from jax.experimental.pallas import tpu as pltpu

F32 = jnp.float32
BF16 = jnp.bfloat16
I32 = jnp.int32

LANES = 128
SUBLANES = 8
VMEM_BYTES_V7X = 64 * 1024 * 1024

HEAD_DIM = 128
GRID_W = 64
A_KV_HEADS = 4
NA_WIN_ROWS = 8
NA_WIN_COLS = 16
N_EXPERTS = 32
TOP_K = 4
ROPE_THETA = 10000.0
SWIGLU_ALPHA = 1.702
SWIGLU_LIMIT = 7.0
NORM_EPS = 1e-6
N_MOD = 6
LOG2E = math.log2(math.e)
NEG_INF = float("-inf")

MOE_TILE = 128


def _vmem_limit(nbytes):
    return int(min(nbytes + 8 * 1024 * 1024, VMEM_BYTES_V7X - 8 * 1024 * 1024))


def _params(semantics, vmem_bytes):
    return pltpu.CompilerParams(dimension_semantics=semantics,
                                vmem_limit_bytes=_vmem_limit(vmem_bytes))


def _ada_kernel(c_ref, w_ref, b_ref, o_ref):
    c = c_ref[...]
    a = (c * jax.nn.sigmoid(c)).astype(BF16)
    o_ref[0] = jnp.dot(a, w_ref[0].astype(BF16), preferred_element_type=F32) + b_ref[0]


def _ada_modulation(c_rows, ada_w, ada_b):
    depth, d, n = ada_w.shape
    rows = c_rows.shape[0]
    tn = 1024
    return pl.pallas_call(
        _ada_kernel,
        grid=(depth, n // tn),
        in_specs=[pl.BlockSpec((rows, d), lambda l, j: (0, 0)),
                  pl.BlockSpec((1, d, tn), lambda l, j: (l, 0, j)),
                  pl.BlockSpec((1, 1, tn), lambda l, j: (l, 0, j))],
        out_specs=pl.BlockSpec((1, rows, tn), lambda l, j: (l, 0, j)),
        out_shape=jax.ShapeDtypeStruct((depth, rows, n), F32),
        compiler_params=_params(("arbitrary", "arbitrary"), 2 * d * tn * 4 + d * tn * 2),
        name="ada_modulation",
    )(c_rows, ada_w, ada_b.reshape(depth, 1, n))


def _norm_modulate(x, g, shift, scale):
    ms = jnp.mean(x * x, axis=-1, keepdims=True)
    y = x * lax.rsqrt(ms + NORM_EPS) * g
    return y * (1.0 + scale) + shift


def _norm_mod_kernel(x_ref, sh_ref, sc_ref, g_ref, h_ref):
    h_ref[...] = _norm_modulate(x_ref[...], g_ref[...], sh_ref[...], sc_ref[...]).astype(BF16)


def _norm_mod(x, shift, scale, g, *, tm):
    b, t, d = x.shape
    return pl.pallas_call(
        _norm_mod_kernel,
        grid=(b, t // tm),
        in_specs=[pl.BlockSpec((None, tm, d), lambda bi, m: (bi, m, 0)),
                  pl.BlockSpec((None, 1, d), lambda bi, m: (bi, 0, 0)),
                  pl.BlockSpec((None, 1, d), lambda bi, m: (bi, 0, 0)),
                  pl.BlockSpec((1, d), lambda bi, m: (0, 0))],
        out_specs=pl.BlockSpec((None, tm, d), lambda bi, m: (bi, m, 0)),
        out_shape=jax.ShapeDtypeStruct((b, t, d), BF16),
        compiler_params=_params(("arbitrary", "arbitrary"), 2 * tm * d * 6 + 4 * tm * d * 4),
        name="norm_mod",
    )(x, shift, scale, g)


def _qkv_kernel(flag_ref, h_ref, w_ref, gain_ref, cos_ref, sin_ref, o_ref, wb_ref, *, rope):
    n = pl.program_id(0)

    @pl.when((pl.program_id(1) == 0) & (pl.program_id(2) == 0))
    def _():
        wb_ref[...] = w_ref[...].astype(BF16)

    tm, tn = o_ref.shape
    piece = min(tm, 256)
    pieces = [slice(r, r + piece) for r in range(0, tm, piece)]

    @pl.when(flag_ref[n] == 0)
    def _():
        for rs in pieces:
            o_ref[rs, :] = jnp.dot(h_ref[rs, :], wb_ref[...], preferred_element_type=F32).astype(BF16)

    @pl.when(flag_ref[n] == 1)
    def _():
        lane = lax.broadcasted_iota(I32, (1, HEAD_DIM), 1)
        first_half = (lane % (HEAD_DIM // 2)) < (HEAD_DIM // 4)
        for rs in pieces:
            a = jnp.dot(h_ref[rs, :], wb_ref[...], preferred_element_type=F32)
            for hh in range(tn // HEAD_DIM):
                sl = slice(hh * HEAD_DIM, (hh + 1) * HEAD_DIM)
                ah = a[:, sl]
                ms = jnp.mean(ah * ah, axis=-1, keepdims=True)
                yh = ah * lax.rsqrt(ms + NORM_EPS) * gain_ref[:, sl]
                if rope:
                    partner = jnp.where(first_half,
                                        pltpu.roll(yh, HEAD_DIM - HEAD_DIM // 4, 1),
                                        pltpu.roll(yh, HEAD_DIM // 4, 1))
                    yh = yh * cos_ref[rs, :] + partner * sin_ref[rs, :]
                o_ref[rs, sl] = yh.astype(BF16)


def _qkv_project(x, shift, scale, g, w, gain_cols, norm_flags, cos, sin, *, rope, tm):
    b, t, d = x.shape
    n = w.shape[1]
    tn = 512
    h = _norm_mod(x, shift, scale, g, tm=min(tm, 512))
    kernel = functools.partial(_qkv_kernel, rope=rope)
    grid_spec = pltpu.PrefetchScalarGridSpec(
        num_scalar_prefetch=1,
        grid=(n // tn, b, t // tm),
        in_specs=[pl.BlockSpec((None, tm, d), lambda j, bi, m, f: (bi, m, 0)),
                  pl.BlockSpec((d, tn), lambda j, bi, m, f: (0, j)),
                  pl.BlockSpec((1, tn), lambda j, bi, m, f: (0, j)),
                  pl.BlockSpec((tm, HEAD_DIM), lambda j, bi, m, f: (m, 0)),
                  pl.BlockSpec((tm, HEAD_DIM), lambda j, bi, m, f: (m, 0))],
        out_specs=pl.BlockSpec((None, tm, tn), lambda j, bi, m, f: (bi, m, j)),
        scratch_shapes=[pltpu.VMEM((d, tn), BF16)])
    vmem = 2 * tm * d * 2 + 2 * d * tn * 4 + d * tn * 2 + 6 * tm * tn * 4
    return pl.pallas_call(
        kernel, grid_spec=grid_spec,
        out_shape=jax.ShapeDtypeStruct((b, t, n), BF16),
        compiler_params=_params(("arbitrary", "arbitrary", "arbitrary"), vmem),
        name="qkv_project",
    )(norm_flags, h, w, gain_cols, cos, sin)


def _nt_dot(a, b):
    return lax.dot_general(a, b, (((1,), (1,)), ((), ())), preferred_element_type=F32)


def _values_with_ones(v_ref, dst_ref):
    dst_ref[:, :HEAD_DIM] = v_ref[...]
    dst_ref[:, HEAD_DIM:] = jnp.ones((v_ref.shape[0], HEAD_DIM), BF16)


def _attn_kernel(q_ref, kc_ref, vc_ref, *rest, group, has_x):
    if has_x:
        kx_ref, vx_ref, o_ref = rest
    else:
        (o_ref,) = rest
    kc = kc_ref[...]
    vc = vc_ref[...]
    for g in range(group):
        sl = slice(g * HEAD_DIM, (g + 1) * HEAD_DIM)
        q = q_ref[:, sl]
        sc = _nt_dot(q, kc)
        m = jnp.max(sc, axis=-1, keepdims=True)
        if has_x:
            sx = _nt_dot(q, kx_ref[...])
            m = jnp.maximum(m, jnp.max(sx, axis=-1, keepdims=True))
        pc = jnp.exp2(sc - m)
        l = jnp.sum(pc, axis=-1, keepdims=True)
        acc = jnp.dot(pc.astype(BF16), vc, preferred_element_type=F32)
        if has_x:
            px = jnp.exp2(sx - m)
            l = l + jnp.sum(px, axis=-1, keepdims=True)
            acc = acc + jnp.dot(px.astype(BF16), vx_ref[...], preferred_element_type=F32)
        o_ref[:, sl] = (acc / l).astype(BF16)


def _attention(q_src, ctx_src, x_src, *, n_q_heads, n_kv_heads, k_col0, v_col0, tq):
    b, t_q, _ = q_src.shape
    l = ctx_src.shape[1]
    group = n_q_heads // n_kv_heads
    has_x = x_src is not None
    gw = group * HEAD_DIM
    in_specs = [pl.BlockSpec((None, tq, gw), lambda bi, kv, i: (bi, i, kv)),
                pl.BlockSpec((None, l, HEAD_DIM), lambda bi, kv, i: (bi, 0, k_col0 + kv)),
                pl.BlockSpec((None, l, HEAD_DIM), lambda bi, kv, i: (bi, 0, v_col0 + kv))]
    args = [q_src, ctx_src, ctx_src]
    s = 0
    if has_x:
        s = x_src.shape[1]
        in_specs += [pl.BlockSpec((None, s, HEAD_DIM), lambda bi, kv, i: (bi, 0, k_col0 + kv)),
                     pl.BlockSpec((None, s, HEAD_DIM), lambda bi, kv, i: (bi, 0, v_col0 + kv))]
        args += [x_src, x_src]
    vmem = 4 * (l + s) * HEAD_DIM * 2 + 4 * tq * gw * 2 + 4 * tq * (l + s) * 4
    return pl.pallas_call(
        functools.partial(_attn_kernel, group=group, has_x=has_x),
        grid=(b, n_kv_heads, t_q // tq),
        in_specs=in_specs,
        out_specs=pl.BlockSpec((None, tq, gw), lambda bi, kv, i: (bi, i, kv)),
        out_shape=jax.ShapeDtypeStruct((b, t_q, n_q_heads * HEAD_DIM), BF16),
        compiler_params=_params(("arbitrary", "arbitrary", "arbitrary"), vmem),
        name="attention_x" if has_x else "attention_ctx",
    )(*args)


NA_Q_ROWS = 8
NA_BAND_ROWS = 2 * NA_WIN_ROWS
NA_DR = 2 * NA_WIN_ROWS - 1
NA_DC = 2 * NA_WIN_COLS - 1
NA_PAIR_TILES = 30


def _na_kernel(tab_ref, q_ref, k_ref, v_ref, kc_ref, vc_ref, o_ref, tile_ref, pair_ref, s_ref,
               v1_ref, vc1_ref, *, rows):
    h = pl.program_id(1)
    a = pl.program_id(2)
    nb = NA_BAND_ROWS * GRID_W

    @pl.when(a == 0)
    def _():
        _values_with_ones(v_ref, v1_ref)
        _values_with_ones(vc_ref, vc1_ref)
        qc = lax.broadcasted_iota(I32, (GRID_W, LANES), 0)
        kc = lax.broadcasted_iota(I32, (GRID_W, LANES), 1) % GRID_W
        dc = jnp.clip(kc - qc, -(NA_WIN_COLS - 1), NA_WIN_COLS - 1) + NA_WIN_COLS - 1
        c0 = jnp.clip(qc - NA_WIN_COLS // 2, 0, GRID_W - NA_WIN_COLS)
        col_ok = (kc >= c0) & (kc < c0 + NA_WIN_COLS)
        for dr in range(NA_DR):
            acc = jnp.zeros((GRID_W, LANES), F32)
            for d in range(NA_DC):
                acc = jnp.where(dc == d, tab_ref[h * (NA_DR * NA_DC) + dr * NA_DC + d] * LOG2E, acc)
            tile_ref[dr] = jnp.where(col_ok, acc, NEG_INF)
        left = lax.broadcasted_iota(I32, (GRID_W, LANES), 1) < GRID_W
        zero = jnp.zeros((GRID_W, LANES), F32)
        for p in range(NA_PAIR_TILES):
            lo = tile_ref[p - NA_WIN_ROWS] if 0 <= p - NA_WIN_ROWS < NA_DR else zero
            hi = tile_ref[p - NA_WIN_ROWS + 1] if 0 <= p - NA_WIN_ROWS + 1 < NA_DR else zero
            pair_ref[p] = jnp.where(left, lo, hi)

    wr = NA_WIN_ROWS
    kb0 = jnp.clip(a * NA_Q_ROWS - wr // 2, 0, rows - NA_BAND_ROWS)
    k_band = k_ref[pl.ds(pl.multiple_of(kb0 * GRID_W, 4 * GRID_W), nb), :]
    v_band = v1_ref[pl.ds(pl.multiple_of(kb0 * GRID_W, 4 * GRID_W), nb), :]
    q = q_ref[...]
    s_loc = _nt_dot(q, k_band)
    s_ctx = _nt_dot(q, kc_ref[...])

    band_row = lax.broadcasted_iota(I32, (1, nb), 1) // GRID_W
    for rq in range(NA_Q_ROWS):
        r = a * NA_Q_ROWS + rq
        r0 = jnp.clip(r - wr // 2, 0, rows - wr)
        lo = r0 - kb0
        row_mask = jnp.where((band_row >= lo) & (band_row < lo + wr), 0.0, NEG_INF)
        j0 = kb0 - r + 2 * wr - 1
        bias = jnp.concatenate([pair_ref[j0 + 2 * i] for i in range(NA_BAND_ROWS // 2)], axis=1)
        rs = slice(rq * GRID_W, (rq + 1) * GRID_W)
        sl = s_loc[rs, :] + bias + row_mask
        sc = s_ctx[rs, :]
        m = jnp.maximum(jnp.max(sl, axis=-1, keepdims=True), jnp.max(sc, axis=-1, keepdims=True))
        s_ref[rs, :nb] = jnp.exp2(sl - m).astype(BF16)
        s_ref[rs, nb:] = jnp.exp2(sc - m).astype(BF16)

    pb = s_ref[...]
    acc = (jnp.dot(pb[:, :nb], v_band, preferred_element_type=F32)
           + jnp.dot(pb[:, nb:], vc1_ref[...], preferred_element_type=F32))
    o_ref[...] = (acc[:, :HEAD_DIM] / acc[:, HEAD_DIM:]).astype(BF16)


def _neighborhood_attention(qkv_x, qkv_c, rel_bias, *, n_heads):
    b, s, _ = qkv_x.shape
    l = qkv_c.shape[1]
    rows = s // GRID_W
    nq = NA_Q_ROWS * GRID_W
    nb = NA_BAND_ROWS * GRID_W
    grid_spec = pltpu.PrefetchScalarGridSpec(
        num_scalar_prefetch=0,
        grid=(b, n_heads, rows // NA_Q_ROWS),
        in_specs=[pl.BlockSpec(memory_space=pltpu.SMEM),
                  pl.BlockSpec((None, nq, HEAD_DIM), lambda bi, h, a: (bi, a, h)),
                  pl.BlockSpec((None, s, HEAD_DIM), lambda bi, h, a: (bi, 0, n_heads + h)),
                  pl.BlockSpec((None, s, HEAD_DIM), lambda bi, h, a: (bi, 0, 2 * n_heads + h)),
                  pl.BlockSpec((None, l, HEAD_DIM), lambda bi, h, a: (bi, 0, n_heads + h)),
                  pl.BlockSpec((None, l, HEAD_DIM), lambda bi, h, a: (bi, 0, 2 * n_heads + h))],
        out_specs=pl.BlockSpec((None, nq, HEAD_DIM), lambda bi, h, a: (bi, a, h)),
        scratch_shapes=[pltpu.VMEM((NA_DR, GRID_W, LANES), F32),
                        pltpu.VMEM((NA_PAIR_TILES, GRID_W, LANES), F32),
                        pltpu.VMEM((nq, nb + l), BF16),
                        pltpu.VMEM((s, 2 * HEAD_DIM), BF16),
                        pltpu.VMEM((l, 2 * HEAD_DIM), BF16)])
    vmem = 6 * (s + l) * HEAD_DIM * 2 + 4 * nq * (nb + l) * 4 + 2 * 1024 * 1024
    return pl.pallas_call(
        functools.partial(_na_kernel, rows=rows), grid_spec=grid_spec,
        out_shape=jax.ShapeDtypeStruct((b, s, n_heads * HEAD_DIM), BF16),
        compiler_params=_params(("arbitrary", "arbitrary", "arbitrary"), vmem),
        name="neighborhood_attention",
    )(rel_bias.reshape(-1), qkv_x, qkv_x, qkv_x, qkv_c, qkv_c)


def _oproj_kernel(o_ref, w_ref, x_ref, g_ref, out_ref, wb_ref):
    @pl.when((pl.program_id(1) == 0) & (pl.program_id(2) == 0))
    def _():
        wb_ref[...] = w_ref[...].astype(BF16)

    y = jnp.dot(o_ref[...], wb_ref[...], preferred_element_type=F32)
    out_ref[...] = x_ref[...] + g_ref[...] * y


def _out_project(o, w, x, gate, *, tm):
    b, t, k = o.shape
    d = w.shape[1]
    tn = 512
    vmem = 2 * tm * k * 2 + 2 * k * tn * 4 + k * tn * 2 + 6 * tm * tn * 4
    return pl.pallas_call(
        _oproj_kernel,
        grid=(d // tn, b, t // tm),
        in_specs=[pl.BlockSpec((None, tm, k), lambda j, bi, m: (bi, m, 0)),
                  pl.BlockSpec((k, tn), lambda j, bi, m: (0, j)),
                  pl.BlockSpec((None, tm, tn), lambda j, bi, m: (bi, m, j)),
                  pl.BlockSpec((None, 1, tn), lambda j, bi, m: (bi, 0, j))],
        out_specs=pl.BlockSpec((None, tm, tn), lambda j, bi, m: (bi, m, j)),
        scratch_shapes=[pltpu.VMEM((k, tn), BF16)],
        out_shape=jax.ShapeDtypeStruct((b, t, d), F32),
        compiler_params=_params(("arbitrary", "arbitrary", "arbitrary"), vmem),
        name="out_project",
    )(o, w, x, gate)


ROUTER_BLOCK = 256


def _router_kernel(x_ref, sh_ref, sc_ref, g_ref, rw_ref, rb_ref, cnt_in_ref,
                   hp_ref, idx_ref, wgt_ref, rank_ref, cnt_ref, run_ref):
    i = pl.program_id(0)

    @pl.when(i == 0)
    def _():
        run_ref[...] = cnt_in_ref[...]

    h = _norm_modulate(x_ref[...], g_ref[...], sh_ref[...], sc_ref[...])
    tb, d = h.shape
    packed = pltpu.pack_elementwise([h[:, :d // 2], h[:, d // 2:]], packed_dtype=BF16)
    for k in range(SUBLANES):
        hp_ref[pl.ds(k, tb, stride=SUBLANES), :] = packed[:, k * LANES:(k + 1) * LANES]

    h_hi = h.astype(BF16)
    h_lo = (h - h_hi.astype(F32)).astype(BF16)
    hi_terms = jnp.dot(h_hi, rw_ref[...], preferred_element_type=F32)
    logits = (hi_terms[:, :LANES]
              + (hi_terms[:, LANES:] + jnp.dot(h_lo, rw_ref[:, :LANES], preferred_element_type=F32))
              + rb_ref[...])
    lane = lax.broadcasted_iota(I32, (tb, LANES), 1)
    lane_f = lane.astype(F32)
    work = logits
    idx_acc = jnp.zeros((tb, LANES), F32)
    val_acc = jnp.full((tb, LANES), NEG_INF, F32)
    hots = []
    for k in range(TOP_K):
        m = jnp.max(work, axis=-1, keepdims=True)
        idx = jnp.min(jnp.where(work == m, lane_f, float(LANES)), axis=-1, keepdims=True)
        hot = lane_f == idx
        hots.append(hot)
        idx_acc = jnp.where(lane == k, idx, idx_acc)
        val_acc = jnp.where(lane == k, m, val_acc)
        work = jnp.where(hot, NEG_INF, work)

    e = jnp.exp(val_acc - jnp.max(val_acc, axis=-1, keepdims=True))
    wgt_ref[...] = e / jnp.sum(e, axis=-1, keepdims=True)
    idx_ref[...] = idx_acc.astype(I32)

    chosen = jnp.zeros((tb, LANES), F32)
    for hot in hots:
        chosen = chosen + jnp.where(hot, 1.0, 0.0)
    row = lax.broadcasted_iota(I32, (tb, tb), 0)
    col = lax.broadcasted_iota(I32, (tb, tb), 1)
    earlier = jnp.where(col < row, 1.0, 0.0).astype(BF16)
    before = jnp.dot(earlier, chosen.astype(BF16), preferred_element_type=F32) + run_ref[...]
    rank_acc = jnp.zeros((tb, LANES), F32)
    for k, hot in enumerate(hots):
        rk = jnp.sum(jnp.where(hot, before, 0.0), axis=-1, keepdims=True)
        rank_acc = jnp.where(lane == k, rk, rank_acc)
    rank_ref[...] = rank_acc.astype(I32)
    run_ref[...] = run_ref[...] + jnp.sum(chosen, axis=0, keepdims=True)
    cnt_ref[...] = run_ref[...]


def _router(x2d, shift, scale, g, rw_pad, rb_pad, counts_in, *, rows_per_mod):
    n, d = x2d.shape
    tb = ROUTER_BLOCK
    per = rows_per_mod // tb
    vmem = 2 * tb * d * 4 + 6 * tb * d * 4 + 2 * d * LANES * 4
    outs = pl.pallas_call(
        _router_kernel,
        grid=(n // tb,),
        in_specs=[pl.BlockSpec((tb, d), lambda i: (i, 0)),
                  pl.BlockSpec((None, 1, d), lambda i: (i // per, 0, 0)),
                  pl.BlockSpec((None, 1, d), lambda i: (i // per, 0, 0)),
                  pl.BlockSpec((1, d), lambda i: (0, 0)),
                  pl.BlockSpec((d, 2 * LANES), lambda i: (0, 0)),
                  pl.BlockSpec((1, LANES), lambda i: (0, 0)),
                  pl.BlockSpec((1, LANES), lambda i: (0, 0))],
        out_specs=[pl.BlockSpec((tb * SUBLANES, LANES), lambda i: (i, 0)),
                   pl.BlockSpec((tb, LANES), lambda i: (i, 0)),
                   pl.BlockSpec((tb, LANES), lambda i: (i, 0)),
                   pl.BlockSpec((tb, LANES), lambda i: (i, 0)),
                   pl.BlockSpec((1, LANES), lambda i: (0, 0))],
        out_shape=[jax.ShapeDtypeStruct((n * SUBLANES, LANES), jnp.uint32),
                   jax.ShapeDtypeStruct((n, LANES), I32),
                   jax.ShapeDtypeStruct((n, LANES), F32),
                   jax.ShapeDtypeStruct((n, LANES), I32),
                   jax.ShapeDtypeStruct((1, LANES), F32)],
        scratch_shapes=[pltpu.VMEM((1, LANES), F32)],
        compiler_params=_params(("arbitrary",), vmem),
        name="router",
    )(x2d, shift, scale, g, rw_pad, rb_pad, counts_in)
    return outs


DISPATCH_BLOCK = 512


def _dispatch_kernel(fill_ref, slot_ref, hp_ref, xs_ref, zero_ref, sem, fill_sem):
    tb = hp_ref.shape[0] // SUBLANES
    t_rows = zero_ref.shape[0]

    @pl.when(pl.program_id(0) == 0)
    def _():
        zero_ref[...] = jnp.zeros_like(zero_ref)

        def tile_copy(t):
            return pltpu.make_async_copy(zero_ref, xs_ref.at[pl.ds(t * t_rows, t_rows), :], fill_sem)

        def fill(t, carry):
            @pl.when(fill_ref[t] == 1)
            def _():
                tile_copy(t).start()
            return carry

        def fill_wait(t, carry):
            @pl.when(fill_ref[t] == 1)
            def _():
                tile_copy(t).wait()
            return carry

        lax.fori_loop(0, fill_ref.shape[0], fill, 0)
        lax.fori_loop(0, fill_ref.shape[0], fill_wait, 0)

    def row_copy(j, s):
        src = j * SUBLANES if isinstance(j, int) else pl.multiple_of(j * SUBLANES, SUBLANES)
        dst = s * SUBLANES if isinstance(s, int) else pl.multiple_of(s * SUBLANES, SUBLANES)
        return pltpu.make_async_copy(hp_ref.at[pl.ds(src, SUBLANES), :],
                                     xs_ref.at[pl.ds(dst, SUBLANES), :], sem)

    def issue(j, carry):
        for k in range(TOP_K):
            row_copy(j, slot_ref[k, j]).start()
        return carry

    lax.fori_loop(0, tb, issue, 0)

    for k in range(TOP_K):
        pltpu.make_async_copy(hp_ref, xs_ref.at[pl.ds(0, tb * SUBLANES), :], sem).wait()


def _dispatch(fill_tiles, slots_t, hp, n_slots):
    n = hp.shape[0] // SUBLANES
    tb = DISPATCH_BLOCK
    assert n % tb == 0
    grid_spec = pltpu.PrefetchScalarGridSpec(
        num_scalar_prefetch=1,
        grid=(n // tb,),
        in_specs=[pl.BlockSpec((TOP_K, tb), lambda i, f: (0, i), memory_space=pltpu.SMEM),
                  pl.BlockSpec((tb * SUBLANES, LANES), lambda i, f: (i, 0))],
        out_specs=pl.BlockSpec(memory_space=pl.ANY),
        scratch_shapes=[pltpu.VMEM((MOE_TILE * SUBLANES, LANES), jnp.uint32),
                        pltpu.SemaphoreType.DMA(()), pltpu.SemaphoreType.DMA(())])
    return pl.pallas_call(
        _dispatch_kernel, grid_spec=grid_spec,
        out_shape=jax.ShapeDtypeStruct((n_slots * SUBLANES, LANES), jnp.uint32),
        compiler_params=_params(("arbitrary",), (2 * tb + MOE_TILE) * SUBLANES * LANES * 4),
        name="dispatch",
    )(fill_tiles, slots_t, hp)


EXPERT_TILES = 10
HIDDEN_CHUNK = 512


def _swiglu_pairs(a):
    even = (lax.broadcasted_iota(I32, (1, LANES), 1) % 2) == 0
    outs = []
    for c in range(a.shape[1] // (2 * LANES)):
        ga = a[:, (2 * c) * LANES:(2 * c + 1) * LANES]
        gb = a[:, (2 * c + 1) * LANES:(2 * c + 2) * LANES]
        glu = jnp.where(even, ga, pltpu.roll(gb, 1, 1))
        lin = jnp.where(even, pltpu.roll(ga, LANES - 1, 1), gb)
        glu = jnp.minimum(glu, SWIGLU_LIMIT)
        lin = jnp.clip(lin, -SWIGLU_LIMIT, SWIGLU_LIMIT)
        outs.append((glu * jax.nn.sigmoid(SWIGLU_ALPHA * glu) * (lin + 1.0)).astype(BF16))
    return outs[0] if len(outs) == 1 else jnp.concatenate(outs, axis=1)


def _interleaved_rows_bf16(w_ref, dst_ref):
    half = LANES // 2
    for qd in range(w_ref.shape[0] // LANES):
        wa = w_ref[qd * LANES:qd * LANES + half, :]
        wc = w_ref[qd * LANES + half:(qd + 1) * LANES, :]
        packed = pltpu.pack_elementwise([wa, wc], packed_dtype=BF16)
        dst_ref[qd * LANES:(qd + 1) * LANES, :] = pltpu.bitcast(packed, BF16)


def _expert_ffn_kernel(ie_ref, row0_ref, nt_ref, zt_ref, xs_ref, w1_ref, b1_ref, w2_ref, b2_ref, y_ref,
                       xb_ref, stage_ref, ybuf_ref, w1b_ref, w2b_ref, pend_ref,
                       sem_in, sem_out, *, n_chunks, n_items):
    del ie_ref
    i = pl.program_id(0)
    c = pl.program_id(1)
    t_rows = MOE_TILE
    nt = nt_ref[i]
    row0 = row0_ref[i]

    def out_copy(src_ref, r_src, r_dst):
        dst = r_dst if isinstance(r_dst, int) else pl.multiple_of(r_dst, t_rows)
        return pltpu.make_async_copy(src_ref.at[pl.ds(r_src, t_rows), :],
                                     y_ref.at[pl.ds(dst, t_rows), :], sem_out)

    def wait_pending():
        def body(_, carry):
            out_copy(ybuf_ref, 0, 0).wait()
            return carry
        lax.fori_loop(0, pend_ref[0], body, 0)
        pend_ref[0] = 0

    @pl.when((i == 0) & (c == 0))
    def _():
        pend_ref[0] = 0

    def in_copy(first_row, t):
        src = pl.multiple_of((first_row + t * t_rows) * SUBLANES, t_rows * SUBLANES)
        return pltpu.make_async_copy(xs_ref.at[pl.ds(src, t_rows * SUBLANES), :],
                                     stage_ref.at[t], sem_in.at[t])

    def request_tiles(item):
        def body(t, carry):
            in_copy(row0_ref[item], t).start()
            return carry
        lax.fori_loop(0, nt_ref[item], body, 0)

    @pl.when(c == 0)
    def _():
        @pl.when(i == 0)
        def _():
            request_tiles(0)

        def load(t, carry):
            slot = t
            in_copy(row0, t).wait()

            r = pl.multiple_of(t * t_rows, t_rows)
            half = xb_ref.shape[1] // 2
            for k in range(SUBLANES):
                xw = stage_ref[slot, pl.ds(k, t_rows, stride=SUBLANES), :]
                lo = pltpu.unpack_elementwise(xw, index=0, packed_dtype=BF16, unpacked_dtype=F32)
                hi = pltpu.unpack_elementwise(xw, index=1, packed_dtype=BF16, unpacked_dtype=F32)
                xb_ref[pl.ds(r, t_rows), k * LANES:(k + 1) * LANES] = lo.astype(BF16)
                xb_ref[pl.ds(r, t_rows), half + k * LANES:half + (k + 1) * LANES] = hi.astype(BF16)
            return carry

        lax.fori_loop(0, nt, load, 0)
        wait_pending()

        def zfill(t, carry):
            out_copy(ybuf_ref, 0, row0 + t * t_rows).start()
            return carry

        @pl.when(zt_ref[i] > 0)
        def _():
            ybuf_ref[0:t_rows, :] = jnp.zeros((t_rows, ybuf_ref.shape[1]), F32)

        lax.fori_loop(0, zt_ref[i], zfill, 0)
        pend_ref[0] = zt_ref[i]

    @pl.when((c == n_chunks - 1) & (i + 1 < n_items))
    def _():
        request_tiles(i + 1)

    @pl.when(nt > 0)
    def _():
        w1b_ref[...] = w1_ref[...].astype(BF16)
        _interleaved_rows_bf16(w2_ref, w2b_ref)
        last = c == n_chunks - 1

        def run(first):
            def rows_step(r, n_tiles):
                n_rows = n_tiles * t_rows
                xb = xb_ref[pl.ds(r, n_rows), :]
                a = jnp.dot(xb, w1b_ref[...], preferred_element_type=F32) + b1_ref[...]
                act = _swiglu_pairs(a)
                part = jnp.dot(act, w2b_ref[...], preferred_element_type=F32)
                if first:
                    ybuf_ref[pl.ds(r, n_rows), :] = part + b2_ref[...]
                else:
                    ybuf_ref[pl.ds(r, n_rows), :] += part

                if not first or n_chunks == 1:
                    @pl.when(last)
                    def _():
                        for k in range(n_tiles):
                            out_copy(ybuf_ref, r + k * t_rows, row0 + r + k * t_rows).start()

            def quad(p, carry):
                rows_step(pl.multiple_of(p * (4 * t_rows), 4 * t_rows), 4)
                return carry

            lax.fori_loop(0, nt // 4, quad, 0)

            @pl.when(nt % 4 >= 2)
            def _():
                rows_step(pl.multiple_of((nt // 4) * (4 * t_rows), 2 * t_rows), 2)

            @pl.when(nt % 2 == 1)
            def _():
                rows_step(pl.multiple_of((nt - 1) * t_rows, t_rows), 1)

        @pl.when(c == 0)
        def _():
            run(True)

        @pl.when(c != 0)
        def _():
            run(False)

        @pl.when(last)
        def _():
            pend_ref[0] = nt

    @pl.when((i == n_items - 1) & (c == n_chunks - 1))
    def _():
        wait_pending()


def _expert_ffn(item_tabs, xs, w1, b1, w2, b2, layer, max_items):
    p = xs.shape[0] // SUBLANES
    half = w1.shape[2] // 2
    _, e, d, n = w1.shape
    hc = HIDDEN_CHUNK
    n_chunks = (n // 2) // hc
    rows = EXPERT_TILES * MOE_TILE

    def chunk(c, nt, i):
        return jnp.where(nt[i] > 0, c, n_chunks - 1)

    grid_spec = pltpu.PrefetchScalarGridSpec(
        num_scalar_prefetch=4,
        grid=(max_items, n_chunks),
        in_specs=[pl.BlockSpec(memory_space=pl.ANY),
                  pl.BlockSpec((None, None, d, 2 * hc), lambda i, c, ie, r0, nt, zt: (layer, ie[i], 0, chunk(c, nt, i))),
                  pl.BlockSpec((None, None, 1, 2 * hc), lambda i, c, ie, r0, nt, zt: (layer, ie[i], 0, chunk(c, nt, i))),
                  pl.BlockSpec((None, None, hc, d), lambda i, c, ie, r0, nt, zt: (layer, ie[i], chunk(c, nt, i), 0)),
                  pl.BlockSpec((None, None, 1, d), lambda i, c, ie, r0, nt, zt: (layer, ie[i], 0, 0))],
        out_specs=pl.BlockSpec(memory_space=pl.ANY),
        scratch_shapes=[pltpu.VMEM((rows, d), BF16),
                        pltpu.VMEM((EXPERT_TILES, MOE_TILE * SUBLANES, LANES), jnp.uint32),
                        pltpu.VMEM((rows, d), F32),
                        pltpu.VMEM((d, 2 * hc), BF16),
                        pltpu.VMEM((hc, d), BF16),
                        pltpu.SMEM((1,), I32),
                        pltpu.SemaphoreType.DMA((EXPERT_TILES,)),
                        pltpu.SemaphoreType.DMA(())])
    vmem = (rows * d * 6 + 2 * MOE_TILE * half * 4 + MOE_TILE * d * 4
            + 3 * d * 2 * hc * 4 + d * 2 * hc * 2 + 3 * hc * d * 4 + hc * d * 2
            + 8 * MOE_TILE * 2 * hc * 4 + 4 * MOE_TILE * d * 4)
    depth = w1.shape[0]
    return pl.pallas_call(
        functools.partial(_expert_ffn_kernel, n_chunks=n_chunks, n_items=max_items), grid_spec=grid_spec,
        out_shape=jax.ShapeDtypeStruct((p, d), F32),
        compiler_params=_params(("arbitrary", "arbitrary"), vmem),
        name="expert_ffn",
    )(*item_tabs, xs, w1, b1.reshape(depth, e, 1, n), w2, b2.reshape(depth, e, 1, d))


COMBINE_BLOCK = 128


def _combine_kernel(slot_ref, slot_next_ref, y_ref, w_ref, x_ref, g_ref, o_ref, buf, sem, *, nblk):
    i = pl.program_id(0)
    tb = x_ref.shape[0]

    def row_copy(s, b, k, j):
        return pltpu.make_async_copy(y_ref.at[pl.ds(s, 1), :], buf.at[b, k, pl.ds(j, 1), :], sem.at[b])

    def issue(slots, b):
        def body(j8, carry):
            for jj in range(SUBLANES):
                j = j8 * SUBLANES + jj
                for k in range(TOP_K):
                    row_copy(slots[k, j], b, k, j).start()
            return carry
        lax.fori_loop(0, tb // SUBLANES, body, 0)

    @pl.when(i == 0)
    def _():
        issue(slot_ref, 0)

    @pl.when(i + 1 < nblk)
    def _():
        issue(slot_next_ref, (i + 1) % 2)

    cur = i % 2
    for k in range(TOP_K):
        pltpu.make_async_copy(y_ref.at[pl.ds(0, tb), :], buf.at[cur, k], sem.at[cur]).wait()

    w = w_ref[...]
    acc = w[:, 0:1] * buf[cur, 0]
    for k in range(1, TOP_K):
        acc = acc + w[:, k:k + 1] * buf[cur, k]
    o_ref[...] = x_ref[...] + g_ref[...] * acc


def _combine(slots_t, y, gates, x2d, gate_mod, *, rows_per_mod):
    n, d = x2d.shape
    tb = COMBINE_BLOCK
    nblk = n // tb
    per = rows_per_mod // tb
    vmem = 2 * TOP_K * tb * d * 4 + 6 * tb * d * 4
    return pl.pallas_call(
        functools.partial(_combine_kernel, nblk=nblk),
        grid=(nblk,),
        in_specs=[pl.BlockSpec((TOP_K, tb), lambda i: (0, i), memory_space=pltpu.SMEM),
                  pl.BlockSpec((TOP_K, tb), lambda i: (0, jnp.minimum(i + 1, nblk - 1)),
                               memory_space=pltpu.SMEM),
                  pl.BlockSpec(memory_space=pl.ANY),
                  pl.BlockSpec((tb, LANES), lambda i: (i, 0)),
                  pl.BlockSpec((tb, d), lambda i: (i, 0)),
                  pl.BlockSpec((None, 1, d), lambda i: (i // per, 0, 0))],
        out_specs=pl.BlockSpec((tb, d), lambda i: (i, 0)),
        out_shape=jax.ShapeDtypeStruct((n, d), F32),
        scratch_shapes=[pltpu.VMEM((2, TOP_K, tb, d), F32), pltpu.SemaphoreType.DMA((2,))],
        compiler_params=_params(("arbitrary",), vmem),
        name="combine",
    )(slots_t, slots_t, y, gates, x2d, gate_mod)


def _owner(ends, idx):
    return jnp.minimum(jnp.sum((ends[None, :] <= idx[:, None]).astype(I32), axis=1), ends.shape[0] - 1)


def _schedule(counts, max_tiles, max_items):
    t_rows = MOE_TILE
    tiles_e = (counts + t_rows - 1) // t_rows
    tile_end = jnp.cumsum(tiles_e)
    tile_start = tile_end - tiles_e
    num_tiles = tile_end[-1]
    t = jnp.arange(max_tiles, dtype=I32)
    te = _owner(tile_end, jnp.minimum(t, num_tiles - 1))
    fill = jnp.where((t < num_tiles) & (t != tile_end[te] - 1), 0, 1)

    items_e = (tiles_e + EXPERT_TILES - 1) // EXPERT_TILES
    item_end = jnp.cumsum(items_e)
    item_start = item_end - items_e
    n_items = item_end[-1]
    i = jnp.arange(max_items, dtype=I32)
    live = i < n_items
    ie = _owner(item_end, jnp.minimum(i, n_items - 1))
    local = jnp.minimum(i, n_items - 1) - item_start[ie]
    nt = jnp.where(live, jnp.clip(tiles_e[ie] - local * EXPERT_TILES, 0, EXPERT_TILES), 0)
    dead_tile0 = num_tiles + (i - n_items) * EXPERT_TILES
    zt = jnp.where(live, 0, jnp.clip(max_tiles - dead_tile0, 0, EXPERT_TILES))
    row0 = jnp.where(live, (tile_start[ie] + local * EXPERT_TILES) * t_rows,
                     jnp.clip(dead_tile0, 0, max_tiles - 1) * t_rows)
    return ((tile_start * t_rows).astype(I32), fill.astype(I32),
            (ie.astype(I32), row0.astype(I32), nt.astype(I32), zt.astype(I32)))


def _moe(parts, shift2, scale2, gate2, norm_g, router_w, router_b, w1, b1, w2, b2, layer):
    d = router_w.shape[0]
    rw_f32 = jnp.zeros((d, LANES), F32).at[:, :N_EXPERTS].set(router_w)
    rw_hi = lax.bitcast_convert_type(
        lax.bitcast_convert_type(rw_f32, jnp.uint32) & jnp.uint32(0xFFFF0000), F32)
    rw_pad = jnp.concatenate([rw_hi.astype(BF16), (rw_f32 - rw_hi).astype(BF16)], axis=1)
    rb_pad = jnp.full((1, LANES), NEG_INF, F32).at[0, :N_EXPERTS].set(router_b)
    counts = jnp.zeros((1, LANES), F32)
    routed = []
    for x2d, mod_rows, per in parts:
        hp, idx, wgt, rank, counts = _router(
            x2d, shift2[mod_rows][:, None, :], scale2[mod_rows][:, None, :], norm_g[None, :],
            rw_pad, rb_pad, counts, rows_per_mod=per)
        routed.append((hp, idx[:, :TOP_K], wgt, rank[:, :TOP_K]))
    n_total = sum(x2d.shape[0] for x2d, _, _ in parts)
    max_tiles = (n_total * TOP_K) // MOE_TILE + N_EXPERTS
    n_slots = max_tiles * MOE_TILE
    max_items = N_EXPERTS + -(-max_tiles // EXPERT_TILES)
    group_start, fill_tiles, tabs = _schedule(counts[0, :N_EXPERTS].astype(I32), max_tiles, max_items)

    slots = [(group_start[idx] + rank).T for _, idx, _, rank in routed]
    hp_all = jnp.concatenate([r[0] for r in routed], axis=0) if len(routed) > 1 else routed[0][0]
    slots_all = jnp.concatenate(slots, axis=1) if len(slots) > 1 else slots[0]
    xs = _dispatch(fill_tiles, slots_all, hp_all, n_slots)
    y = _expert_ffn(tabs, xs, w1, b1, w2, b2, layer, max_items)
    outs = []
    for (x2d, mod_rows, per), sl, (_, _, wgt, _) in zip(parts, slots, routed):
        outs.append(_combine(sl, y, wgt, x2d, gate2[mod_rows][:, None, :], rows_per_mod=per))
    return outs


def _rope_tables(seq):
    t = np.arange(seq)
    m = HEAD_DIM // 4
    inv_freq = ROPE_THETA ** (-np.arange(m, dtype=np.float64) / m)
    ang_r = (t // GRID_W)[:, None] * inv_freq[None, :]
    ang_c = (t % GRID_W)[:, None] * inv_freq[None, :]
    cos = np.concatenate([np.cos(ang_r)] * 2 + [np.cos(ang_c)] * 2, axis=-1)
    sin = np.concatenate([-np.sin(ang_r), np.sin(ang_r), -np.sin(ang_c), np.sin(ang_c)], axis=-1)
    return jnp.asarray(cos, F32), jnp.asarray(sin, F32)


def kernel(x, c, ctx, c_ctx, ada_w, ada_b, norm_mix_g, norm_ffn_g, a_w_qkv, a_w_o, a_q_gain, a_k_gain,
           b_w_qkv, b_w_o, b_q_gain, b_k_gain, b_rel_bias, router_w, router_b, exp_w1, exp_b1, exp_w2, exp_b2):
    bsz, seq, d = x.shape
    l_ctx = ctx.shape[1]
    depth = ada_w.shape[0]
    n_heads = d // HEAD_DIM
    q_scale = HEAD_DIM ** -0.5 * LOG2E

    c_rows = jnp.zeros((SUBLANES, d), F32).at[:bsz].set(c).at[bsz].set(c_ctx)
    mod = _ada_modulation(c_rows, ada_w, ada_b)
    cos, sin = _rope_tables(seq)
    ones_tab = jnp.ones((l_ctx, HEAD_DIM), F32)
    x_rows = jnp.arange(bsz)
    c_rows_idx = jnp.full((bsz,), bsz)

    for i in range(depth):
        last = i == depth - 1
        sh1, sc1, g1, sh2, sc2, g2 = [mod[i, :, k * d:(k + 1) * d] for k in range(N_MOD)]
        j = i // 2
        if i % 2 == 0:
            w_qkv, w_o, qg, kg = a_w_qkv[j], a_w_o[j], a_q_gain[j], a_k_gain[j]
            n_q, n_k, n_v = n_heads, A_KV_HEADS, A_KV_HEADS
        else:
            w_qkv, w_o, qg, kg = b_w_qkv[j], b_w_o[j], b_q_gain[j], b_k_gain[j]
            n_q, n_k, n_v = n_heads, n_heads, n_heads
        heads_per_tile = 512 // HEAD_DIM
        gain_cols = jnp.concatenate([jnp.tile(qg * q_scale, n_q), jnp.tile(kg, n_k),
                                     jnp.ones((n_v * HEAD_DIM,), F32)])[None, :]
        flags = jnp.concatenate([jnp.ones(((n_q + n_k) // heads_per_tile,), I32),
                                 jnp.zeros((n_v // heads_per_tile,), I32)])
        rope = i % 2 == 0
        qkv_x = _qkv_project(x, sh1[x_rows][:, None, :], sc1[x_rows][:, None, :], norm_mix_g[i][None, :],
                             w_qkv, gain_cols, flags, cos, sin, rope=rope, tm=1024)
        qkv_c = _qkv_project(ctx, sh1[c_rows_idx][:, None, :], sc1[c_rows_idx][:, None, :],
                             norm_mix_g[i][None, :], w_qkv, gain_cols, flags, ones_tab, ones_tab,
                             rope=False, tm=l_ctx)
        if i % 2 == 0:
            ox = _attention(qkv_x, qkv_c, qkv_x, n_q_heads=n_q, n_kv_heads=n_k,
                            k_col0=n_q, v_col0=n_q + n_k, tq=256)
        else:
            ox = _neighborhood_attention(qkv_x, qkv_c, b_rel_bias[j], n_heads=n_heads)
        x = _out_project(ox, w_o, x, g1[x_rows][:, None, :], tm=1024)
        parts = [(x.reshape(bsz * seq, d), x_rows, seq)]
        if not last:
            oc = _attention(qkv_c, qkv_c, None, n_q_heads=n_q, n_kv_heads=n_k,
                            k_col0=n_q, v_col0=n_q + n_k, tq=l_ctx)
            ctx = _out_project(oc, w_o, ctx, g1[c_rows_idx][:, None, :], tm=l_ctx)
            parts.append((ctx.reshape(bsz * l_ctx, d), jnp.full((1,), bsz), bsz * l_ctx))
        outs = _moe(parts, sh2, sc2, g2, norm_ffn_g[i], router_w[i], router_b[i],
                    exp_w1, exp_b1, exp_w2, exp_b2, i)
        x = outs[0].reshape(bsz, seq, d)
        if not last:
            ctx = outs[1].reshape(bsz, l_ctx, d)
    return x
```
